```python
import jax
import jax.numpy as jnp
from jax import lax
import numpy as np

D_MODEL = 2048
BATCH = 16
SEQ = 2048
DEPTH = 2

GRID_W = 64
CTX_LEN = 256
F32 = jnp.float32

ATT_HEAD_DIM = 128
ATT_HEADS = D_MODEL // 256
ATT_KV_HEADS = ATT_HEADS // 4
ATT_W = ATT_HEADS * ATT_HEAD_DIM
ATT_KV_W = ATT_KV_HEADS * ATT_HEAD_DIM
ATT_SCALE = ATT_HEAD_DIM ** -0.5
ROPE_THETA = 10000.0
ROPE_FREQS = ATT_HEAD_DIM // 4
Q_BLOCK = 128

RWKV_HEAD_DIM = 64
RWKV_W = D_MODEL // 2
RWKV_HEADS = RWKV_W // RWKV_HEAD_DIM
DECAY_RANK = 64
ICLR_RANK = 64
GATE_RANK = 128
RWKV_SPLITS = (RWKV_W, RWKV_W, RWKV_W, 2 * DECAY_RANK, 2 * ICLR_RANK, GATE_RANK)
RWKV_IN_W = sum(RWKV_SPLITS)
RWKV_OFFSETS = tuple(int(o) for o in np.cumsum(RWKV_SPLITS)[:-1])
GN_EPS = 64e-5

CONV_W = D_MODEL // 2
CONV_K = 31

N_BRANCH = 3
IN_SPLITS = (ATT_W, ATT_KV_W, ATT_KV_W, RWKV_IN_W, 2 * CONV_W, N_BRANCH * D_MODEL)
IN_W = sum(IN_SPLITS)
IN_OFFSETS = tuple(int(o) for o in np.cumsum(IN_SPLITS)[:-1])

N_GROUPS = 4
EXPERTS_PER_GROUP = 8
N_EXPERTS = N_GROUPS * EXPERTS_PER_GROUP
TOP_K = 2
EXPERT_FF = D_MODEL // 2
MOE_BLOCK = 128

DN_ALPHA = (2 * DEPTH) ** 0.25
DN_BETA = (8 * DEPTH) ** -0.25
LN_EPS = 1e-6
RMS_EPS = 1e-6

kernel_name = 'hybrid_rwkv7_conformer_axialgqa_hmoe'


def _ln(x):
    xf = x.astype(F32)
    mu = jnp.mean(xf, -1, keepdims=True)
    var = jnp.mean(jnp.square(xf - mu), -1, keepdims=True)
    return (xf - mu) * lax.rsqrt(var + LN_EPS)


def _layernorm(x, g, b):
    return (_ln(x) * g + b).astype(x.dtype)


def _modulate(x, shift, scale):
    return (_ln(x) * (1.0 + scale) + shift).astype(x.dtype)


def _adaln(cvec, p):
    m = jax.nn.silu(cvec) @ p['w_mod'] + p['b_mod']
    return jnp.split(m, 6, axis=-1)


def _heads(t, n):
    return t.reshape(*t.shape[:-1], n, t.shape[-1] // n)


def _rms(t, g):
    tf = t.astype(F32)
    return tf * lax.rsqrt(jnp.mean(tf * tf, -1, keepdims=True) + RMS_EPS) * g


def _axial_rope_tables(n_tokens):
    rows = n_tokens // GRID_W
    row = jnp.repeat(jnp.arange(rows, dtype=F32), GRID_W)
    col = jnp.tile(jnp.arange(GRID_W, dtype=F32), rows)
    inv = ROPE_THETA ** (-jnp.arange(ROPE_FREQS, dtype=F32) / ROPE_FREQS)
    ang = jnp.concatenate([row[:, None] * inv, col[:, None] * inv], -1)
    return jnp.cos(ang)[:, None, :], jnp.sin(ang)[:, None, :]


def _rot(t, cos, sin):
    t1, t2 = jnp.split(t, 2, -1)
    return jnp.concatenate([t1 * cos - t2 * sin, t2 * cos + t1 * sin], -1)


def _apply_axial_rope(t, cos, sin):
    tr, tc = jnp.split(t, 2, -1)
    cr, cc = jnp.split(cos, 2, -1)
    sr, sc = jnp.split(sin, 2, -1)
    return jnp.concatenate([_rot(tr, cr, sr), _rot(tc, cc, sc)], -1)


def _gqa(q):
    return q.reshape(*q.shape[:2], ATT_KV_HEADS, ATT_HEADS // ATT_KV_HEADS, ATT_HEAD_DIM)


def _attend(q, k, v):
    s = jnp.einsum('bqhgd,bkhd->bhgqk', q, k)
    w = jax.nn.softmax(s, axis=-1).astype(v.dtype)
    return jnp.einsum('bhgqk,bkhd->bqhgd', w, v)


def _blocked_attend(q, k, v):
    B, T = q.shape[:2]
    nb = T // Q_BLOCK
    qb = jnp.swapaxes(q.reshape(B, nb, Q_BLOCK, *q.shape[2:]), 0, 1)
    ob = lax.map(lambda qi: _attend(qi, k, v), qb)
    return jnp.swapaxes(ob, 0, 1).reshape(q.shape)


def _att_out(o, p):
    return jnp.einsum('btc,cd->btd', o.reshape(*o.shape[:2], ATT_W), p['w_att_o'])


def _token_shift(z, mu):
    prev = jnp.pad(z[:, :-1], ((0, 0), (1, 0), (0, 0)))
    nxt = jnp.pad(z[:, 1:], ((0, 0), (0, 1), (0, 0)))
    return z + mu[0] * (prev - z) + mu[1] * (nxt - z)


def _rwkv_inputs(z, p):
    B, T, _ = z.shape
    z = _token_shift(z.astype(F32), p['rwkv_mu'])
    r, k, v, wl, al, gl = jnp.split(z, RWKV_OFFSETS, axis=-1)
    wl = wl.reshape(B, T, 2, DECAY_RANK)
    al = al.reshape(B, T, 2, ICLR_RANK)
    w = -jax.nn.softplus(-(p['rwkv_w0'] + jnp.einsum('btdr,drc->btdc', jnp.tanh(wl), p['rwkv_w2']))) - 0.5
    decay = jnp.exp(-jnp.exp(w))
    a = jax.nn.sigmoid(p['rwkv_a0'] + jnp.einsum('btdr,drc->btdc', al, p['rwkv_a2']))
    g = jnp.einsum('btr,rc->btc', jax.nn.sigmoid(gl), p['rwkv_g2'])
    kk = _heads(k * p['rwkv_k_k'], RWKV_HEADS)
    kk = kk / jnp.maximum(jnp.sqrt(jnp.sum(kk * kk, -1, keepdims=True)), 1e-12)
    kd = k[:, :, None, :] * (1.0 + (a - 1.0) * p['rwkv_k_a'])
    H = lambda t: _heads(t, RWKV_HEADS)
    return (H(r), H(v), kk, H(kd), H(decay), H(a), g)


def _rwkv_scan(state0, r, v, kk, kd, decay, a):
    def both(t):
        tt = jnp.moveaxis(t, 1, 0)
        return jnp.stack([tt, tt[::-1]], axis=1)

    def per_dir(t):
        tt = jnp.moveaxis(t, 1, 0)
        return jnp.stack([tt[:, :, 0], tt[::-1, :, 1]], axis=1)

    xs = (both(r), both(v), both(kk), per_dir(kd), per_dir(decay), per_dir(kk[:, :, None] * a))

    def step(S, inp):
        r_t, v_t, kk_t, k_t, w_t, b_t = inp
        sa = jnp.einsum('dbhij,dbhj->dbhi', S, -kk_t)
        S = S * w_t[..., None, :] + sa[..., :, None] * b_t[..., None, :] + v_t[..., :, None] * k_t[..., None, :]
        return S, jnp.einsum('dbhij,dbhj->dbhi', S, r_t)

    S, y = lax.scan(step, state0, xs)
    y = y[:, 0] + y[::-1, 1]
    return S, jnp.moveaxis(y, 0, 1)


def _rwkv_out(y, ins, p, dtype):
    r, v, _, kd, _, _, g = ins
    B, T = y.shape[:2]
    mu = jnp.mean(y, -1, keepdims=True)
    var = jnp.mean(jnp.square(y - mu), -1, keepdims=True)
    yn = ((y - mu) * lax.rsqrt(var + GN_EPS)).reshape(B, T, RWKV_W) * p['rwkv_ln_g'] + p['rwkv_ln_b']
    bonus = jnp.sum(r[:, :, None] * kd * p['rwkv_r_k'], -1, keepdims=True) * v[:, :, None]
    yo = (yn + jnp.sum(bonus, 2).reshape(B, T, RWKV_W)) * g
    return jnp.einsum('btc,cd->btd', yo.astype(dtype), p['w_rwkv_o'])


def _conv_module(z, p):
    u = z[..., :CONV_W] * jax.nn.sigmoid(z[..., CONV_W:])
    u = lax.conv_general_dilated(u, p['conv_w'][:, None, :], window_strides=(1,),
                                 padding=((CONV_K // 2, CONV_K // 2),),
                                 dimension_numbers=('NWC', 'WIO', 'NWC'),
                                 feature_group_count=CONV_W) + p['conv_b']
    u = jax.nn.silu(_layernorm(u, p['conv_ln_g'], p['conv_ln_b']))
    return jnp.einsum('btc,cd->btd', u, p['w_conv_o'])


def _merge(gt, ys, p):
    B, T, _ = gt.shape
    g = jax.nn.sigmoid(gt + p['b_gate']).reshape(B, T, N_BRANCH, D_MODEL)
    m = g[:, :, 0] * ys[0] + g[:, :, 1] * ys[1] + g[:, :, 2] * ys[2]
    return jnp.einsum('btd,de->bte', m, p['w_out'])


def _mixer(h, hc, p, need_ctx):
    B, S, _ = h.shape
    dt = h.dtype
    qa, ka, va, rwa, cva, gta = jnp.split(jnp.einsum('btd,de->bte', h, p['w_in']), IN_OFFSETS, -1)
    qc, kc, vc, rwc, cvc, gtc = jnp.split(jnp.einsum('btd,de->bte', hc, p['w_in']), IN_OFFSETS, -1)

    cos, sin = _axial_rope_tables(S)
    q = _apply_axial_rope(_rms(_heads(qa, ATT_HEADS), p['q_norm']), cos, sin) * ATT_SCALE
    k = _apply_axial_rope(_rms(_heads(ka, ATT_KV_HEADS), p['k_norm']), cos, sin)
    k_ctx = _rms(_heads(kc, ATT_KV_HEADS), p['k_norm'])
    v_ctx = _heads(vc, ATT_KV_HEADS)
    k_all = jnp.concatenate([k, k_ctx], 1)
    v_all = jnp.concatenate([_heads(va, ATT_KV_HEADS), v_ctx], 1)
    y_att = _att_out(_blocked_attend(_gqa(q), k_all, v_all), p)

    r_c = _rwkv_inputs(rwc, p)
    state0 = jnp.zeros((2, B, RWKV_HEADS, RWKV_HEAD_DIM, RWKV_HEAD_DIM), F32)
    state_ctx, yr_c = _rwkv_scan(state0, *r_c[:6])
    r_l = _rwkv_inputs(rwa, p)
    _, yr_l = _rwkv_scan(state_ctx, *r_l[:6])
    y_rwkv = _rwkv_out(yr_l, r_l, p, dt)

    y_conv = _conv_module(cva, p)

    y = _merge(gta, (y_att, y_rwkv, y_conv), p)
    if not need_ctx:
        return y, None
    q_c = _rms(_heads(qc, ATT_HEADS), p['q_norm']) * ATT_SCALE
    yc_att = _att_out(_attend(_gqa(q_c), k_ctx, v_ctx), p)
    yc_rwkv = _rwkv_out(yr_c, r_c, p, dt)
    yc_conv = _conv_module(cvc, p)
    yc = _merge(gtc, (yc_att, yc_rwkv, yc_conv), p)
    return y, yc


def _moe(xt, p):
    n, D = xt.shape
    rows = jnp.arange(n)
    hf = xt.astype(F32)
    grp_logits = hf @ p['w_group'] + p['b_group']
    grp = jnp.argmax(grp_logits, -1)
    grp_w = jax.nn.softmax(grp_logits, -1)[rows, grp][:, None]
    exp_logits = (hf @ p['w_router'] + p['b_router']).reshape(n, N_GROUPS, EXPERTS_PER_GROUP)
    in_grp = exp_logits[rows, grp]
    top_p, top_i = lax.top_k(jax.nn.softmax(in_grp, -1), TOP_K)
    wts = grp_w * top_p / jnp.sum(top_p, -1, keepdims=True)
    ids = grp[:, None] * EXPERTS_PER_GROUP + top_i

    A = n * TOP_K
    flat_e = ids.reshape(-1)
    flat_tok = jnp.repeat(rows, TOP_K)
    flat_w = wts.reshape(-1)
    order = jnp.argsort(flat_e)
    se = flat_e[order]
    counts = jnp.bincount(flat_e, length=N_EXPERTS)
    padded = (counts + MOE_BLOCK - 1) // MOE_BLOCK * MOE_BLOCK
    pad_end = jnp.cumsum(padded)
    pad_start = pad_end - padded
    start = jnp.cumsum(counts) - counts
    dest = pad_start[se] + jnp.arange(A) - start[se]
    n_blocks = -(-A // MOE_BLOCK) + N_EXPERTS
    slot_tok = jnp.full((n_blocks * MOE_BLOCK,), n, jnp.int32).at[dest].set(flat_tok[order].astype(jnp.int32))
    slot_w = jnp.zeros((n_blocks * MOE_BLOCK,), F32).at[dest].set(flat_w[order])
    block_e = jnp.minimum(jnp.searchsorted(pad_end, jnp.arange(n_blocks) * MOE_BLOCK, side='right'), N_EXPERTS - 1)
    x_pad = jnp.concatenate([xt, jnp.zeros((1, D), xt.dtype)], 0)

    def run_block(args):
        tok, e = args
        xb = x_pad[tok]
        hdn = jax.nn.silu(xb @ p['w_e_gate'][e]) * (xb @ p['w_e_up'][e])
        return hdn @ p['w_e_down'][e]

    yb = lax.map(run_block, (slot_tok.reshape(n_blocks, MOE_BLOCK), block_e))
    y = jnp.zeros((n + 1, D), F32).at[slot_tok].add(yb.reshape(-1, D).astype(F32) * slot_w[:, None])
    return y[:n].astype(xt.dtype)


def _layer(x, xc, c, c_ctx, p, last):
    B, S, D = x.shape
    sh1, sc1, gm1, sh2, sc2, gm2 = _adaln(c, p)
    csh1, csc1, cgm1, csh2, csc2, cgm2 = _adaln(c_ctx, p)
    h = _modulate(x, sh1[:, None], sc1[:, None])
    hc = _modulate(xc, csh1, csc1)
    y, yc = _mixer(h, hc, p, not last)
    x = _layernorm(DN_ALPHA * x + gm1[:, None] * y, p['ln1_g'], p['ln1_b'])
    h2 = _modulate(x, sh2[:, None], sc2[:, None]).reshape(B * S, D)
    if last:
        f = _moe(h2, p)
    else:
        xc = _layernorm(DN_ALPHA * xc + cgm1 * yc, p['ln1_g'], p['ln1_b'])
        hc2 = _modulate(xc, csh2, csc2).reshape(-1, D)
        f_all = _moe(jnp.concatenate([h2, hc2], 0), p)
        f = f_all[:B * S]
        xc = _layernorm(DN_ALPHA * xc + cgm2 * f_all[B * S:].reshape(xc.shape), p['ln2_g'], p['ln2_b'])
    x = _layernorm(DN_ALPHA * x + gm2[:, None] * f.reshape(B, S, D), p['ln2_g'], p['ln2_b'])
    return x, xc


def setup_inputs(seed: int = 0) -> dict:
    key = jax.random.key(seed)
    ks = iter(jax.random.split(key, 64))
    D, L = D_MODEL, DEPTH

    def nrm(shape, scale):
        return scale * jax.random.normal(next(ks), shape, F32)

    def near_one(shape):
        return 1.0 + nrm(shape, 0.02)

    return {
        'x': nrm((BATCH, SEQ, D), 1.0),
        'c': nrm((BATCH, D), 1.0),
        'ctx': nrm((BATCH, CTX_LEN, D), 1.0),
        'c_ctx': nrm((D,), 1.0),
        'w_mod': nrm((L, D, 6 * D), 0.5 * D ** -0.5),
        'b_mod': nrm((L, 6 * D), 0.01),
        'w_in': nrm((L, D, IN_W), D ** -0.5),
        'b_gate': nrm((L, N_BRANCH * D), 0.01),
        'q_norm': near_one((L, ATT_HEAD_DIM)),
        'k_norm': near_one((L, ATT_HEAD_DIM)),
        'w_att_o': nrm((L, ATT_W, D), ATT_W ** -0.5),
        'rwkv_mu': jax.random.uniform(next(ks), (L, 2, RWKV_IN_W), F32, 0.0, 0.5),
        'rwkv_w0': jax.random.uniform(next(ks), (L, 2, RWKV_W), F32, -6.0, -1.0),
        'rwkv_w2': nrm((L, 2, DECAY_RANK, RWKV_W), 0.1),
        'rwkv_a0': nrm((L, 2, RWKV_W), 0.1),
        'rwkv_a2': nrm((L, 2, ICLR_RANK, RWKV_W), 0.1 * ICLR_RANK ** -0.5),
        'rwkv_g2': nrm((L, GATE_RANK, RWKV_W), GATE_RANK ** -0.5),
        'rwkv_k_k': 0.85 + nrm((L, RWKV_W), 0.02),
        'rwkv_k_a': near_one((L, RWKV_W)),
        'rwkv_r_k': nrm((L, RWKV_HEADS, RWKV_HEAD_DIM), 0.1),
        'rwkv_ln_g': near_one((L, RWKV_W)),
        'rwkv_ln_b': nrm((L, RWKV_W), 0.01),
        'w_rwkv_o': nrm((L, RWKV_W, D), RWKV_W ** -0.5),
        'conv_w': nrm((L, CONV_K, CONV_W), CONV_K ** -0.5),
        'conv_b': nrm((L, CONV_W), 0.01),
        'conv_ln_g': near_one((L, CONV_W)),
        'conv_ln_b': nrm((L, CONV_W), 0.01),
        'w_conv_o': nrm((L, CONV_W, D), CONV_W ** -0.5),
        'w_out': nrm((L, D, D), DN_BETA * D ** -0.5),
        'ln1_g': near_one((L, D)),
        'ln1_b': nrm((L, D), 0.01),
        'w_group': nrm((L, D, N_GROUPS), D ** -0.5),
        'b_group': nrm((L, N_GROUPS), 0.01),
        'w_router': nrm((L, D, N_EXPERTS), D ** -0.5),
        'b_router': nrm((L, N_EXPERTS), 0.01),
        'w_e_gate': nrm((L, N_EXPERTS, D, EXPERT_FF), D ** -0.5),
        'w_e_up': nrm((L, N_EXPERTS, D, EXPERT_FF), D ** -0.5),
        'w_e_down': nrm((L, N_EXPERTS, EXPERT_FF, D), DN_BETA * EXPERT_FF ** -0.5),
        'ln2_g': near_one((L, D)),
        'ln2_b': nrm((L, D), 0.01),
    }


def reference(x, c, ctx, c_ctx, w_mod, b_mod, w_in, b_gate, q_norm, k_norm, w_att_o,
              rwkv_mu, rwkv_w0, rwkv_w2, rwkv_a0, rwkv_a2, rwkv_g2, rwkv_k_k, rwkv_k_a, rwkv_r_k,
              rwkv_ln_g, rwkv_ln_b, w_rwkv_o, conv_w, conv_b, conv_ln_g, conv_ln_b, w_conv_o,
              w_out, ln1_g, ln1_b, w_group, b_group, w_router, b_router, w_e_gate, w_e_up,
              w_e_down, ln2_g, ln2_b):
    xc = ctx
    for l in range(DEPTH):
        p = dict(w_mod=w_mod[l], b_mod=b_mod[l], w_in=w_in[l], b_gate=b_gate[l],
                 q_norm=q_norm[l], k_norm=k_norm[l], w_att_o=w_att_o[l],
                 rwkv_mu=rwkv_mu[l], rwkv_w0=rwkv_w0[l], rwkv_w2=rwkv_w2[l], rwkv_a0=rwkv_a0[l],
                 rwkv_a2=rwkv_a2[l], rwkv_g2=rwkv_g2[l], rwkv_k_k=rwkv_k_k[l], rwkv_k_a=rwkv_k_a[l],
                 rwkv_r_k=rwkv_r_k[l], rwkv_ln_g=rwkv_ln_g[l], rwkv_ln_b=rwkv_ln_b[l],
                 w_rwkv_o=w_rwkv_o[l], conv_w=conv_w[l], conv_b=conv_b[l], conv_ln_g=conv_ln_g[l],
                 conv_ln_b=conv_ln_b[l], w_conv_o=w_conv_o[l], w_out=w_out[l],
                 ln1_g=ln1_g[l], ln1_b=ln1_b[l], w_group=w_group[l], b_group=b_group[l],
                 w_router=w_router[l], b_router=b_router[l], w_e_gate=w_e_gate[l],
                 w_e_up=w_e_up[l], w_e_down=w_e_down[l], ln2_g=ln2_g[l], ln2_b=ln2_b[l])
        x, xc = _layer(x, xc, c, c_ctx, p, l == DEPTH - 1)
    return x
```

```python
import functools
import math

import numpy as np
import jax
import jax.numpy as jnp
from jax import lax
from jax.experimental import pallas as pl
from jax.experimental.pallas import tpu as pltpu

F32 = jnp.float32
BF16 = jnp.bfloat16
HI = lax.Precision.HIGHEST

LANE = 128
SUBLANE = 8
ATT_HEAD_DIM = 128
ATT_GROUP = 4
ROPE_THETA = 10000.0
GRID_W = 64
RWKV_HEAD_DIM = 64
CHUNK = 64
CONV_HALO = 16
SHIFT_HALO = 8
N_GROUPS = 4
EXPERTS_PER_GROUP = 8
N_EXPERTS = N_GROUPS * EXPERTS_PER_GROUP
MOE_BLOCK = 256
GN_EPS = 64e-5
LN_EPS = 1e-6
RMS_EPS = 1e-6
VMEM_LIMIT = 48 * 1024 * 1024

NT_DIMS = (((1,), (1,)), ((), ()))
TN_DIMS = (((0,), (0,)), ((), ()))


def _params(*sem):
    return pltpu.CompilerParams(dimension_semantics=sem, vmem_limit_bytes=VMEM_LIMIT)


def _sigmoid(x):
    return 1.0 / (1.0 + jnp.exp(-x))


def _ln(x):
    mu = jnp.mean(x, -1, keepdims=True)
    xc = x - mu
    var = jnp.mean(xc * xc, -1, keepdims=True)
    return xc * lax.rsqrt(var + LN_EPS)


def _tile(n, pref):
    t = min(n, pref)
    while n % t:
        t -= SUBLANE
    return t


def _adaln_kernel(c_ref, w_ref, b_ref, o_ref):
    c = c_ref[...]
    s = c * _sigmoid(c)
    o_ref[...] = jnp.dot(s, w_ref[...], precision=HI, preferred_element_type=F32) + b_ref[...]


def _adaln(cv, w_mod, b_mod):
    R, D = cv.shape
    W = w_mod.shape[1]
    tn = _tile(W, 1024)
    return pl.pallas_call(
        _adaln_kernel,
        grid=(W // tn,),
        in_specs=[pl.BlockSpec((R, D), lambda j: (0, 0)),
                  pl.BlockSpec((D, tn), lambda j: (0, j)),
                  pl.BlockSpec((1, tn), lambda j: (0, j))],
        out_specs=pl.BlockSpec((R, tn), lambda j: (0, j)),
        out_shape=jax.ShapeDtypeStruct((R, W), F32),
        compiler_params=_params("parallel"),
        name="adaln",
    )(cv, w_mod, b_mod.reshape(1, W))


def _lnmod_kernel(x_ref, sh_ref, sc_ref, o_ref):
    o_ref[...] = (_ln(x_ref[...]) * (1.0 + sc_ref[0]) + sh_ref[0]).astype(o_ref.dtype)


def _lnmod(xa, mods3, shift_blk, scale_blk, rows, modrow, tm):
    D = xa.shape[1]
    return pl.pallas_call(
        _lnmod_kernel,
        grid=(rows // tm,),
        in_specs=[pl.BlockSpec((tm, D), lambda i: (i, 0)),
                  pl.BlockSpec((1, 1, D), lambda i: (modrow(i), 0, shift_blk)),
                  pl.BlockSpec((1, 1, D), lambda i: (modrow(i), 0, scale_blk))],
        out_specs=pl.BlockSpec((tm, D), lambda i: (i, 0)),
        out_shape=jax.ShapeDtypeStruct((rows, D), BF16),
        compiler_params=_params("parallel"),
        name="lnmod",
    )(xa, mods3, mods3)


def _mm_kernel(x_ref, w_ref, o_ref):
    o_ref[...] = jnp.dot(x_ref[...], w_ref[...], preferred_element_type=F32).astype(o_ref.dtype)


def _matmul(x, w, tn_pref, rows=None, out_dtype=F32, tm_pref=1024):
    M, K = x.shape
    M = rows or M
    N = w.shape[1]
    tm = _tile(M, tm_pref)
    tn = _tile(N, tn_pref)
    return pl.pallas_call(
        _mm_kernel,
        grid=(N // tn, M // tm),
        in_specs=[pl.BlockSpec((tm, K), lambda j, i: (i, 0)),
                  pl.BlockSpec((K, tn), lambda j, i: (0, j))],
        out_specs=pl.BlockSpec((tm, tn), lambda j, i: (i, j)),
        out_shape=jax.ShapeDtypeStruct((M, N), out_dtype),
        compiler_params=_params("parallel", "parallel"),
        name="matmul",
    )(x, w)


def _rms(t, g):
    return t * lax.rsqrt(jnp.mean(t * t, -1, keepdims=True) + RMS_EPS) * g


def _rope(t, cos, sin_signed):
    lane = lax.broadcasted_iota(jnp.int32, t.shape, 1)
    first = (lane & 63) < 32
    partner = jnp.where(first, pltpu.roll(t, LANE - 32, 1), pltpu.roll(t, 32, 1))
    return t * cos + partner * sin_signed


def _attn_kernel(*refs, rope, has_lat, tq, scale):
    if has_lat:
        (q_ref, kc_ref, vc_ref, qn_ref, kn_ref, kl_ref, vl_ref, cq_ref, sq_ref, ck_ref, sk_ref,
         o_ref, kcb, vcb, klb, vlb) = refs
    else:
        q_ref, kc_ref, vc_ref, qn_ref, kn_ref, o_ref, kcb, vcb = refs

    @pl.when(pl.program_id(2) == 0)
    def _():
        kcb[...] = _rms(kc_ref[...], kn_ref[...]).astype(BF16)
        vcb[...] = vc_ref[...].astype(BF16)
        if has_lat:
            klb[...] = _rope(_rms(kl_ref[...], kn_ref[...]), ck_ref[...], sk_ref[...]).astype(BF16)
            vlb[...] = vl_ref[...].astype(BF16)

    qs = []
    for g in range(ATT_GROUP):
        qg = _rms(q_ref[:, g * LANE:(g + 1) * LANE], qn_ref[...])
        if rope:
            qg = _rope(qg, cq_ref[...], sq_ref[...])
        qs.append((qg * scale).astype(BF16))
    qa = jnp.concatenate(qs, axis=0)
    sc = lax.dot_general(qa, kcb[...], NT_DIMS, preferred_element_type=F32)
    m = jnp.max(sc, -1, keepdims=True)
    if has_lat:
        sl = lax.dot_general(qa, klb[...], NT_DIMS, preferred_element_type=F32)
        m = jnp.maximum(m, jnp.max(sl, -1, keepdims=True))
    pc = jnp.exp(sc - m)
    den = jnp.sum(pc, -1, keepdims=True)
    o = jnp.dot(pc.astype(BF16), vcb[...], preferred_element_type=F32)
    if has_lat:
        pl_ = jnp.exp(sl - m)
        den = den + jnp.sum(pl_, -1, keepdims=True)
        o = o + jnp.dot(pl_.astype(BF16), vlb[...], preferred_element_type=F32)
    o = o / den
    for g in range(ATT_GROUP):
        o_ref[:, g * LANE:(g + 1) * LANE] = o[g * tq:(g + 1) * tq].astype(o_ref.dtype)


def _attention(qkv, qn, kn, cos_t, sin_t, *, B, S, Tc, heads, latent_queries):
    NL = B * S
    kvh = heads // ATT_GROUP
    GW = ATT_GROUP * LANE
    scale = ATT_HEAD_DIM ** -0.5
    Tq = S if latent_queries else Tc
    tq = _tile(Tq, 128)
    nq = Tq // tq
    qbase = 0 if latent_queries else NL // tq
    cbase = NL // Tc
    common = [pl.BlockSpec((tq, GW), lambda b, j, i: (qbase + b * nq + i, j)),
              pl.BlockSpec((Tc, LANE), lambda b, j, i: (cbase + b, heads + j)),
              pl.BlockSpec((Tc, LANE), lambda b, j, i: (cbase + b, heads + kvh + j)),
              pl.BlockSpec((1, LANE), lambda b, j, i: (0, 0)),
              pl.BlockSpec((1, LANE), lambda b, j, i: (0, 0))]
    args = [qkv, qkv, qkv, qn, kn]
    scratch = [pltpu.VMEM((Tc, LANE), BF16), pltpu.VMEM((Tc, LANE), BF16)]
    if latent_queries:
        common += [pl.BlockSpec((S, LANE), lambda b, j, i: (b, heads + j)),
                   pl.BlockSpec((S, LANE), lambda b, j, i: (b, heads + kvh + j)),
                   pl.BlockSpec((tq, LANE), lambda b, j, i: (i, 0)),
                   pl.BlockSpec((tq, LANE), lambda b, j, i: (i, 0)),
                   pl.BlockSpec((S, LANE), lambda b, j, i: (0, 0)),
                   pl.BlockSpec((S, LANE), lambda b, j, i: (0, 0))]
        args += [qkv, qkv, cos_t, sin_t, cos_t, sin_t]
        scratch += [pltpu.VMEM((S, LANE), BF16), pltpu.VMEM((S, LANE), BF16)]
    return pl.pallas_call(
        functools.partial(_attn_kernel, rope=latent_queries, has_lat=latent_queries, tq=tq, scale=scale),
        grid=(B, kvh, nq),
        in_specs=common,
        out_specs=pl.BlockSpec((tq, GW), lambda b, j, i: (b * nq + i, j)),
        out_shape=jax.ShapeDtypeStruct((B * Tq, heads * LANE), BF16),
        scratch_shapes=scratch,
        compiler_params=_params("parallel", "parallel", "arbitrary"),
        name="attention_lat" if latent_queries else "attention_ctx",
    )(*args)


def _segsum(x, seg_ref, segt_ref):
    s = jnp.dot(x, seg_ref[...], precision=HI, preferred_element_type=F32)
    return jnp.dot(s, segt_ref[...], precision=HI, preferred_element_type=F32)


def _rwprep_kernel(first_ref, last_ref, z_ref, zp_ref, zn_ref, mu_ref, w0_ref, w2_ref, a0_ref, a2_ref, g2_ref,
                   kkp_ref, ka_ref, rk_ref, seg_ref, segt_ref,
                   r_o, v_o, kk_o, kd_o, lw_o, bb_o, g_o, bon_o, *, RW, R2W, R2A, RG):
    i = pl.program_id(0)
    has_prev = (1 - first_ref[i]).astype(F32)
    has_next = (1 - last_ref[i]).astype(F32)
    tt = z_ref.shape[0]
    row = lax.broadcasted_iota(jnp.int32, (tt, 1), 0)

    def shifted(lo, hi):
        z = z_ref[:, lo:hi]
        hp = zp_ref[SHIFT_HALO - 1:SHIFT_HALO, lo:hi] * has_prev
        hn = zn_ref[0:1, lo:hi] * has_next
        prev = jnp.where(row == 0, hp, pltpu.roll(z, 1, 0))
        nxt = jnp.where(row == tt - 1, hn, pltpu.roll(z, tt - 1, 0))
        return z + mu_ref[0:1, lo:hi] * (prev - z) + mu_ref[1:2, lo:hi] * (nxt - z)

    o = 3 * RW
    r = shifted(0, RW)
    k = shifted(RW, 2 * RW)
    v = shifted(2 * RW, o)
    wl = shifted(o, o + R2W)
    al = shifted(o + R2W, o + R2W + R2A)
    gl = shifted(o + R2W + R2A, o + R2W + R2A + RG)

    u = -(w0_ref[...] + jnp.dot(jnp.tanh(wl), w2_ref[...], precision=HI, preferred_element_type=F32))
    softplus = jnp.maximum(u, 0.0) + jnp.log(1.0 + jnp.exp(-jnp.abs(u)))
    lw_o[...] = -jnp.exp(-softplus - 0.5)
    a = _sigmoid(a0_ref[...] + jnp.dot(al, a2_ref[...], precision=HI, preferred_element_type=F32))
    g_o[...] = jnp.dot(_sigmoid(gl), g2_ref[...], precision=HI, preferred_element_type=F32)

    kk = k * kkp_ref[...]
    kk = kk / jnp.maximum(jnp.sqrt(_segsum(kk * kk, seg_ref, segt_ref)), 1e-12)
    ka = ka_ref[...]
    kd0 = k * (1.0 + (a[:, :RW] - 1.0) * ka)
    kd1 = k * (1.0 + (a[:, RW:] - 1.0) * ka)
    r_o[...] = r
    v_o[...] = v
    kk_o[...] = kk
    kd_o[:, :RW] = kd0
    kd_o[:, RW:] = kd1
    bb_o[:, :RW] = kk * a[:, :RW]
    bb_o[:, RW:] = kk * a[:, RW:]
    bon_o[...] = _segsum(r * rk_ref[...] * (kd0 + kd1), seg_ref, segt_ref) * v


def _blockdiag2(w):
    z = jnp.zeros_like(w[0])
    return jnp.concatenate([jnp.concatenate([w[0], z], 1), jnp.concatenate([z, w[1]], 1)], 0)


def _seq_edges(B, S, Tc, tt):
    first, last = [], []
    for n, L in ((B, S), (B, Tc)):
        per = L // tt
        for _ in range(n):
            first += [1] + [0] * (per - 1)
            last += [0] * (per - 1) + [1]
    return jnp.asarray(np.array(first, np.int32)), jnp.asarray(np.array(last, np.int32))


def _rwprep(rw, p, *, B, S, Tc):
    N, RWIN = rw.shape
    RW = p["rwkv_k_k"].shape[0]
    R2W = 2 * p["rwkv_w2"].shape[1]
    R2A = 2 * p["rwkv_a2"].shape[1]
    RG = p["rwkv_g2"].shape[0]
    tt = _tile(math.gcd(S, Tc), 128)
    nt = N // tt
    hb = tt // SHIFT_HALO
    nhb = N // SHIFT_HALO
    first, last = _seq_edges(B, S, Tc, tt)
    nh = RW // RWKV_HEAD_DIM
    seg = (jnp.arange(RW)[:, None] // RWKV_HEAD_DIM == jnp.arange(LANE)[None, :]).astype(F32)
    full = lambda shape: pl.BlockSpec(shape, lambda i, f, l: (0,) * len(shape))
    row = lambda w: pl.BlockSpec((tt, w), lambda i, f, l: (i, 0))
    out_w = [RW, RW, RW, 2 * RW, 2 * RW, 2 * RW, RW, RW]
    assert nh <= LANE
    grid_spec = pltpu.PrefetchScalarGridSpec(
        num_scalar_prefetch=2,
        grid=(nt,),
        in_specs=[row(RWIN),
                  pl.BlockSpec((SHIFT_HALO, RWIN), lambda i, f, l: (jnp.maximum(i * hb - 1, 0), 0)),
                  pl.BlockSpec((SHIFT_HALO, RWIN), lambda i, f, l: (jnp.minimum((i + 1) * hb, nhb - 1), 0)),
                  full((2, RWIN)), full((1, 2 * RW)), full((R2W, 2 * RW)), full((1, 2 * RW)), full((R2A, 2 * RW)),
                  full((RG, RW)), full((1, RW)), full((1, RW)), full((1, RW)), full((RW, LANE)), full((LANE, RW))],
        out_specs=[row(w) for w in out_w],
    )
    return pl.pallas_call(
        functools.partial(_rwprep_kernel, RW=RW, R2W=R2W, R2A=R2A, RG=RG),
        grid_spec=grid_spec,
        out_shape=[jax.ShapeDtypeStruct((N, w), F32) for w in out_w],
        compiler_params=_params("parallel"),
        name="rwkv_prep",
    )(first, last, rw, rw, rw, p["rwkv_mu"], p["rwkv_w0"].reshape(1, 2 * RW), _blockdiag2(p["rwkv_w2"]),
      p["rwkv_a0"].reshape(1, 2 * RW), _blockdiag2(p["rwkv_a2"]), p["rwkv_g2"],
      p["rwkv_k_k"].reshape(1, RW), p["rwkv_k_a"].reshape(1, RW), p["rwkv_r_k"].reshape(1, RW), seg, seg.T)


def _bdot(a, b):
    return jnp.dot(a.astype(BF16), b.astype(BF16), preferred_element_type=F32)


def _scanprep_kernel(r_ref, v_ref, kk_ref, kd_ref, lw_ref, bb_ref, A_o, B_o, Q_o, Y_o, *, nch):
    C = CHUNK
    HD = RWKV_HEAD_DIM
    sign = 1 - 2 * pl.program_id(0)
    ti = lax.broadcasted_iota(jnp.int32, (C, C), 0)
    si = lax.broadcasted_iota(jnp.int32, (C, C), 1)
    strict = sign * (ti - si) > 0
    incl = jnp.logical_or(strict, si == ti)
    m_incl = incl.astype(F32)
    eye = (si == ti).astype(F32)
    for c in range(nch):
        rows = slice(c * C, (c + 1) * C)
        lw = lw_ref[rows, :]
        P = jnp.dot(m_incl, lw, precision=HI, preferred_element_type=F32)
        eP = jnp.exp(P)
        enP = jnp.exp(-P)
        at2 = -kk_ref[rows, :] * jnp.exp(P - lw)
        bt2 = bb_ref[rows, :] * enP
        kt2 = kd_ref[rows, :] * enP
        rt2 = r_ref[rows, :] * eP
        v2 = v_ref[rows, :]
        pl2 = jnp.exp(jnp.sum(lw, axis=0, keepdims=True))
        for hh in range(LANE // HD):
            ls = slice(hh * HD, (hh + 1) * HD)
            at, bt, kt, rt, vv, plh = at2[:, ls], bt2[:, ls], kt2[:, ls], rt2[:, ls], v2[:, ls], pl2[:, ls]
            ar = jnp.concatenate([at, rt], axis=0).astype(BF16)
            gb = lax.dot_general(ar, bt.astype(BF16), NT_DIMS, preferred_element_type=F32)
            gk = lax.dot_general(ar, kt.astype(BF16), NT_DIMS, preferred_element_type=F32)
            L = jnp.where(strict, gb[:C], 0.0)
            lak = jnp.where(strict, gk[:C], 0.0)
            grb = jnp.where(incl, gb[C:], 0.0)
            grk = jnp.where(incl, gk[C:], 0.0)
            T = eye + L
            Lp = L
            for _ in range(int(math.log2(C)) - 1):
                Lp = _bdot(Lp, Lp)
                T = T + _bdot(T, Lp)
            ta = _bdot(T, at)
            wu = _bdot(T, _bdot(lak, vv))
            Q_o[0, c, hh] = rt + _bdot(grb, ta)
            Y_o[0, c, hh] = _bdot(grb, wu) + _bdot(grk, vv)
            tw = jnp.concatenate([ta.T, wu.T], axis=0)
            tb = _bdot(tw, bt)
            A_o[0, c, hh] = (eye[:HD, :HD] + tb[:HD]) * plh
            B_o[0, c, hh] = (tb[HD:] + _bdot(vv.T, kt)) * plh


def _scanprep(r, v, kk, kd, lw, bb, *, B, S, Tc):
    N, RW = r.shape
    nh = RW // RWKV_HEAD_DIM
    hp = RW // LANE
    rb = _tile(math.gcd(S, Tc), 256)
    nch = rb // CHUNK
    nchunks = N // CHUNK
    shared = lambda: pl.BlockSpec((rb, LANE), lambda d, i, h: (i, h))
    perdir = lambda: pl.BlockSpec((rb, LANE), lambda d, i, h: (i, d * hp + h))
    out = lambda: pl.BlockSpec((1, nch, LANE // RWKV_HEAD_DIM, CHUNK, RWKV_HEAD_DIM), lambda d, i, h: (d, i, h, 0, 0))
    shp = jax.ShapeDtypeStruct((2, nchunks, nh, CHUNK, RWKV_HEAD_DIM), F32)
    return pl.pallas_call(
        functools.partial(_scanprep_kernel, nch=nch),
        grid=(2, N // rb, hp),
        in_specs=[shared(), shared(), shared(), perdir(), perdir(), perdir()],
        out_specs=[out(), out(), out(), out()],
        out_shape=[shp, shp, shp, shp],
        compiler_params=_params("parallel", "parallel", "parallel"),
        name="rwkv_chunk_maps",
    )(r, v, kk, kd, lw, bb)


def _scanserial_kernel(A_ref, B_ref, Q_ref, Y0_ref, y_ref, s_ref, *, nh):
    @pl.when(pl.program_id(2) == 0)
    def _():
        s_ref[...] = jnp.zeros_like(s_ref)

    ys = []
    for h in range(nh):
        s0 = s_ref[h]
        sb = s0.astype(BF16)
        ys.append(lax.dot_general(Q_ref[0, 0, h].astype(BF16), sb, NT_DIMS, preferred_element_type=F32)
                  + Y0_ref[0, 0, h])
        s_ref[h] = jnp.dot(sb, A_ref[0, 0, h].astype(BF16), preferred_element_type=F32) + B_ref[0, 0, h]
    y_ref[0] = jnp.concatenate(ys, axis=1)


def _scanserial(A, Bm, Q, Y0, *, B, S, Tc):
    _, nchunks, nh, C, HD = A.shape
    N = nchunks * C
    nl, nc = S // C, Tc // C
    cbase = B * nl

    def chunk_of(d, b, p):
        kc = jnp.where(d == 0, p, nc - 1 - p)
        kl = jnp.where(d == 0, p - nc, nl - 1 - (p - nc))
        return jnp.where(p < nc, cbase + b * nc + kc, b * nl + kl)

    blk = lambda: pl.BlockSpec((1, 1, nh, C, HD), lambda d, b, p: (d, chunk_of(d, b, p), 0, 0, 0))
    return pl.pallas_call(
        functools.partial(_scanserial_kernel, nh=nh),
        grid=(2, B, nl + nc),
        in_specs=[blk(), blk(), blk(), blk()],
        out_specs=pl.BlockSpec((1, C, nh * HD), lambda d, b, p: (d, chunk_of(d, b, p), 0)),
        out_shape=jax.ShapeDtypeStruct((2, N, nh * HD), F32),
        scratch_shapes=[pltpu.VMEM((nh, HD, HD), F32)],
        compiler_params=_params("parallel", "parallel", "arbitrary"),
        name="rwkv_chunk_chain",
    )(A, Bm, Q, Y0)


def _rwout_kernel(y_ref, bon_ref, g_ref, lg_ref, lb_ref, seg_ref, segt_ref, o_ref):
    y = y_ref[0] + y_ref[1]
    inv = 1.0 / RWKV_HEAD_DIM
    mu = _segsum(y, seg_ref, segt_ref) * inv
    yc = y - mu
    var = _segsum(yc * yc, seg_ref, segt_ref) * inv
    yn = yc * lax.rsqrt(var + GN_EPS) * lg_ref[...] + lb_ref[...]
    o_ref[...] = ((yn + bon_ref[...]) * g_ref[...]).astype(o_ref.dtype)


def _rwout(y2, bonus, g, p, rows):
    RW = g.shape[1]
    tm = _tile(rows, 256)
    seg = (jnp.arange(RW)[:, None] // RWKV_HEAD_DIM == jnp.arange(LANE)[None, :]).astype(F32)
    full = lambda shape: pl.BlockSpec(shape, lambda i: (0,) * len(shape))
    return pl.pallas_call(
        _rwout_kernel,
        grid=(rows // tm,),
        in_specs=[pl.BlockSpec((2, tm, RW), lambda i: (0, i, 0)),
                  pl.BlockSpec((tm, RW), lambda i: (i, 0)),
                  pl.BlockSpec((tm, RW), lambda i: (i, 0)),
                  full((1, RW)), full((1, RW)), full((RW, LANE)), full((LANE, RW))],
        out_specs=pl.BlockSpec((tm, RW), lambda i: (i, 0)),
        out_shape=jax.ShapeDtypeStruct((rows, RW), BF16),
        compiler_params=_params("parallel"),
        name="rwkv_out",
    )(y2, bonus, g, p["rwkv_ln_g"].reshape(1, RW), p["rwkv_ln_b"].reshape(1, RW), seg, seg.T)


def _conv_kernel(first_ref, last_ref, z_ref, zp_ref, zn_ref, w_ref, b_ref, lg_ref, lb_ref, o_ref, u_ref, *, CW, K):
    i = pl.program_id(0)
    has_prev = (1 - first_ref[i]).astype(F32)
    has_next = (1 - last_ref[i]).astype(F32)
    tt = z_ref.shape[0]
    H = CONV_HALO
    glu = lambda ref: ref[:, :CW] * _sigmoid(ref[:, CW:])
    u_ref[0:H, :] = glu(zp_ref) * has_prev
    u_ref[H:H + tt, :] = glu(z_ref)
    u_ref[H + tt:, :] = glu(zn_ref) * has_next
    off = H - K // 2
    acc = jnp.zeros((tt, CW), F32) + b_ref[...]
    for k in range(K):
        acc = acc + w_ref[k:k + 1, :] * u_ref[off + k:off + k + tt, :]
    y = _ln(acc) * lg_ref[...] + lb_ref[...]
    o_ref[...] = (y * _sigmoid(y)).astype(o_ref.dtype)


def _conv(cv, p, *, B, S, Tc, rows):
    CW = cv.shape[1] // 2
    K = p["conv_w"].shape[0]
    assert K // 2 <= CONV_HALO
    tt = _tile(math.gcd(S, Tc), 256)
    hb = tt // CONV_HALO
    nhb = cv.shape[0] // CONV_HALO
    first, last = _seq_edges(B, S, Tc, tt)
    full = lambda shape: pl.BlockSpec(shape, lambda i, f, l: (0,) * len(shape))
    grid_spec = pltpu.PrefetchScalarGridSpec(
        num_scalar_prefetch=2,
        grid=(rows // tt,),
        in_specs=[pl.BlockSpec((tt, 2 * CW), lambda i, f, l: (i, 0)),
                  pl.BlockSpec((CONV_HALO, 2 * CW), lambda i, f, l: (jnp.maximum(i * hb - 1, 0), 0)),
                  pl.BlockSpec((CONV_HALO, 2 * CW), lambda i, f, l: (jnp.minimum((i + 1) * hb, nhb - 1), 0)),
                  full((K, CW)), full((1, CW)), full((1, CW)), full((1, CW))],
        out_specs=pl.BlockSpec((tt, CW), lambda i, f, l: (i, 0)),
        scratch_shapes=[pltpu.VMEM((tt + 2 * CONV_HALO, CW), F32)],
    )
    return pl.pallas_call(
        functools.partial(_conv_kernel, CW=CW, K=K),
        grid_spec=grid_spec,
        out_shape=jax.ShapeDtypeStruct((rows, CW), BF16),
        compiler_params=_params("parallel"),
        name="conv_module",
    )(first, last, cv, cv, cv, p["conv_w"], p["conv_b"].reshape(1, CW), p["conv_ln_g"].reshape(1, CW),
      p["conv_ln_b"].reshape(1, CW))


def _merge_kernel(a_ref, r_ref, c_ref, wa_ref, wr_ref, wc_ref, g0_ref, g1_ref, g2_ref, b0_ref, b1_ref, b2_ref, o_ref):
    m = _sigmoid(g0_ref[...] + b0_ref[...]) * jnp.dot(a_ref[...], wa_ref[...], preferred_element_type=F32)
    m = m + _sigmoid(g1_ref[...] + b1_ref[...]) * jnp.dot(r_ref[...], wr_ref[...], preferred_element_type=F32)
    m = m + _sigmoid(g2_ref[...] + b2_ref[...]) * jnp.dot(c_ref[...], wc_ref[...], preferred_element_type=F32)
    o_ref[...] = m.astype(o_ref.dtype)


def _merge(att, rwo, cvo, gt, wa, wr, wc, b_gate, rows):
    D = wa.shape[1]
    tm = _tile(rows, 512)
    tn = _tile(D, 512)
    nj = D // tn
    xin = lambda w: pl.BlockSpec((tm, w), lambda j, i: (i, 0))
    win = lambda w: pl.BlockSpec((w, tn), lambda j, i: (0, j))
    gate = lambda k: pl.BlockSpec((tm, tn), lambda j, i: (i, k * nj + j))
    bias = lambda k: pl.BlockSpec((1, tn), lambda j, i: (0, k * nj + j))
    bg = b_gate.reshape(1, 3 * D)
    return pl.pallas_call(
        _merge_kernel,
        grid=(nj, rows // tm),
        in_specs=[xin(att.shape[1]), xin(rwo.shape[1]), xin(cvo.shape[1]),
                  win(wa.shape[0]), win(wr.shape[0]), win(wc.shape[0]),
                  gate(0), gate(1), gate(2), bias(0), bias(1), bias(2)],
        out_specs=pl.BlockSpec((tm, tn), lambda j, i: (i, j)),
        out_shape=jax.ShapeDtypeStruct((rows, D), BF16),
        compiler_params=_params("parallel", "parallel"),
        name="branch_merge",
    )(att, rwo, cvo, wa, wr, wc, gt, gt, gt, bg, bg, bg)


def _outproj_kernel(m_ref, w_ref, x_ref, gm_ref, sh_ref, sc_ref, lg_ref, lb_ref, wrt_ref, brt_ref,
                    x_o, h_o, lgt_o, *, alpha):
    y = jnp.dot(m_ref[...], w_ref[...], preferred_element_type=F32)
    x1 = _ln(alpha * x_ref[...] + gm_ref[0] * y) * lg_ref[...] + lb_ref[...]
    x_o[...] = x1
    h2 = _ln(x1) * (1.0 + sc_ref[0]) + sh_ref[0]
    h_o[...] = h2.astype(h_o.dtype)
    lgt_o[...] = jnp.dot(h2, wrt_ref[...], precision=HI, preferred_element_type=F32) + brt_ref[...]


def _outproj(m, w_out, xa, mods3, ln_g, ln_b, w_rt, b_rt, rows, modrow, tm, alpha):
    D = w_out.shape[0]
    full = lambda shape: pl.BlockSpec(shape, lambda i: (0,) * len(shape))
    rowb = lambda w: pl.BlockSpec((tm, w), lambda i: (i, 0))
    mod = lambda blk: pl.BlockSpec((1, 1, D), lambda i: (modrow(i), 0, blk))
    return pl.pallas_call(
        functools.partial(_outproj_kernel, alpha=alpha),
        grid=(rows // tm,),
        in_specs=[rowb(D), full((D, D)), rowb(D), mod(2), mod(3), mod(4), full((1, D)), full((1, D)),
                  full((D, LANE)), full((1, LANE))],
        out_specs=[rowb(D), rowb(D), rowb(LANE)],
        out_shape=[jax.ShapeDtypeStruct((rows, D), F32), jax.ShapeDtypeStruct((rows, D), BF16),
                   jax.ShapeDtypeStruct((rows, LANE), F32)],
        compiler_params=_params("parallel"),
        name="out_proj",
    )(m, w_out, xa, mods3, mods3, mods3, ln_g.reshape(1, D), ln_b.reshape(1, D), w_rt, b_rt)


def _route_kernel(l_ref, id_o, w_o):
    lg = l_ref[...]
    lane = lax.broadcasted_iota(jnp.int32, lg.shape, 1)
    neg = -jnp.inf
    big = jnp.int32(2 ** 30)
    isg = jnp.logical_and(lane >= N_EXPERTS, lane < N_EXPERTS + N_GROUPS)
    gl = jnp.where(isg, lg, neg)
    gmax = jnp.max(gl, -1, keepdims=True)
    grp = jnp.min(jnp.where(gl == gmax, lane - N_EXPERTS, big), -1, keepdims=True)
    grp_w = 1.0 / jnp.sum(jnp.exp(gl - gmax), -1, keepdims=True)
    ing = jnp.logical_and(lane < N_EXPERTS, jnp.right_shift(lane, int(math.log2(EXPERTS_PER_GROUP))) == grp)
    el = jnp.where(ing, lg, neg)
    e1 = jnp.max(el, -1, keepdims=True)
    i1 = jnp.min(jnp.where(el == e1, lane, big), -1, keepdims=True)
    psum = jnp.sum(jnp.exp(el - e1), -1, keepdims=True)
    el2 = jnp.where(lane == i1, neg, el)
    e2 = jnp.max(el2, -1, keepdims=True)
    i2 = jnp.min(jnp.where(el2 == e2, lane, big), -1, keepdims=True)
    p1 = 1.0 / psum
    p2 = jnp.exp(e2 - e1) / psum
    tot = p1 + p2
    id_o[...] = jnp.where(lane == 0, i1, jnp.where(lane == 1, i2, 0))
    w_o[...] = jnp.where(lane == 0, grp_w * p1 / tot, jnp.where(lane == 1, grp_w * p2 / tot, 0.0))


def _route(logits):
    n = logits.shape[0]
    tm = _tile(n, 1024)
    blk = lambda: pl.BlockSpec((tm, LANE), lambda i: (i, 0))
    return pl.pallas_call(
        _route_kernel,
        grid=(n // tm,),
        in_specs=[blk()],
        out_specs=[blk(), blk()],
        out_shape=[jax.ShapeDtypeStruct((n, LANE), jnp.int32), jax.ShapeDtypeStruct((n, LANE), F32)],
        compiler_params=_params("parallel"),
        name="route",
    )(logits)


def _moe_kernel(be_ref, act_ref, x_ref, wg_ref, wu_ref, wd_ref, o_ref):
    i = pl.program_id(0)

    @pl.when(act_ref[i] > 0)
    def _():
        x = x_ref[...]
        gte = jnp.dot(x, wg_ref[0], preferred_element_type=F32)
        up = jnp.dot(x, wu_ref[0], preferred_element_type=F32)
        h = (gte * _sigmoid(gte) * up).astype(BF16)
        o_ref[...] = jnp.dot(h, wd_ref[0], preferred_element_type=F32)

    @pl.when(act_ref[i] == 0)
    def _():
        o_ref[...] = jnp.zeros_like(o_ref)


def _moe_ffn(xg, block_e, active, wg, wu, wd):
    R, D = xg.shape
    FF = wg.shape[2]
    nb = R // MOE_BLOCK
    grid_spec = pltpu.PrefetchScalarGridSpec(
        num_scalar_prefetch=2,
        grid=(nb,),
        in_specs=[pl.BlockSpec((MOE_BLOCK, D), lambda i, be, ac: (i, 0)),
                  pl.BlockSpec((1, D, FF), lambda i, be, ac: (be[i], 0, 0)),
                  pl.BlockSpec((1, D, FF), lambda i, be, ac: (be[i], 0, 0)),
                  pl.BlockSpec((1, FF, D), lambda i, be, ac: (be[i], 0, 0))],
        out_specs=pl.BlockSpec((MOE_BLOCK, D), lambda i, be, ac: (i, 0)),
    )
    return pl.pallas_call(
        _moe_kernel,
        grid_spec=grid_spec,
        out_shape=jax.ShapeDtypeStruct((R, D), F32),
        compiler_params=_params("arbitrary"),
        name="moe_ffn",
    )(block_e, active, xg, wg, wu, wd)


def _dispatch(ids):
    n = ids.shape[0]
    A = 2 * n
    flat_e = ids.reshape(-1)
    order = jnp.argsort(flat_e)
    se = flat_e[order]
    counts = jnp.bincount(flat_e, length=N_EXPERTS)
    padded = (counts + MOE_BLOCK - 1) // MOE_BLOCK * MOE_BLOCK
    pad_end = jnp.cumsum(padded)
    pad_start = pad_end - padded
    start = jnp.cumsum(counts) - counts
    dest_sorted = (pad_start[se] + jnp.arange(A) - start[se]).astype(jnp.int32)
    nb = -(-A // MOE_BLOCK) + N_EXPERTS
    slot_tok = jnp.zeros((nb * MOE_BLOCK,), jnp.int32).at[dest_sorted].set((order // 2).astype(jnp.int32))
    dest = jnp.zeros((A,), jnp.int32).at[order].set(dest_sorted)
    blk_start = jnp.arange(nb) * MOE_BLOCK
    block_e = jnp.minimum(jnp.searchsorted(pad_end, blk_start, side="right"), N_EXPERTS - 1).astype(jnp.int32)
    active = (blk_start < pad_end[-1]).astype(jnp.int32)
    return slot_tok, dest.reshape(n, 2), block_e, active


def _final_kernel(x_ref, y0_ref, y1_ref, w_ref, gm_ref, lg_ref, lb_ref, o_ref, *, alpha):
    w = w_ref[...]
    f = w[:, 0:1] * y0_ref[...] + w[:, 1:2] * y1_ref[...]
    o_ref[...] = _ln(alpha * x_ref[...] + gm_ref[0] * f) * lg_ref[...] + lb_ref[...]


def _final(x1, y0, y1, wts, mods3, ln_g, ln_b, rows, modrow, tm, alpha):
    D = x1.shape[1]
    rowb = lambda w: pl.BlockSpec((tm, w), lambda i: (i, 0))
    full = lambda shape: pl.BlockSpec(shape, lambda i: (0,) * len(shape))
    return pl.pallas_call(
        functools.partial(_final_kernel, alpha=alpha),
        grid=(rows // tm,),
        in_specs=[rowb(D), rowb(D), rowb(D), rowb(LANE),
                  pl.BlockSpec((1, 1, D), lambda i: (modrow(i), 0, 5)), full((1, D)), full((1, D))],
        out_specs=rowb(D),
        out_shape=jax.ShapeDtypeStruct((rows, D), F32),
        compiler_params=_params("parallel"),
        name="ffn_residual",
    )(x1, y0, y1, wts, mods3, ln_g.reshape(1, D), ln_b.reshape(1, D))


def _rope_tables(S):
    rows = S // GRID_W
    row = jnp.repeat(jnp.arange(rows, dtype=F32), GRID_W)
    col = jnp.tile(jnp.arange(GRID_W, dtype=F32), rows)
    nf = ATT_HEAD_DIM // 4
    inv = ROPE_THETA ** (-jnp.arange(nf, dtype=F32) / nf)
    ar, ac = row[:, None] * inv, col[:, None] * inv
    cos = jnp.concatenate([jnp.cos(ar), jnp.cos(ar), jnp.cos(ac), jnp.cos(ac)], -1)
    sin = jnp.concatenate([-jnp.sin(ar), jnp.sin(ar), -jnp.sin(ac), jnp.sin(ac)], -1)
    return cos, sin


def kernel(x, c, ctx, c_ctx, w_mod, b_mod, w_in, b_gate, q_norm, k_norm, w_att_o, rwkv_mu, rwkv_w0, rwkv_w2, rwkv_a0, rwkv_a2, rwkv_g2, rwkv_k_k, rwkv_k_a, rwkv_r_k, rwkv_ln_g, rwkv_ln_b, w_rwkv_o, conv_w, conv_b, conv_ln_g, conv_ln_b, w_conv_o, w_out, ln1_g, ln1_b, w_group, b_group, w_router, b_router, w_e_gate, w_e_up, w_e_down, ln2_g, ln2_b):
    B, S, D = x.shape
    Tc = ctx.shape[1]
    depth = w_mod.shape[0]
    NL, NC = B * S, B * Tc
    N = NL + NC
    alpha = (2 * depth) ** 0.25
    att_w = w_att_o.shape[1]
    heads = att_w // ATT_HEAD_DIM
    rw_in = rwkv_mu.shape[-1]
    conv_cw = w_conv_o.shape[1]
    kv_w = (w_in.shape[-1] - att_w - rw_in - 2 * conv_cw - 3 * D) // 2
    assert kv_w == heads // ATT_GROUP * ATT_HEAD_DIM
    o_rw = att_w + 2 * kv_w
    o_cv = o_rw + rw_in
    o_gt = o_cv + 2 * conv_cw

    tmod = _tile(math.gcd(S, Tc), 256)
    tps = S // tmod
    nlt = NL // tmod
    modrow = lambda i: jnp.where(i < nlt, i // tps, B)
    cos_t, sin_t = _rope_tables(S)

    R = -(-(B + 1) // SUBLANE) * SUBLANE
    cv = jnp.zeros((R, D), F32).at[:B].set(c).at[B].set(c_ctx)
    xa = jnp.concatenate([x.reshape(NL, D), ctx.reshape(NC, D)], 0)

    for l in range(depth):
        last = l == depth - 1
        rows = NL if last else N
        p = dict(rwkv_mu=rwkv_mu[l], rwkv_w0=rwkv_w0[l], rwkv_w2=rwkv_w2[l], rwkv_a0=rwkv_a0[l], rwkv_a2=rwkv_a2[l],
                 rwkv_g2=rwkv_g2[l], rwkv_k_k=rwkv_k_k[l], rwkv_k_a=rwkv_k_a[l], rwkv_r_k=rwkv_r_k[l],
                 rwkv_ln_g=rwkv_ln_g[l], rwkv_ln_b=rwkv_ln_b[l], conv_w=conv_w[l], conv_b=conv_b[l],
                 conv_ln_g=conv_ln_g[l], conv_ln_b=conv_ln_b[l])
        mods3 = _adaln(cv, w_mod[l], b_mod[l]).reshape(R, 1, 6 * D)
        wl = w_in[l]
        h = _lnmod(xa, mods3, 0, 1, N, modrow, tmod)
        qkv = _matmul(h, wl[:, :o_rw].astype(BF16), 512)
        rw = _matmul(h, wl[:, o_rw:o_cv].astype(BF16), 1152)
        cvx = _matmul(h, wl[:, o_cv:o_gt].astype(BF16), 1024, rows)
        gt = _matmul(h, wl[:, o_gt:].astype(BF16), 1024, rows)

        qn, kn = q_norm[l].reshape(1, LANE), k_norm[l].reshape(1, LANE)
        att = _attention(qkv, qn, kn, cos_t, sin_t, B=B, S=S, Tc=Tc, heads=heads, latent_queries=True)
        if not last:
            att_c = _attention(qkv, qn, kn, cos_t, sin_t, B=B, S=S, Tc=Tc, heads=heads, latent_queries=False)
            att = jnp.concatenate([att, att_c], 0)

        r, v, kk, kd, lw, bb, g, bonus = _rwprep(rw, p, B=B, S=S, Tc=Tc)
        Am, Bm, Qm, Y0 = _scanprep(r, v, kk, kd, lw, bb, B=B, S=S, Tc=Tc)
        y2 = _scanserial(Am, Bm, Qm, Y0, B=B, S=S, Tc=Tc)
        rwo = _rwout(y2, bonus, g, p, rows)
        cvo = _conv(cvx, p, B=B, S=S, Tc=Tc, rows=rows)

        m = _merge(att, rwo, cvo, gt, w_att_o[l].astype(BF16), w_rwkv_o[l].astype(BF16), w_conv_o[l].astype(BF16),
                   b_gate[l], rows)
        w_rt = jnp.zeros((D, LANE), F32).at[:, :N_EXPERTS].set(w_router[l]).at[:, N_EXPERTS:N_EXPERTS + N_GROUPS].set(w_group[l])
        b_rt = jnp.zeros((1, LANE), F32).at[0, :N_EXPERTS].set(b_router[l]).at[0, N_EXPERTS:N_EXPERTS + N_GROUPS].set(b_group[l])
        x1, h2, logits = _outproj(m, w_out[l].astype(BF16), xa, mods3, ln1_g[l], ln1_b[l], w_rt, b_rt, rows, modrow,
                                  tmod, alpha)

        ids128, wts128 = _route(logits)
        slot_tok, dest, block_e, active = _dispatch(ids128[:, :2])
        xg = jnp.take(h2, slot_tok, axis=0)
        yb = _moe_ffn(xg, block_e, active, w_e_gate[l].astype(BF16), w_e_up[l].astype(BF16), w_e_down[l].astype(BF16))
        y0 = jnp.take(yb, dest[:, 0], axis=0)
        y1 = jnp.take(yb, dest[:, 1], axis=0)
        xa = _final(x1, y0, y1, wts128, mods3, ln2_g[l], ln2_b[l], rows, modrow, tmod, alpha)

    return xa[:NL].reshape(B, S, D)
```

```python
import functools
import math

import numpy as np
import jax
import jax.numpy as jnp
from jax import lax
from jax.experimental import pallas as pl
from jax.experimental.pallas import tpu as pltpu

F32 = jnp.float32
BF16 = jnp.bfloat16
HI = lax.Precision.HIGHEST

LANE = 128
SUBLANE = 8
ATT_HEAD_DIM = 128
ATT_GROUP = 4
ROPE_THETA = 10000.0
GRID_W = 64
RWKV_HEAD_DIM = 64
CHUNK = 64
CONV_HALO = 16
SHIFT_HALO = 8
N_GROUPS = 4
EXPERTS_PER_GROUP = 8
N_EXPERTS = N_GROUPS * EXPERTS_PER_GROUP
MOE_BLOCK = 256
GN_EPS = 64e-5
LN_EPS = 1e-6
RMS_EPS = 1e-6
VMEM_LIMIT = 48 * 1024 * 1024

NT_DIMS = (((1,), (1,)), ((), ()))
TN_DIMS = (((0,), (0,)), ((), ()))


def _params(*sem):
    return pltpu.CompilerParams(dimension_semantics=sem, vmem_limit_bytes=VMEM_LIMIT)


def _sigmoid(x):
    return 1.0 / (1.0 + jnp.exp(-x))


def _ln(x):
    mu = jnp.mean(x, -1, keepdims=True)
    xc = x - mu
    var = jnp.mean(xc * xc, -1, keepdims=True)
    return xc * lax.rsqrt(var + LN_EPS)


def _tile(n, pref):
    t = min(n, pref)
    while n % t:
        t -= SUBLANE
    return t


def _adaln_kernel(c_ref, w_ref, b_ref, o_ref):
    c = c_ref[...]
    s = c * _sigmoid(c)
    o_ref[...] = jnp.dot(s, w_ref[...], precision=HI, preferred_element_type=F32) + b_ref[...]


def _adaln(cv, w_mod, b_mod):
    R, D = cv.shape
    W = w_mod.shape[1]
    tn = _tile(W, 1024)
    return pl.pallas_call(
        _adaln_kernel,
        grid=(W // tn,),
        in_specs=[pl.BlockSpec((R, D), lambda j: (0, 0)),
                  pl.BlockSpec((D, tn), lambda j: (0, j)),
                  pl.BlockSpec((1, tn), lambda j: (0, j))],
        out_specs=pl.BlockSpec((R, tn), lambda j: (0, j)),
        out_shape=jax.ShapeDtypeStruct((R, W), F32),
        compiler_params=_params("parallel"),
        name="adaln",
    )(cv, w_mod, b_mod.reshape(1, W))


def _lnmod_kernel(x_ref, sh_ref, sc_ref, o_ref):
    o_ref[...] = (_ln(x_ref[...]) * (1.0 + sc_ref[0]) + sh_ref[0]).astype(o_ref.dtype)


def _lnmod(xa, mods3, shift_blk, scale_blk, rows, modrow, tm):
    D = xa.shape[1]
    return pl.pallas_call(
        _lnmod_kernel,
        grid=(rows // tm,),
        in_specs=[pl.BlockSpec((tm, D), lambda i: (i, 0)),
                  pl.BlockSpec((1, 1, D), lambda i: (modrow(i), 0, shift_blk)),
                  pl.BlockSpec((1, 1, D), lambda i: (modrow(i), 0, scale_blk))],
        out_specs=pl.BlockSpec((tm, D), lambda i: (i, 0)),
        out_shape=jax.ShapeDtypeStruct((rows, D), BF16),
        compiler_params=_params("parallel"),
        name="lnmod",
    )(xa, mods3, mods3)


def _mm_kernel(x_ref, w_ref, o_ref):
    o_ref[...] = jnp.dot(x_ref[...], w_ref[...], preferred_element_type=F32).astype(o_ref.dtype)


def _matmul(x, w, tn_pref, rows=None, out_dtype=F32, tm_pref=1024):
    M, K = x.shape
    M = rows or M
    N = w.shape[1]
    tm = _tile(M, tm_pref)
    tn = _tile(N, tn_pref)
    return pl.pallas_call(
        _mm_kernel,
        grid=(N // tn, M // tm),
        in_specs=[pl.BlockSpec((tm, K), lambda j, i: (i, 0)),
                  pl.BlockSpec((K, tn), lambda j, i: (0, j))],
        out_specs=pl.BlockSpec((tm, tn), lambda j, i: (i, j)),
        out_shape=jax.ShapeDtypeStruct((M, N), out_dtype),
        compiler_params=_params("parallel", "parallel"),
        name="matmul",
    )(x, w)


def _rms(t, g):
    return t * lax.rsqrt(jnp.mean(t * t, -1, keepdims=True) + RMS_EPS) * g


def _rope(t, cos, sin_signed):
    lane = lax.broadcasted_iota(jnp.int32, t.shape, 1)
    first = (lane & 63) < 32
    partner = jnp.where(first, pltpu.roll(t, LANE - 32, 1), pltpu.roll(t, 32, 1))
    return t * cos + partner * sin_signed


def _attn_kernel(*refs, rope, has_lat, tq, scale):
    if has_lat:
        (q_ref, kc_ref, vc_ref, qn_ref, kn_ref, kl_ref, vl_ref, cq_ref, sq_ref, ck_ref, sk_ref,
         o_ref, kcb, vcb, klb, vlb) = refs
    else:
        q_ref, kc_ref, vc_ref, qn_ref, kn_ref, o_ref, kcb, vcb = refs

    @pl.when(pl.program_id(2) == 0)
    def _():
        kcb[...] = _rms(kc_ref[...], kn_ref[...]).astype(BF16)
        vcb[:, :LANE] = vc_ref[...].astype(BF16)
        vcb[:, LANE:] = jnp.ones(vc_ref.shape, BF16)
        if has_lat:
            klb[...] = _rope(_rms(kl_ref[...], kn_ref[...]), ck_ref[...], sk_ref[...]).astype(BF16)
            vlb[:, :LANE] = vl_ref[...].astype(BF16)
            vlb[:, LANE:] = jnp.ones(vl_ref.shape, BF16)

    qs = []
    for g in range(ATT_GROUP):
        qg = _rms(q_ref[:, g * LANE:(g + 1) * LANE], qn_ref[...])
        if rope:
            qg = _rope(qg, cq_ref[...], sq_ref[...])
        qs.append((qg * (scale * math.log2(math.e))).astype(BF16))
    qa = jnp.concatenate(qs, axis=0)
    sc = lax.dot_general(qa, kcb[...], NT_DIMS, preferred_element_type=F32)
    m = jnp.max(sc, -1, keepdims=True)
    if has_lat:
        sl = lax.dot_general(qa, klb[...], NT_DIMS, preferred_element_type=F32)
        m = jnp.maximum(m, jnp.max(sl, -1, keepdims=True))
    od = jnp.dot(jnp.exp2(sc - m).astype(BF16), vcb[...], preferred_element_type=F32)
    if has_lat:
        od = od + jnp.dot(jnp.exp2(sl - m).astype(BF16), vlb[...], preferred_element_type=F32)
    o = od[:, :LANE] / od[:, LANE:]
    for g in range(ATT_GROUP):
        o_ref[:, g * LANE:(g + 1) * LANE] = o[g * tq:(g + 1) * tq].astype(o_ref.dtype)


def _attention(qkv, qn, kn, cos_t, sin_t, *, B, S, Tc, heads, latent_queries):
    NL = B * S
    kvh = heads // ATT_GROUP
    GW = ATT_GROUP * LANE
    scale = ATT_HEAD_DIM ** -0.5
    Tq = S if latent_queries else Tc
    tq = _tile(Tq, 128)
    nq = Tq // tq
    qbase = 0 if latent_queries else NL // tq
    cbase = NL // Tc
    common = [pl.BlockSpec((tq, GW), lambda b, j, i: (qbase + b * nq + i, j)),
              pl.BlockSpec((Tc, LANE), lambda b, j, i: (cbase + b, heads + j)),
              pl.BlockSpec((Tc, LANE), lambda b, j, i: (cbase + b, heads + kvh + j)),
              pl.BlockSpec((1, LANE), lambda b, j, i: (0, 0)),
              pl.BlockSpec((1, LANE), lambda b, j, i: (0, 0))]
    args = [qkv, qkv, qkv, qn, kn]
    scratch = [pltpu.VMEM((Tc, LANE), BF16), pltpu.VMEM((Tc, 2 * LANE), BF16)]
    if latent_queries:
        common += [pl.BlockSpec((S, LANE), lambda b, j, i: (b, heads + j)),
                   pl.BlockSpec((S, LANE), lambda b, j, i: (b, heads + kvh + j)),
                   pl.BlockSpec((tq, LANE), lambda b, j, i: (i, 0)),
                   pl.BlockSpec((tq, LANE), lambda b, j, i: (i, 0)),
                   pl.BlockSpec((S, LANE), lambda b, j, i: (0, 0)),
                   pl.BlockSpec((S, LANE), lambda b, j, i: (0, 0))]
        args += [qkv, qkv, cos_t, sin_t, cos_t, sin_t]
        scratch += [pltpu.VMEM((S, LANE), BF16), pltpu.VMEM((S, 2 * LANE), BF16)]
    return pl.pallas_call(
        functools.partial(_attn_kernel, rope=latent_queries, has_lat=latent_queries, tq=tq, scale=scale),
        grid=(B, kvh, nq),
        in_specs=common,
        out_specs=pl.BlockSpec((tq, GW), lambda b, j, i: (b * nq + i, j)),
        out_shape=jax.ShapeDtypeStruct((B * Tq, heads * LANE), BF16),
        scratch_shapes=scratch,
        compiler_params=_params("parallel", "parallel", "arbitrary"),
        name="attention_lat" if latent_queries else "attention_ctx",
    )(*args)


def _segsum(x, seg_ref, segt_ref):
    return _split_dot(_split_dot(x, seg_ref[...]), segt_ref[...])


def _split_dot(x, w):
    hi = x.astype(BF16)
    lo = (x - hi.astype(F32)).astype(BF16)
    return jnp.dot(hi, w, preferred_element_type=F32) + jnp.dot(lo, w, preferred_element_type=F32)


def _rwprep_kernel(first_ref, last_ref, z_ref, zp_ref, zn_ref, mu_ref, w0_ref, w2_ref, a0_ref, a2_ref, g2_ref,
                   kkp_ref, ka_ref, rk_ref, seg_ref, segt_ref,
                   r_o, v_o, kk_o, kd_o, lw_o, bb_o, g_o, bon_o, *, RW, R2W, R2A, RG):
    i = pl.program_id(0)
    has_prev = (1 - first_ref[i]).astype(F32)
    has_next = (1 - last_ref[i]).astype(F32)
    tt = z_ref.shape[0]
    row = lax.broadcasted_iota(jnp.int32, (tt, 1), 0)

    def shifted(lo, hi):
        z = z_ref[:, lo:hi]
        hp = zp_ref[SHIFT_HALO - 1:SHIFT_HALO, lo:hi] * has_prev
        hn = zn_ref[0:1, lo:hi] * has_next
        prev = jnp.where(row == 0, hp, pltpu.roll(z, 1, 0))
        nxt = jnp.where(row == tt - 1, hn, pltpu.roll(z, tt - 1, 0))
        return z + mu_ref[0:1, lo:hi] * (prev - z) + mu_ref[1:2, lo:hi] * (nxt - z)

    o = 3 * RW
    r = shifted(0, RW)
    k = shifted(RW, 2 * RW)
    v = shifted(2 * RW, o)
    wl = shifted(o, o + R2W)
    al = shifted(o + R2W, o + R2W + R2A)
    gl = shifted(o + R2W + R2A, o + R2W + R2A + RG)

    u = -(w0_ref[...] + _bdot(jnp.tanh(wl), w2_ref[...]))
    softplus = jnp.maximum(u, 0.0) + jnp.log(1.0 + jnp.exp(-jnp.abs(u)))
    lw_o[...] = -jnp.exp(-softplus - 0.5)
    a = _sigmoid(a0_ref[...] + _bdot(al, a2_ref[...]))
    g_o[...] = _bdot(_sigmoid(gl), g2_ref[...])

    kk = k * kkp_ref[...]
    kk = kk / jnp.maximum(jnp.sqrt(_segsum(kk * kk, seg_ref, segt_ref)), 1e-12)
    ka = ka_ref[...]
    kd0 = k * (1.0 + (a[:, :RW] - 1.0) * ka)
    kd1 = k * (1.0 + (a[:, RW:] - 1.0) * ka)
    r_o[...] = r
    v_o[...] = v
    kk_o[...] = kk
    kd_o[:, :RW] = kd0
    kd_o[:, RW:] = kd1
    bb_o[:, :RW] = kk * a[:, :RW]
    bb_o[:, RW:] = kk * a[:, RW:]
    bon_o[...] = _segsum(r * rk_ref[...] * (kd0 + kd1), seg_ref, segt_ref) * v


def _blockdiag2(w):
    z = jnp.zeros_like(w[0])
    return jnp.concatenate([jnp.concatenate([w[0], z], 1), jnp.concatenate([z, w[1]], 1)], 0)


def _seq_edges(B, S, Tc, tt):
    first, last = [], []
    for n, L in ((B, S), (B, Tc)):
        per = L // tt
        for _ in range(n):
            first += [1] + [0] * (per - 1)
            last += [0] * (per - 1) + [1]
    return jnp.asarray(np.array(first, np.int32)), jnp.asarray(np.array(last, np.int32))


def _rwprep(rw, p, *, B, S, Tc):
    N, RWIN = rw.shape
    RW = p["rwkv_k_k"].shape[0]
    R2W = 2 * p["rwkv_w2"].shape[1]
    R2A = 2 * p["rwkv_a2"].shape[1]
    RG = p["rwkv_g2"].shape[0]
    tt = _tile(math.gcd(S, Tc), 128)
    nt = N // tt
    hb = tt // SHIFT_HALO
    nhb = N // SHIFT_HALO
    first, last = _seq_edges(B, S, Tc, tt)
    nh = RW // RWKV_HEAD_DIM
    seg = (jnp.arange(RW)[:, None] // RWKV_HEAD_DIM == jnp.arange(LANE)[None, :]).astype(BF16)
    full = lambda shape: pl.BlockSpec(shape, lambda i, f, l: (0,) * len(shape))
    row = lambda w: pl.BlockSpec((tt, w), lambda i, f, l: (i, 0))
    out_w = [RW, RW, RW, 2 * RW, 2 * RW, 2 * RW, RW, RW]
    assert nh <= LANE
    grid_spec = pltpu.PrefetchScalarGridSpec(
        num_scalar_prefetch=2,
        grid=(nt,),
        in_specs=[row(RWIN),
                  pl.BlockSpec((SHIFT_HALO, RWIN), lambda i, f, l: (jnp.maximum(i * hb - 1, 0), 0)),
                  pl.BlockSpec((SHIFT_HALO, RWIN), lambda i, f, l: (jnp.minimum((i + 1) * hb, nhb - 1), 0)),
                  full((2, RWIN)), full((1, 2 * RW)), full((R2W, 2 * RW)), full((1, 2 * RW)), full((R2A, 2 * RW)),
                  full((RG, RW)), full((1, RW)), full((1, RW)), full((1, RW)), full((RW, LANE)), full((LANE, RW))],
        out_specs=[row(w) for w in out_w],
    )
    return pl.pallas_call(
        functools.partial(_rwprep_kernel, RW=RW, R2W=R2W, R2A=R2A, RG=RG),
        grid_spec=grid_spec,
        out_shape=[jax.ShapeDtypeStruct((N, w), F32) for w in out_w],
        compiler_params=_params("parallel"),
        name="rwkv_prep",
    )(first, last, rw, rw, rw, p["rwkv_mu"], p["rwkv_w0"].reshape(1, 2 * RW), _blockdiag2(p["rwkv_w2"]),
      p["rwkv_a0"].reshape(1, 2 * RW), _blockdiag2(p["rwkv_a2"]), p["rwkv_g2"],
      p["rwkv_k_k"].reshape(1, RW), p["rwkv_k_a"].reshape(1, RW), p["rwkv_r_k"].reshape(1, RW), seg, seg.T)


def _bdot(a, b):
    return jnp.dot(a.astype(BF16), b.astype(BF16), preferred_element_type=F32)


def _scanprep_kernel(r_ref, v_ref, kk_ref, kd_ref, lw_ref, bb_ref, A_o, B_o, Q_o, Y_o, *, G, ngrp):
    C = CHUNK
    HD = RWKV_HEAD_DIM
    shift = int(math.log2(C))
    hshift = int(math.log2(HD))
    sign = 1 - 2 * pl.program_id(0)
    ti = lax.broadcasted_iota(jnp.int32, (G, G), 0)
    si = lax.broadcasted_iota(jnp.int32, (G, G), 1)
    same = jnp.right_shift(ti, shift) == jnp.right_shift(si, shift)
    strict = jnp.logical_and(same, sign * (ti - si) > 0)
    incl = jnp.logical_and(same, sign * (ti - si) >= 0)
    eye = (si == ti).astype(F32)
    head0 = lax.broadcasted_iota(jnp.int32, (G, LANE), 1) < HD
    hi2 = lax.broadcasted_iota(jnp.int32, (LANE, LANE), 0)
    hj2 = lax.broadcasted_iota(jnp.int32, (LANE, LANE), 1)
    same_head = jnp.right_shift(hi2, hshift) == jnp.right_shift(hj2, hshift)
    eye_h = (hi2 == hj2).astype(F32)
    sel = lambda a0, a1: jnp.where(head0, a0, a1)
    both = (0, 1)
    fwd = (pl.program_id(0) == 0).astype(F32)
    trow = jnp.bitwise_and(lax.broadcasted_iota(jnp.int32, (G, LANE), 0), C - 1)

    st = []
    for g in range(ngrp):
        rows = slice(g * G, (g + 1) * G)
        lw = lw_ref[rows, :]
        F = lw
        for k in (1 << i for i in range(shift)):
            F = F + jnp.where(trow >= k, pltpu.roll(F, k, 0), 0.0)
        tot = jnp.concatenate([jnp.broadcast_to(F[(c + 1) * C - 1:(c + 1) * C], (C, LANE)) for c in range(G // C)], axis=0)
        P = fwd * F + (1.0 - fwd) * (tot - F + lw)
        eP = jnp.exp(P)
        enP = jnp.exp(-P)
        at = -kk_ref[rows, :] * jnp.exp(P - lw)
        bt = (bb_ref[rows, :] * enP).astype(BF16)
        kt = (kd_ref[rows, :] * enP).astype(BF16)
        rt = r_ref[rows, :] * eP
        vv = v_ref[rows, :].astype(BF16)
        rest = jnp.exp(tot - P)
        bk = jnp.concatenate([bt, kt], axis=0)
        L, lak, grb, grk = [], [], [], []
        for h in both:
            hm = head0 if h == 0 else jnp.logical_not(head0)
            ar = jnp.concatenate([jnp.where(hm, at, 0.0), jnp.where(hm, rt, 0.0)], axis=0).astype(BF16)
            gm = lax.dot_general(ar, bk, NT_DIMS, preferred_element_type=F32)
            L.append(jnp.where(strict, gm[:G, :G], 0.0))
            lak.append(jnp.where(strict, gm[:G, G:], 0.0).astype(BF16))
            grb.append(jnp.where(incl, gm[G:, :G], 0.0).astype(BF16))
            grk.append(jnp.where(incl, gm[G:, G:], 0.0).astype(BF16))
        st.append(dict(at=at.astype(BF16), rt=rt, vv=vv, L=L, lak=lak, grb=grb, grk=grk,
                       bd=bb_ref[rows, :] * rest, kd=kd_ref[rows, :] * rest, dec=jnp.exp(tot)))

    for s in st:
        s["T"] = [eye + s["L"][h] for h in both]
        lb = [s["L"][h].astype(BF16) for h in both]
        s["Lp"] = [jnp.dot(lb[h], lb[h], preferred_element_type=F32).astype(BF16) for h in both]
    for lvl in range(1, shift):
        for s in st:
            for h in both:
                if lvl < shift - 1:
                    x = jnp.concatenate([s["Lp"][h], s["T"][h].astype(BF16)], axis=1)
                    res = jnp.dot(s["Lp"][h], x, preferred_element_type=F32)
                    s["Lp"][h] = res[:, :G].astype(BF16)
                    s["T"][h] = s["T"][h] + res[:, G:]
                else:
                    s["T"][h] = s["T"][h] + jnp.dot(s["Lp"][h], s["T"][h].astype(BF16), preferred_element_type=F32)
    for s in st:
        s["lv"] = sel(*[jnp.dot(s["lak"][h], s["vv"], preferred_element_type=F32) for h in both]).astype(BF16)
    for s in st:
        x = jnp.concatenate([s["at"], s["lv"]], axis=1)
        tw = [jnp.dot(s["T"][h].astype(BF16), x, preferred_element_type=F32) for h in both]
        s["tw"] = jnp.concatenate([sel(tw[0][:, :LANE], tw[1][:, :LANE]), sel(tw[0][:, LANE:], tw[1][:, LANE:])],
                                  axis=1).astype(BF16)
    for g, s in enumerate(st):
        rows = slice(g * G, (g + 1) * G)
        qy = [jnp.dot(s["grb"][h], s["tw"], preferred_element_type=F32) for h in both]
        gv = [jnp.dot(s["grk"][h], s["vv"], preferred_element_type=F32) for h in both]
        Q_o[0, rows, :] = (s["rt"] + sel(qy[0][:, :LANE], qy[1][:, :LANE])).astype(Q_o.dtype)
        Y_o[0, rows, :] = sel(qy[0][:, LANE:] + gv[0], qy[1][:, LANE:] + gv[1])
        for c in range(G // C):
            cr = slice(c * C, (c + 1) * C)
            ab = lax.dot_general(s["bd"][cr].astype(BF16), s["tw"][cr], TN_DIMS, preferred_element_type=F32)
            kv = lax.dot_general(s["kd"][cr].astype(BF16), s["vv"][cr], TN_DIMS, preferred_element_type=F32)
            k = g * (G // C) + c
            A_o[0, k, 0] = jnp.where(same_head, eye_h * s["dec"][c * C:c * C + 1] + ab[:, :LANE], 0.0).astype(A_o.dtype)
            B_o[0, k, 0] = jnp.where(same_head, ab[:, LANE:] + kv, 0.0)


def _scanprep(r, v, kk, kd, lw, bb):
    N, RW = r.shape
    hp = RW // LANE
    G = _tile(N, 2 * CHUNK)
    ngrp = 4 if N % (4 * G) == 0 else 1
    rb = G * ngrp
    nchunks = N // CHUNK
    shared = lambda: pl.BlockSpec((rb, LANE), lambda d, i, h: (i, h))
    perdir = lambda: pl.BlockSpec((rb, LANE), lambda d, i, h: (i, d * hp + h))
    rowo = lambda: pl.BlockSpec((1, rb, LANE), lambda d, i, h: (d, i, h))
    mato = lambda: pl.BlockSpec((1, rb // CHUNK, 1, LANE, LANE), lambda d, i, h: (d, i, h, 0, 0))
    return pl.pallas_call(
        functools.partial(_scanprep_kernel, G=G, ngrp=ngrp),
        grid=(2, N // rb, hp),
        in_specs=[shared(), shared(), shared(), perdir(), perdir(), perdir()],
        out_specs=[mato(), mato(), rowo(), rowo()],
        out_shape=[jax.ShapeDtypeStruct((2, nchunks, hp, LANE, LANE), BF16),
                   jax.ShapeDtypeStruct((2, nchunks, hp, LANE, LANE), F32),
                   jax.ShapeDtypeStruct((2, N, RW), BF16),
                   jax.ShapeDtypeStruct((2, N, RW), F32)],
        compiler_params=_params("parallel", "parallel", "parallel"),
        name="rwkv_chunk_maps",
    )(r, v, kk, kd, lw, bb)


def _scanserial_kernel(A_ref, B_ref, Q_ref, Y0_ref, y_ref, s_ref, *, npair):
    @pl.when(pl.program_id(2) == 0)
    def _():
        s_ref[...] = jnp.zeros_like(s_ref)

    for h in range(npair):
        ls = slice(h * LANE, (h + 1) * LANE)
        sb = s_ref[h].astype(BF16)
        aq = jnp.concatenate([A_ref[0, 0, h], Q_ref[0, :, ls]], axis=0)
        res = jnp.dot(aq, sb, preferred_element_type=F32)
        y_ref[0, :, ls] = res[LANE:] + Y0_ref[0, :, ls]
        s_ref[h] = res[:LANE] + B_ref[0, 0, h]


def _scanserial(A, Bm, Q, Y0, *, B, S, Tc):
    _, nchunks, npair, _, _ = A.shape
    C = CHUNK
    N = nchunks * C
    RW = npair * LANE
    nl, nc = S // C, Tc // C
    cbase = B * nl

    def chunk_of(d, b, p):
        kc = jnp.where(d == 0, p, nc - 1 - p)
        kl = jnp.where(d == 0, p - nc, nl - 1 - (p - nc))
        return jnp.where(p < nc, cbase + b * nc + kc, b * nl + kl)

    mat = lambda: pl.BlockSpec((1, 1, npair, LANE, LANE), lambda d, b, p: (d, chunk_of(d, b, p), 0, 0, 0))
    row = lambda: pl.BlockSpec((1, C, RW), lambda d, b, p: (d, chunk_of(d, b, p), 0))
    return pl.pallas_call(
        functools.partial(_scanserial_kernel, npair=npair),
        grid=(2, B, nl + nc),
        in_specs=[mat(), mat(), row(), row()],
        out_specs=row(),
        out_shape=jax.ShapeDtypeStruct((2, N, RW), F32),
        scratch_shapes=[pltpu.VMEM((npair, LANE, LANE), F32)],
        compiler_params=_params("parallel", "parallel", "arbitrary"),
        name="rwkv_chunk_chain",
    )(A, Bm, Q, Y0)


def _rwout_kernel(y_ref, bon_ref, g_ref, lg_ref, lb_ref, seg_ref, segt_ref, o_ref):
    y = y_ref[0] + y_ref[1]
    inv = 1.0 / RWKV_HEAD_DIM
    mu = _segsum(y, seg_ref, segt_ref) * inv
    yc = y - mu
    var = _segsum(yc * yc, seg_ref, segt_ref) * inv
    yn = yc * lax.rsqrt(var + GN_EPS) * lg_ref[...] + lb_ref[...]
    o_ref[...] = ((yn + bon_ref[...]) * g_ref[...]).astype(o_ref.dtype)


def _rwout(y2, bonus, g, p, rows):
    RW = g.shape[1]
    tm = _tile(rows, 256)
    seg = (jnp.arange(RW)[:, None] // RWKV_HEAD_DIM == jnp.arange(LANE)[None, :]).astype(BF16)
    full = lambda shape: pl.BlockSpec(shape, lambda i: (0,) * len(shape))
    return pl.pallas_call(
        _rwout_kernel,
        grid=(rows // tm,),
        in_specs=[pl.BlockSpec((2, tm, RW), lambda i: (0, i, 0)),
                  pl.BlockSpec((tm, RW), lambda i: (i, 0)),
                  pl.BlockSpec((tm, RW), lambda i: (i, 0)),
                  full((1, RW)), full((1, RW)), full((RW, LANE)), full((LANE, RW))],
        out_specs=pl.BlockSpec((tm, RW), lambda i: (i, 0)),
        out_shape=jax.ShapeDtypeStruct((rows, RW), BF16),
        compiler_params=_params("parallel"),
        name="rwkv_out",
    )(y2, bonus, g, p["rwkv_ln_g"].reshape(1, RW), p["rwkv_ln_b"].reshape(1, RW), seg, seg.T)


def _conv_kernel(first_ref, last_ref, z_ref, zp_ref, zn_ref, w_ref, b_ref, lg_ref, lb_ref, o_ref, u_ref, *, CW, K):
    i = pl.program_id(0)
    has_prev = (1 - first_ref[i]).astype(F32)
    has_next = (1 - last_ref[i]).astype(F32)
    tt = z_ref.shape[0]
    H = CONV_HALO
    glu = lambda ref: ref[:, :CW] * _sigmoid(ref[:, CW:])
    u_ref[0:H, :] = glu(zp_ref) * has_prev
    u_ref[H:H + tt, :] = glu(z_ref)
    u_ref[H + tt:, :] = glu(zn_ref) * has_next
    off = H - K // 2
    acc = jnp.zeros((tt, CW), F32) + b_ref[...]
    for k in range(K):
        acc = acc + w_ref[k:k + 1, :] * u_ref[off + k:off + k + tt, :]
    y = _ln(acc) * lg_ref[...] + lb_ref[...]
    o_ref[...] = (y * _sigmoid(y)).astype(o_ref.dtype)


def _conv(cv, p, *, B, S, Tc, rows):
    CW = cv.shape[1] // 2
    K = p["conv_w"].shape[0]
    assert K // 2 <= CONV_HALO
    tt = _tile(math.gcd(S, Tc), 256)
    hb = tt // CONV_HALO
    nhb = cv.shape[0] // CONV_HALO
    first, last = _seq_edges(B, S, Tc, tt)
    full = lambda shape: pl.BlockSpec(shape, lambda i, f, l: (0,) * len(shape))
    grid_spec = pltpu.PrefetchScalarGridSpec(
        num_scalar_prefetch=2,
        grid=(rows // tt,),
        in_specs=[pl.BlockSpec((tt, 2 * CW), lambda i, f, l: (i, 0)),
                  pl.BlockSpec((CONV_HALO, 2 * CW), lambda i, f, l: (jnp.maximum(i * hb - 1, 0), 0)),
                  pl.BlockSpec((CONV_HALO, 2 * CW), lambda i, f, l: (jnp.minimum((i + 1) * hb, nhb - 1), 0)),
                  full((K, CW)), full((1, CW)), full((1, CW)), full((1, CW))],
        out_specs=pl.BlockSpec((tt, CW), lambda i, f, l: (i, 0)),
        scratch_shapes=[pltpu.VMEM((tt + 2 * CONV_HALO, CW), F32)],
    )
    return pl.pallas_call(
        functools.partial(_conv_kernel, CW=CW, K=K),
        grid_spec=grid_spec,
        out_shape=jax.ShapeDtypeStruct((rows, CW), BF16),
        compiler_params=_params("parallel"),
        name="conv_module",
    )(first, last, cv, cv, cv, p["conv_w"], p["conv_b"].reshape(1, CW), p["conv_ln_g"].reshape(1, CW),
      p["conv_ln_b"].reshape(1, CW))


def _merge_kernel(a_ref, r_ref, c_ref, wa_ref, wr_ref, wc_ref, g0_ref, g1_ref, g2_ref, b0_ref, b1_ref, b2_ref, o_ref):
    m = _sigmoid(g0_ref[...] + b0_ref[...]) * jnp.dot(a_ref[...], wa_ref[...], preferred_element_type=F32)
    m = m + _sigmoid(g1_ref[...] + b1_ref[...]) * jnp.dot(r_ref[...], wr_ref[...], preferred_element_type=F32)
    m = m + _sigmoid(g2_ref[...] + b2_ref[...]) * jnp.dot(c_ref[...], wc_ref[...], preferred_element_type=F32)
    o_ref[...] = m.astype(o_ref.dtype)


def _merge(att, rwo, cvo, gt, wa, wr, wc, b_gate, rows):
    D = wa.shape[1]
    tm = _tile(rows, 512)
    tn = _tile(D, 512)
    nj = D // tn
    xin = lambda w: pl.BlockSpec((tm, w), lambda j, i: (i, 0))
    win = lambda w: pl.BlockSpec((w, tn), lambda j, i: (0, j))
    gate = lambda k: pl.BlockSpec((tm, tn), lambda j, i: (i, k * nj + j))
    bias = lambda k: pl.BlockSpec((1, tn), lambda j, i: (0, k * nj + j))
    bg = b_gate.reshape(1, 3 * D)
    return pl.pallas_call(
        _merge_kernel,
        grid=(nj, rows // tm),
        in_specs=[xin(att.shape[1]), xin(rwo.shape[1]), xin(cvo.shape[1]),
                  win(wa.shape[0]), win(wr.shape[0]), win(wc.shape[0]),
                  gate(0), gate(1), gate(2), bias(0), bias(1), bias(2)],
        out_specs=pl.BlockSpec((tm, tn), lambda j, i: (i, j)),
        out_shape=jax.ShapeDtypeStruct((rows, D), BF16),
        compiler_params=_params("parallel", "parallel"),
        name="branch_merge",
    )(att, rwo, cvo, wa, wr, wc, gt, gt, gt, bg, bg, bg)


def _outproj_kernel(m_ref, w_ref, x_ref, gm_ref, sh_ref, sc_ref, lg_ref, lb_ref, wrt_ref, brt_ref,
                    x_o, h_o, lgt_o, *, alpha):
    y = jnp.dot(m_ref[...], w_ref[...], preferred_element_type=F32)
    x1 = _ln(alpha * x_ref[...] + gm_ref[0] * y) * lg_ref[...] + lb_ref[...]
    x_o[...] = x1
    h2 = _ln(x1) * (1.0 + sc_ref[0]) + sh_ref[0]
    h_o[...] = h2.astype(h_o.dtype)
    hi = h2.astype(BF16)
    lo = (h2 - hi.astype(F32)).astype(BF16)
    r1 = jnp.dot(hi, wrt_ref[...], preferred_element_type=F32)
    r2 = jnp.dot(lo, wrt_ref[:, :LANE], preferred_element_type=F32)
    lgt_o[...] = r1[:, :LANE] + r1[:, LANE:] + r2 + brt_ref[...]


def _outproj(m, w_out, xa, mods3, ln_g, ln_b, w_rt, b_rt, rows, modrow, tm, alpha):
    D = w_out.shape[0]
    full = lambda shape: pl.BlockSpec(shape, lambda i: (0,) * len(shape))
    rowb = lambda w: pl.BlockSpec((tm, w), lambda i: (i, 0))
    mod = lambda blk: pl.BlockSpec((1, 1, D), lambda i: (modrow(i), 0, blk))
    return pl.pallas_call(
        functools.partial(_outproj_kernel, alpha=alpha),
        grid=(rows // tm,),
        in_specs=[rowb(D), full((D, D)), rowb(D), mod(2), mod(3), mod(4), full((1, D)), full((1, D)),
                  full((D, 2 * LANE)), full((1, LANE))],
        out_specs=[rowb(D), rowb(D), rowb(LANE)],
        out_shape=[jax.ShapeDtypeStruct((rows, D), F32), jax.ShapeDtypeStruct((rows, D), BF16),
                   jax.ShapeDtypeStruct((rows, LANE), F32)],
        compiler_params=_params("parallel"),
        name="out_proj",
    )(m, w_out, xa, mods3, mods3, mods3, ln_g.reshape(1, D), ln_b.reshape(1, D), w_rt, b_rt)


def _route_kernel(l_ref, id_o, w_o):
    lg = l_ref[...]
    lane = lax.broadcasted_iota(jnp.int32, lg.shape, 1)
    neg = -jnp.inf
    big = jnp.int32(2 ** 30)
    isg = jnp.logical_and(lane >= N_EXPERTS, lane < N_EXPERTS + N_GROUPS)
    gl = jnp.where(isg, lg, neg)
    gmax = jnp.max(gl, -1, keepdims=True)
    grp = jnp.min(jnp.where(gl == gmax, lane - N_EXPERTS, big), -1, keepdims=True)
    grp_w = 1.0 / jnp.sum(jnp.exp(gl - gmax), -1, keepdims=True)
    ing = jnp.logical_and(lane < N_EXPERTS, jnp.right_shift(lane, int(math.log2(EXPERTS_PER_GROUP))) == grp)
    el = jnp.where(ing, lg, neg)
    e1 = jnp.max(el, -1, keepdims=True)
    i1 = jnp.min(jnp.where(el == e1, lane, big), -1, keepdims=True)
    psum = jnp.sum(jnp.exp(el - e1), -1, keepdims=True)
    el2 = jnp.where(lane == i1, neg, el)
    e2 = jnp.max(el2, -1, keepdims=True)
    i2 = jnp.min(jnp.where(el2 == e2, lane, big), -1, keepdims=True)
    p1 = 1.0 / psum
    p2 = jnp.exp(e2 - e1) / psum
    tot = p1 + p2
    id_o[...] = jnp.where(lane == 0, i1, jnp.where(lane == 1, i2, 0))
    w_o[...] = jnp.where(lane == 0, grp_w * p1 / tot, jnp.where(lane == 1, grp_w * p2 / tot, 0.0))


def _route(logits):
    n = logits.shape[0]
    tm = _tile(n, 1024)
    blk = lambda: pl.BlockSpec((tm, LANE), lambda i: (i, 0))
    return pl.pallas_call(
        _route_kernel,
        grid=(n // tm,),
        in_specs=[blk()],
        out_specs=[blk(), blk()],
        out_shape=[jax.ShapeDtypeStruct((n, LANE), jnp.int32), jax.ShapeDtypeStruct((n, LANE), F32)],
        compiler_params=_params("parallel"),
        name="route",
    )(logits)


def _moe_kernel(be_ref, act_ref, x_ref, wg_ref, wu_ref, wd_ref, o_ref):
    i = pl.program_id(0)

    @pl.when(act_ref[i] > 0)
    def _():
        x = x_ref[...]
        gte = jnp.dot(x, wg_ref[0], preferred_element_type=F32)
        up = jnp.dot(x, wu_ref[0], preferred_element_type=F32)
        h = (gte * _sigmoid(gte) * up).astype(BF16)
        o_ref[...] = jnp.dot(h, wd_ref[0], preferred_element_type=F32)

    @pl.when(act_ref[i] == 0)
    def _():
        o_ref[...] = jnp.zeros_like(o_ref)


def _moe_ffn(xg, block_e, active, wg, wu, wd):
    R, D = xg.shape
    FF = wg.shape[2]
    nb = R // MOE_BLOCK
    grid_spec = pltpu.PrefetchScalarGridSpec(
        num_scalar_prefetch=2,
        grid=(nb,),
        in_specs=[pl.BlockSpec((MOE_BLOCK, D), lambda i, be, ac: (i, 0)),
                  pl.BlockSpec((1, D, FF), lambda i, be, ac: (be[i], 0, 0)),
                  pl.BlockSpec((1, D, FF), lambda i, be, ac: (be[i], 0, 0)),
                  pl.BlockSpec((1, FF, D), lambda i, be, ac: (be[i], 0, 0))],
        out_specs=pl.BlockSpec((MOE_BLOCK, D), lambda i, be, ac: (i, 0)),
    )
    return pl.pallas_call(
        _moe_kernel,
        grid_spec=grid_spec,
        out_shape=jax.ShapeDtypeStruct((R, D), F32),
        compiler_params=_params("arbitrary"),
        name="moe_ffn",
    )(block_e, active, xg, wg, wu, wd)


def _dispatch(ids):
    n = ids.shape[0]
    A = 2 * n
    flat_e = ids.reshape(-1)
    order = jnp.argsort(flat_e)
    se = flat_e[order]
    counts = jnp.bincount(flat_e, length=N_EXPERTS)
    padded = (counts + MOE_BLOCK - 1) // MOE_BLOCK * MOE_BLOCK
    pad_end = jnp.cumsum(padded)
    pad_start = pad_end - padded
    start = jnp.cumsum(counts) - counts
    dest_sorted = (pad_start[se] + jnp.arange(A) - start[se]).astype(jnp.int32)
    nb = -(-A // MOE_BLOCK) + N_EXPERTS
    slot_tok = jnp.zeros((nb * MOE_BLOCK,), jnp.int32).at[dest_sorted].set((order // 2).astype(jnp.int32))
    dest = jnp.zeros((A,), jnp.int32).at[order].set(dest_sorted)
    blk_start = jnp.arange(nb) * MOE_BLOCK
    block_e = jnp.minimum(jnp.searchsorted(pad_end, blk_start, side="right"), N_EXPERTS - 1).astype(jnp.int32)
    active = (blk_start < pad_end[-1]).astype(jnp.int32)
    return slot_tok, dest.reshape(n, 2), block_e, active


def _final_kernel(x_ref, y0_ref, y1_ref, w_ref, gm_ref, lg_ref, lb_ref, o_ref, *, alpha):
    w = w_ref[...]
    f = w[:, 0:1] * y0_ref[...] + w[:, 1:2] * y1_ref[...]
    o_ref[...] = _ln(alpha * x_ref[...] + gm_ref[0] * f) * lg_ref[...] + lb_ref[...]


def _final(x1, y0, y1, wts, mods3, ln_g, ln_b, rows, modrow, tm, alpha):
    D = x1.shape[1]
    rowb = lambda w: pl.BlockSpec((tm, w), lambda i: (i, 0))
    full = lambda shape: pl.BlockSpec(shape, lambda i: (0,) * len(shape))
    return pl.pallas_call(
        functools.partial(_final_kernel, alpha=alpha),
        grid=(rows // tm,),
        in_specs=[rowb(D), rowb(D), rowb(D), rowb(LANE),
                  pl.BlockSpec((1, 1, D), lambda i: (modrow(i), 0, 5)), full((1, D)), full((1, D))],
        out_specs=rowb(D),
        out_shape=jax.ShapeDtypeStruct((rows, D), F32),
        compiler_params=_params("parallel"),
        name="ffn_residual",
    )(x1, y0, y1, wts, mods3, ln_g.reshape(1, D), ln_b.reshape(1, D))


def _rope_tables(S):
    rows = S // GRID_W
    row = jnp.repeat(jnp.arange(rows, dtype=F32), GRID_W)
    col = jnp.tile(jnp.arange(GRID_W, dtype=F32), rows)
    nf = ATT_HEAD_DIM // 4
    inv = ROPE_THETA ** (-jnp.arange(nf, dtype=F32) / nf)
    ar, ac = row[:, None] * inv, col[:, None] * inv
    cos = jnp.concatenate([jnp.cos(ar), jnp.cos(ar), jnp.cos(ac), jnp.cos(ac)], -1)
    sin = jnp.concatenate([-jnp.sin(ar), jnp.sin(ar), -jnp.sin(ac), jnp.sin(ac)], -1)
    return cos, sin


def kernel(x, c, ctx, c_ctx, w_mod, b_mod, w_in, b_gate, q_norm, k_norm, w_att_o, rwkv_mu, rwkv_w0, rwkv_w2, rwkv_a0, rwkv_a2, rwkv_g2, rwkv_k_k, rwkv_k_a, rwkv_r_k, rwkv_ln_g, rwkv_ln_b, w_rwkv_o, conv_w, conv_b, conv_ln_g, conv_ln_b, w_conv_o, w_out, ln1_g, ln1_b, w_group, b_group, w_router, b_router, w_e_gate, w_e_up, w_e_down, ln2_g, ln2_b):
    B, S, D = x.shape
    Tc = ctx.shape[1]
    depth = w_mod.shape[0]
    NL, NC = B * S, B * Tc
    N = NL + NC
    alpha = (2 * depth) ** 0.25
    att_w = w_att_o.shape[1]
    heads = att_w // ATT_HEAD_DIM
    rw_in = rwkv_mu.shape[-1]
    conv_cw = w_conv_o.shape[1]
    kv_w = (w_in.shape[-1] - att_w - rw_in - 2 * conv_cw - 3 * D) // 2
    assert kv_w == heads // ATT_GROUP * ATT_HEAD_DIM
    o_rw = att_w + 2 * kv_w
    o_cv = o_rw + rw_in
    o_gt = o_cv + 2 * conv_cw

    tmod = _tile(math.gcd(S, Tc), 256)
    tps = S // tmod
    nlt = NL // tmod
    modrow = lambda i: jnp.where(i < nlt, i // tps, B)
    cos_t, sin_t = _rope_tables(S)

    R = -(-(B + 1) // SUBLANE) * SUBLANE
    cv = jnp.zeros((R, D), F32).at[:B].set(c).at[B].set(c_ctx)
    xa = jnp.concatenate([x.reshape(NL, D), ctx.reshape(NC, D)], 0)

    for l in range(depth):
        last = l == depth - 1
        rows = NL if last else N
        p = dict(rwkv_mu=rwkv_mu[l], rwkv_w0=rwkv_w0[l], rwkv_w2=rwkv_w2[l], rwkv_a0=rwkv_a0[l], rwkv_a2=rwkv_a2[l],
                 rwkv_g2=rwkv_g2[l], rwkv_k_k=rwkv_k_k[l], rwkv_k_a=rwkv_k_a[l], rwkv_r_k=rwkv_r_k[l],
                 rwkv_ln_g=rwkv_ln_g[l], rwkv_ln_b=rwkv_ln_b[l], conv_w=conv_w[l], conv_b=conv_b[l],
                 conv_ln_g=conv_ln_g[l], conv_ln_b=conv_ln_b[l])
        mods3 = _adaln(cv, w_mod[l], b_mod[l]).reshape(R, 1, 6 * D)
        wl = w_in[l]
        h = _lnmod(xa, mods3, 0, 1, N, modrow, tmod)
        qkv = _matmul(h, wl[:, :o_rw].astype(BF16), 512)
        rw = _matmul(h, wl[:, o_rw:o_cv].astype(BF16), 1152)
        cvx = _matmul(h, wl[:, o_cv:o_gt].astype(BF16), 1024, rows)
        gt = _matmul(h, wl[:, o_gt:].astype(BF16), 1024, rows)

        qn, kn = q_norm[l].reshape(1, LANE), k_norm[l].reshape(1, LANE)
        att = _attention(qkv, qn, kn, cos_t, sin_t, B=B, S=S, Tc=Tc, heads=heads, latent_queries=True)
        if not last:
            att_c = _attention(qkv, qn, kn, cos_t, sin_t, B=B, S=S, Tc=Tc, heads=heads, latent_queries=False)
            att = jnp.concatenate([att, att_c], 0)

        r, v, kk, kd, lw, bb, g, bonus = _rwprep(rw, p, B=B, S=S, Tc=Tc)
        Am, Bm, Qm, Y0 = _scanprep(r, v, kk, kd, lw, bb)
        y2 = _scanserial(Am, Bm, Qm, Y0, B=B, S=S, Tc=Tc)
        rwo = _rwout(y2, bonus, g, p, rows)
        cvo = _conv(cvx, p, B=B, S=S, Tc=Tc, rows=rows)

        m = _merge(att, rwo, cvo, gt, w_att_o[l].astype(BF16), w_rwkv_o[l].astype(BF16), w_conv_o[l].astype(BF16),
                   b_gate[l], rows)
        w_rt = jnp.zeros((D, LANE), F32).at[:, :N_EXPERTS].set(w_router[l]).at[:, N_EXPERTS:N_EXPERTS + N_GROUPS].set(w_group[l])
        b_rt = jnp.zeros((1, LANE), F32).at[0, :N_EXPERTS].set(b_router[l]).at[0, N_EXPERTS:N_EXPERTS + N_GROUPS].set(b_group[l])
        w_rt_hi = w_rt.astype(BF16)
        w_rt = jnp.concatenate([w_rt_hi, (w_rt - w_rt_hi.astype(F32)).astype(BF16)], 1)
        x1, h2, logits = _outproj(m, w_out[l].astype(BF16), xa, mods3, ln1_g[l], ln1_b[l], w_rt, b_rt, rows, modrow,
                                  tmod, alpha)

        ids128, wts128 = _route(logits)
        slot_tok, dest, block_e, active = _dispatch(ids128[:, :2])
        xg = jnp.take(h2, slot_tok, axis=0, mode="clip")
        yb = _moe_ffn(xg, block_e, active, w_e_gate[l].astype(BF16), w_e_up[l].astype(BF16), w_e_down[l].astype(BF16))
        y0 = jnp.take(yb, dest[:, 0], axis=0, mode="clip")
        y1 = jnp.take(yb, dest[:, 1], axis=0, mode="clip")
        xa = _final(x1, y0, y1, wts128, mods3, ln2_g[l], ln2_b[l], rows, modrow, tmod, alpha)

    return xa[:NL].reshape(B, S, D)
```

```python
import functools
import math

import numpy as np
import jax
import jax.numpy as jnp
from jax import lax
from jax.experimental import pallas as pl
from jax.experimental.pallas import tpu as pltpu

F32 = jnp.float32
BF16 = jnp.bfloat16
HI = lax.Precision.HIGHEST

LANE = 128
SUBLANE = 8
ATT_HEAD_DIM = 128
ATT_GROUP = 4
ROPE_THETA = 10000.0
GRID_W = 64
RWKV_HEAD_DIM = 64
CHUNK = 64
CONV_HALO = 16
SHIFT_HALO = 8
N_GROUPS = 4
EXPERTS_PER_GROUP = 8
N_EXPERTS = N_GROUPS * EXPERTS_PER_GROUP
MOE_BLOCK = 256
GN_EPS = 64e-5
LN_EPS = 1e-6
RMS_EPS = 1e-6
VMEM_LIMIT = 48 * 1024 * 1024

NT_DIMS = (((1,), (1,)), ((), ()))
TN_DIMS = (((0,), (0,)), ((), ()))


def _params(*sem):
    return pltpu.CompilerParams(dimension_semantics=sem, vmem_limit_bytes=VMEM_LIMIT)


def _sigmoid(x):
    return 1.0 / (1.0 + jnp.exp(-x))


def _ln(x):
    mu = jnp.mean(x, -1, keepdims=True)
    xc = x - mu
    var = jnp.mean(xc * xc, -1, keepdims=True)
    return xc * lax.rsqrt(var + LN_EPS)


def _tile(n, pref):
    t = min(n, pref)
    while n % t:
        t -= SUBLANE
    return t


def _adaln_kernel(c_ref, w_ref, b_ref, o_ref):
    c = c_ref[...]
    s = c * _sigmoid(c)
    o_ref[...] = jnp.dot(s, w_ref[...], precision=HI, preferred_element_type=F32) + b_ref[...]


def _adaln(cv, w_mod, b_mod):
    R, D = cv.shape
    W = w_mod.shape[1]
    tn = _tile(W, 1024)
    return pl.pallas_call(
        _adaln_kernel,
        grid=(W // tn,),
        in_specs=[pl.BlockSpec((R, D), lambda j: (0, 0)),
                  pl.BlockSpec((D, tn), lambda j: (0, j)),
                  pl.BlockSpec((1, tn), lambda j: (0, j))],
        out_specs=pl.BlockSpec((R, tn), lambda j: (0, j)),
        out_shape=jax.ShapeDtypeStruct((R, W), F32),
        compiler_params=_params("parallel"),
        name="adaln",
    )(cv, w_mod, b_mod.reshape(1, W))


def _lnmod_kernel(x_ref, sh_ref, sc_ref, o_ref):
    o_ref[...] = (_ln(x_ref[...]) * (1.0 + sc_ref[0]) + sh_ref[0]).astype(o_ref.dtype)


def _lnmod(xa, mods3, shift_blk, scale_blk, rows, modrow, tm):
    D = xa.shape[1]
    return pl.pallas_call(
        _lnmod_kernel,
        grid=(rows // tm,),
        in_specs=[pl.BlockSpec((tm, D), lambda i: (i, 0)),
                  pl.BlockSpec((1, 1, D), lambda i: (modrow(i), 0, shift_blk)),
                  pl.BlockSpec((1, 1, D), lambda i: (modrow(i), 0, scale_blk))],
        out_specs=pl.BlockSpec((tm, D), lambda i: (i, 0)),
        out_shape=jax.ShapeDtypeStruct((rows, D), BF16),
        compiler_params=_params("parallel"),
        name="lnmod",
    )(xa, mods3, mods3)


def _mm_kernel(x_ref, w_ref, o_ref):
    o_ref[...] = jnp.dot(x_ref[...], w_ref[...], preferred_element_type=F32).astype(o_ref.dtype)


def _matmul(x, w, tn_pref, rows=None, out_dtype=F32, tm_pref=1024):
    M, K = x.shape
    M = rows or M
    N = w.shape[1]
    tm = _tile(M, tm_pref)
    tn = _tile(N, tn_pref)
    return pl.pallas_call(
        _mm_kernel,
        grid=(N // tn, M // tm),
        in_specs=[pl.BlockSpec((tm, K), lambda j, i: (i, 0)),
                  pl.BlockSpec((K, tn), lambda j, i: (0, j))],
        out_specs=pl.BlockSpec((tm, tn), lambda j, i: (i, j)),
        out_shape=jax.ShapeDtypeStruct((M, N), out_dtype),
        compiler_params=_params("parallel", "parallel"),
        name="matmul",
    )(x, w)


def _rms(t, g):
    return t * lax.rsqrt(jnp.mean(t * t, -1, keepdims=True) + RMS_EPS) * g


def _rope(t, cos, sin_signed):
    lane = lax.broadcasted_iota(jnp.int32, t.shape, 1)
    first = (lane & 63) < 32
    partner = jnp.where(first, pltpu.roll(t, LANE - 32, 1), pltpu.roll(t, 32, 1))
    return t * cos + partner * sin_signed


def _attn_kernel(*refs, rope, has_lat, tq, scale):
    if has_lat:
        (q_ref, kc_ref, vc_ref, qn_ref, kn_ref, kl_ref, vl_ref, cq_ref, sq_ref, ck_ref, sk_ref,
         o_ref, kcb, vcb, klb, vlb) = refs
    else:
        q_ref, kc_ref, vc_ref, qn_ref, kn_ref, o_ref, kcb, vcb = refs

    @pl.when(pl.program_id(2) == 0)
    def _():
        kcb[...] = _rms(kc_ref[...], kn_ref[...]).astype(BF16)
        vcb[:, :LANE] = vc_ref[...].astype(BF16)
        vcb[:, LANE:] = jnp.ones(vc_ref.shape, BF16)
        if has_lat:
            klb[...] = _rope(_rms(kl_ref[...], kn_ref[...]), ck_ref[...], sk_ref[...]).astype(BF16)
            vlb[:, :LANE] = vl_ref[...].astype(BF16)
            vlb[:, LANE:] = jnp.ones(vl_ref.shape, BF16)

    qs = []
    for g in range(ATT_GROUP):
        qg = _rms(q_ref[:, g * LANE:(g + 1) * LANE], qn_ref[...])
        if rope:
            qg = _rope(qg, cq_ref[...], sq_ref[...])
        qs.append((qg * (scale * math.log2(math.e))).astype(BF16))
    qa = jnp.concatenate(qs, axis=0)
    sc = lax.dot_general(qa, kcb[...], NT_DIMS, preferred_element_type=F32)
    m = jnp.max(sc, -1, keepdims=True)
    if has_lat:
        sl = lax.dot_general(qa, klb[...], NT_DIMS, preferred_element_type=F32)
        m = jnp.maximum(m, jnp.max(sl, -1, keepdims=True))
    od = jnp.dot(jnp.exp2(sc - m).astype(BF16), vcb[...], preferred_element_type=F32)
    if has_lat:
        od = od + jnp.dot(jnp.exp2(sl - m).astype(BF16), vlb[...], preferred_element_type=F32)
    o = od[:, :LANE] / od[:, LANE:]
    for g in range(ATT_GROUP):
        o_ref[:, g * LANE:(g + 1) * LANE] = o[g * tq:(g + 1) * tq].astype(o_ref.dtype)


def _attention(qkv, qn, kn, cos_t, sin_t, *, B, S, Tc, heads, latent_queries):
    NL = B * S
    kvh = heads // ATT_GROUP
    GW = ATT_GROUP * LANE
    scale = ATT_HEAD_DIM ** -0.5
    Tq = S if latent_queries else Tc
    tq = _tile(Tq, 128)
    nq = Tq // tq
    qbase = 0 if latent_queries else NL // tq
    cbase = NL // Tc
    common = [pl.BlockSpec((tq, GW), lambda b, j, i: (qbase + b * nq + i, j)),
              pl.BlockSpec((Tc, LANE), lambda b, j, i: (cbase + b, heads + j)),
              pl.BlockSpec((Tc, LANE), lambda b, j, i: (cbase + b, heads + kvh + j)),
              pl.BlockSpec((1, LANE), lambda b, j, i: (0, 0)),
              pl.BlockSpec((1, LANE), lambda b, j, i: (0, 0))]
    args = [qkv, qkv, qkv, qn, kn]
    scratch = [pltpu.VMEM((Tc, LANE), BF16), pltpu.VMEM((Tc, 2 * LANE), BF16)]
    if latent_queries:
        common += [pl.BlockSpec((S, LANE), lambda b, j, i: (b, heads + j)),
                   pl.BlockSpec((S, LANE), lambda b, j, i: (b, heads + kvh + j)),
                   pl.BlockSpec((tq, LANE), lambda b, j, i: (i, 0)),
                   pl.BlockSpec((tq, LANE), lambda b, j, i: (i, 0)),
                   pl.BlockSpec((S, LANE), lambda b, j, i: (0, 0)),
                   pl.BlockSpec((S, LANE), lambda b, j, i: (0, 0))]
        args += [qkv, qkv, cos_t, sin_t, cos_t, sin_t]
        scratch += [pltpu.VMEM((S, LANE), BF16), pltpu.VMEM((S, 2 * LANE), BF16)]
    return pl.pallas_call(
        functools.partial(_attn_kernel, rope=latent_queries, has_lat=latent_queries, tq=tq, scale=scale),
        grid=(B, kvh, nq),
        in_specs=common,
        out_specs=pl.BlockSpec((tq, GW), lambda b, j, i: (b * nq + i, j)),
        out_shape=jax.ShapeDtypeStruct((B * Tq, heads * LANE), BF16),
        scratch_shapes=scratch,
        compiler_params=_params("parallel", "parallel", "arbitrary"),
        name="attention_lat" if latent_queries else "attention_ctx",
    )(*args)


def _segsum(x, seg_ref, segt_ref):
    return _split_dot(_split_dot(x, seg_ref[...]), segt_ref[...])


def _split_dot(x, w):
    hi = x.astype(BF16)
    lo = (x - hi.astype(F32)).astype(BF16)
    return jnp.dot(hi, w, preferred_element_type=F32) + jnp.dot(lo, w, preferred_element_type=F32)


def _rwprep_kernel(first_ref, last_ref, z_ref, zp_ref, zn_ref, mu_ref, w0_ref, w2_ref, a0_ref, a2_ref, g2_ref,
                   kkp_ref, ka_ref, rk_ref, seg_ref, segt_ref,
                   r_o, v_o, kk_o, kd_o, lw_o, bb_o, g_o, bon_o, *, RW, R2W, R2A, RG):
    i = pl.program_id(0)
    has_prev = (1 - first_ref[i]).astype(F32)
    has_next = (1 - last_ref[i]).astype(F32)
    tt = z_ref.shape[0]
    row = lax.broadcasted_iota(jnp.int32, (tt, 1), 0)

    def shifted(lo, hi):
        z = z_ref[:, lo:hi]
        hp = zp_ref[SHIFT_HALO - 1:SHIFT_HALO, lo:hi] * has_prev
        hn = zn_ref[0:1, lo:hi] * has_next
        prev = jnp.where(row == 0, hp, pltpu.roll(z, 1, 0))
        nxt = jnp.where(row == tt - 1, hn, pltpu.roll(z, tt - 1, 0))
        return z + mu_ref[0:1, lo:hi] * (prev - z) + mu_ref[1:2, lo:hi] * (nxt - z)

    o = 3 * RW
    r = shifted(0, RW)
    k = shifted(RW, 2 * RW)
    v = shifted(2 * RW, o)
    wl = shifted(o, o + R2W)
    al = shifted(o + R2W, o + R2W + R2A)
    gl = shifted(o + R2W + R2A, o + R2W + R2A + RG)

    u = -(w0_ref[...] + _bdot(jnp.tanh(wl), w2_ref[...]))
    softplus = jnp.maximum(u, 0.0) + jnp.log(1.0 + jnp.exp(-jnp.abs(u)))
    lw_o[...] = -jnp.exp(-softplus - 0.5)
    a = _sigmoid(a0_ref[...] + _bdot(al, a2_ref[...]))
    g_o[...] = _bdot(_sigmoid(gl), g2_ref[...])

    kk = k * kkp_ref[...]
    kk = kk / jnp.maximum(jnp.sqrt(_segsum(kk * kk, seg_ref, segt_ref)), 1e-12)
    ka = ka_ref[...]
    kd0 = k * (1.0 + (a[:, :RW] - 1.0) * ka)
    kd1 = k * (1.0 + (a[:, RW:] - 1.0) * ka)
    r_o[...] = r
    v_o[...] = v
    kk_o[...] = kk
    kd_o[:, :RW] = kd0
    kd_o[:, RW:] = kd1
    bb_o[:, :RW] = kk * a[:, :RW]
    bb_o[:, RW:] = kk * a[:, RW:]
    bon_o[...] = _segsum(r * rk_ref[...] * (kd0 + kd1), seg_ref, segt_ref) * v


def _blockdiag2(w):
    z = jnp.zeros_like(w[0])
    return jnp.concatenate([jnp.concatenate([w[0], z], 1), jnp.concatenate([z, w[1]], 1)], 0)


def _seq_edges(B, S, Tc, tt):
    first, last = [], []
    for n, L in ((B, S), (B, Tc)):
        per = L // tt
        for _ in range(n):
            first += [1] + [0] * (per - 1)
            last += [0] * (per - 1) + [1]
    return jnp.asarray(np.array(first, np.int32)), jnp.asarray(np.array(last, np.int32))


def _rwprep(rw, p, *, B, S, Tc):
    N, RWIN = rw.shape
    RW = p["rwkv_k_k"].shape[0]
    R2W = 2 * p["rwkv_w2"].shape[1]
    R2A = 2 * p["rwkv_a2"].shape[1]
    RG = p["rwkv_g2"].shape[0]
    tt = _tile(math.gcd(S, Tc), 128)
    nt = N // tt
    hb = tt // SHIFT_HALO
    nhb = N // SHIFT_HALO
    first, last = _seq_edges(B, S, Tc, tt)
    nh = RW // RWKV_HEAD_DIM
    seg = (jnp.arange(RW)[:, None] // RWKV_HEAD_DIM == jnp.arange(LANE)[None, :]).astype(BF16)
    full = lambda shape: pl.BlockSpec(shape, lambda i, f, l: (0,) * len(shape))
    row = lambda w: pl.BlockSpec((tt, w), lambda i, f, l: (i, 0))
    out_w = [RW, RW, RW, 2 * RW, 2 * RW, 2 * RW, RW, RW]
    assert nh <= LANE
    grid_spec = pltpu.PrefetchScalarGridSpec(
        num_scalar_prefetch=2,
        grid=(nt,),
        in_specs=[row(RWIN),
                  pl.BlockSpec((SHIFT_HALO, RWIN), lambda i, f, l: (jnp.maximum(i * hb - 1, 0), 0)),
                  pl.BlockSpec((SHIFT_HALO, RWIN), lambda i, f, l: (jnp.minimum((i + 1) * hb, nhb - 1), 0)),
                  full((2, RWIN)), full((1, 2 * RW)), full((R2W, 2 * RW)), full((1, 2 * RW)), full((R2A, 2 * RW)),
                  full((RG, RW)), full((1, RW)), full((1, RW)), full((1, RW)), full((RW, LANE)), full((LANE, RW))],
        out_specs=[row(w) for w in out_w],
    )
    return pl.pallas_call(
        functools.partial(_rwprep_kernel, RW=RW, R2W=R2W, R2A=R2A, RG=RG),
        grid_spec=grid_spec,
        out_shape=[jax.ShapeDtypeStruct((N, w), F32) for w in out_w],
        compiler_params=_params("parallel"),
        name="rwkv_prep",
    )(first, last, rw, rw, rw, p["rwkv_mu"], p["rwkv_w0"].reshape(1, 2 * RW), _blockdiag2(p["rwkv_w2"]),
      p["rwkv_a0"].reshape(1, 2 * RW), _blockdiag2(p["rwkv_a2"]), p["rwkv_g2"],
      p["rwkv_k_k"].reshape(1, RW), p["rwkv_k_a"].reshape(1, RW), p["rwkv_r_k"].reshape(1, RW), seg, seg.T)


def _bdot(a, b):
    return jnp.dot(a.astype(BF16), b.astype(BF16), preferred_element_type=F32)


def _scanprep_kernel(r_ref, v_ref, kk_ref, kd_ref, lw_ref, bb_ref, A_o, B_o, Q_o, Y_o, *, G, ngrp):
    C = CHUNK
    HD = RWKV_HEAD_DIM
    shift = int(math.log2(C))
    hshift = int(math.log2(HD))
    sign = 1 - 2 * pl.program_id(0)
    ti = lax.broadcasted_iota(jnp.int32, (G, G), 0)
    si = lax.broadcasted_iota(jnp.int32, (G, G), 1)
    same = jnp.right_shift(ti, shift) == jnp.right_shift(si, shift)
    strict = jnp.logical_and(same, sign * (ti - si) > 0)
    ti2 = lax.broadcasted_iota(jnp.int32, (G, 2 * G), 0)
    si2 = jnp.bitwise_and(lax.broadcasted_iota(jnp.int32, (G, 2 * G), 1), G - 1)
    incl2 = jnp.logical_and(jnp.right_shift(ti2, shift) == jnp.right_shift(si2, shift), sign * (ti2 - si2) >= 0)
    eye = (si == ti).astype(F32)
    head0 = lax.broadcasted_iota(jnp.int32, (G, LANE), 1) < HD
    hi2 = lax.broadcasted_iota(jnp.int32, (LANE, LANE), 0)
    hj2 = lax.broadcasted_iota(jnp.int32, (LANE, LANE), 1)
    same_head = jnp.right_shift(hi2, hshift) == jnp.right_shift(hj2, hshift)
    eye_h = (hi2 == hj2).astype(F32)
    sel = lambda a0, a1: jnp.where(head0, a0, a1)
    both = (0, 1)
    fwd = (pl.program_id(0) == 0).astype(F32)
    trow = jnp.bitwise_and(lax.broadcasted_iota(jnp.int32, (G, LANE), 0), C - 1)

    st = []
    for g in range(ngrp):
        rows = slice(g * G, (g + 1) * G)
        lw = lw_ref[rows, :]
        F = lw
        for k in (1 << i for i in range(shift)):
            F = F + jnp.where(trow >= k, pltpu.roll(F, k, 0), 0.0)
        tot = jnp.concatenate([jnp.broadcast_to(F[(c + 1) * C - 1:(c + 1) * C], (C, LANE)) for c in range(G // C)], axis=0)
        P = fwd * F + (1.0 - fwd) * (tot - F + lw)
        eP = jnp.exp(P)
        enP = jnp.exp(-P)
        at = -kk_ref[rows, :] * jnp.exp(P - lw)
        bt = (bb_ref[rows, :] * enP).astype(BF16)
        kt = (kd_ref[rows, :] * enP).astype(BF16)
        rt = r_ref[rows, :] * eP
        vv = v_ref[rows, :].astype(BF16)
        rest = jnp.exp(tot - P)
        bk = jnp.concatenate([bt, kt], axis=0)
        L, lak, grbk = [], [], []
        for h in both:
            hm = head0 if h == 0 else jnp.logical_not(head0)
            ar = jnp.concatenate([jnp.where(hm, at, 0.0), jnp.where(hm, rt, 0.0)], axis=0).astype(BF16)
            gm = lax.dot_general(ar, bk, NT_DIMS, preferred_element_type=F32)
            L.append(jnp.where(strict, gm[:G, :G], 0.0))
            lak.append(jnp.where(strict, gm[:G, G:], 0.0).astype(BF16))
            grbk.append(jnp.where(incl2, gm[G:], 0.0).astype(BF16))
        st.append(dict(at=at.astype(BF16), rt=rt, vv=vv, L=L, lak=lak, grbk=grbk,
                       bkd=[jnp.concatenate([(bb_ref[rows, :] * rest)[c * C:(c + 1) * C],
                                             (kd_ref[rows, :] * rest)[c * C:(c + 1) * C]], axis=0).astype(BF16)
                            for c in range(G // C)],
                       dec=jnp.exp(tot)))

    for s in st:
        s["T"] = [eye + s["L"][h] for h in both]
        lb = [s["L"][h].astype(BF16) for h in both]
        s["Lp"] = [jnp.dot(lb[h], lb[h], preferred_element_type=F32).astype(BF16) for h in both]
    for lvl in range(1, shift):
        for s in st:
            for h in both:
                if lvl < shift - 1:
                    x = jnp.concatenate([s["Lp"][h], s["T"][h].astype(BF16)], axis=1)
                    res = jnp.dot(s["Lp"][h], x, preferred_element_type=F32)
                    s["Lp"][h] = res[:, :G].astype(BF16)
                    s["T"][h] = s["T"][h] + res[:, G:]
                else:
                    s["T"][h] = s["T"][h] + jnp.dot(s["Lp"][h], s["T"][h].astype(BF16), preferred_element_type=F32)
    for s in st:
        s["lv"] = sel(*[jnp.dot(s["lak"][h], s["vv"], preferred_element_type=F32) for h in both]).astype(BF16)
    for s in st:
        x = jnp.concatenate([s["at"], s["lv"]], axis=1)
        tw = [jnp.dot(s["T"][h].astype(BF16), x, preferred_element_type=F32) for h in both]
        s["tw"] = jnp.concatenate([sel(tw[0][:, :LANE], tw[1][:, :LANE]), sel(tw[0][:, LANE:], tw[1][:, LANE:])],
                                  axis=1).astype(BF16)
    for g, s in enumerate(st):
        rows = slice(g * G, (g + 1) * G)
        zv = jnp.concatenate([jnp.zeros((G, LANE), BF16), s["vv"]], axis=1)
        rhs = jnp.concatenate([s["tw"], zv], axis=0)
        qy = [jnp.dot(s["grbk"][h], rhs, preferred_element_type=F32) for h in both]
        Q_o[0, rows, :] = (s["rt"] + sel(qy[0][:, :LANE], qy[1][:, :LANE])).astype(Q_o.dtype)
        Y_o[0, rows, :] = sel(qy[0][:, LANE:], qy[1][:, LANE:])
        for c in range(G // C):
            cr = slice(c * C, (c + 1) * C)
            rc = jnp.concatenate([s["tw"][cr], zv[cr]], axis=0)
            ab = lax.dot_general(s["bkd"][c], rc, TN_DIMS, preferred_element_type=F32)
            k = g * (G // C) + c
            A_o[0, k, 0] = jnp.where(same_head, eye_h * s["dec"][c * C:c * C + 1] + ab[:, :LANE], 0.0).astype(A_o.dtype)
            B_o[0, k, 0] = jnp.where(same_head, ab[:, LANE:], 0.0)


def _scanprep(r, v, kk, kd, lw, bb):
    N, RW = r.shape
    hp = RW // LANE
    G = _tile(N, 2 * CHUNK)
    ngrp = 4 if N % (4 * G) == 0 else 1
    rb = G * ngrp
    nchunks = N // CHUNK
    shared = lambda: pl.BlockSpec((rb, LANE), lambda d, i, h: (i, h))
    perdir = lambda: pl.BlockSpec((rb, LANE), lambda d, i, h: (i, d * hp + h))
    rowo = lambda: pl.BlockSpec((1, rb, LANE), lambda d, i, h: (d, i, h))
    mato = lambda: pl.BlockSpec((1, rb // CHUNK, 1, LANE, LANE), lambda d, i, h: (d, i, h, 0, 0))
    return pl.pallas_call(
        functools.partial(_scanprep_kernel, G=G, ngrp=ngrp),
        grid=(2, N // rb, hp),
        in_specs=[shared(), shared(), shared(), perdir(), perdir(), perdir()],
        out_specs=[mato(), mato(), rowo(), rowo()],
        out_shape=[jax.ShapeDtypeStruct((2, nchunks, hp, LANE, LANE), BF16),
                   jax.ShapeDtypeStruct((2, nchunks, hp, LANE, LANE), F32),
                   jax.ShapeDtypeStruct((2, N, RW), BF16),
                   jax.ShapeDtypeStruct((2, N, RW), F32)],
        compiler_params=_params("parallel", "parallel", "parallel"),
        name="rwkv_chunk_maps",
    )(r, v, kk, kd, lw, bb)


def _scanserial_kernel(A_ref, B_ref, Q_ref, Y0_ref, y_ref, s_ref, *, npair, cb):
    @pl.when(pl.program_id(2) == 0)
    def _():
        s_ref[...] = jnp.zeros_like(s_ref)

    C = CHUNK
    d = pl.program_id(0)
    states = [s_ref[h] for h in range(npair)]
    for k in range(cb):
        c = jnp.where(d == 0, k, cb - 1 - k)
        rows = pl.ds(pl.multiple_of(c * C, C), C)
        for h in range(npair):
            ls = slice(h * LANE, (h + 1) * LANE)
            aq = jnp.concatenate([A_ref[0, c, h], Q_ref[0, rows, ls]], axis=0)
            res = jnp.dot(aq, states[h].astype(BF16), preferred_element_type=F32)
            y_ref[0, rows, ls] = res[LANE:] + Y0_ref[0, rows, ls]
            states[h] = res[:LANE] + B_ref[0, c, h]
    for h in range(npair):
        s_ref[h] = states[h]


def _scanserial(A, Bm, Q, Y0, *, B, S, Tc):
    _, nchunks, npair, _, _ = A.shape
    C = CHUNK
    N = nchunks * C
    RW = npair * LANE
    cb = math.gcd(math.gcd(S // C, Tc // C), 4)
    nl, nc = S // (C * cb), Tc // (C * cb)
    cbase = B * nl

    def chunk_of(d, b, p):
        kc = jnp.where(d == 0, p, nc - 1 - p)
        kl = jnp.where(d == 0, p - nc, nl - 1 - (p - nc))
        return jnp.where(p < nc, cbase + b * nc + kc, b * nl + kl)

    mat = lambda: pl.BlockSpec((1, cb, npair, LANE, LANE), lambda d, b, p: (d, chunk_of(d, b, p), 0, 0, 0))
    row = lambda: pl.BlockSpec((1, cb * C, RW), lambda d, b, p: (d, chunk_of(d, b, p), 0))
    return pl.pallas_call(
        functools.partial(_scanserial_kernel, npair=npair, cb=cb),
        grid=(2, B, nl + nc),
        in_specs=[mat(), mat(), row(), row()],
        out_specs=row(),
        out_shape=jax.ShapeDtypeStruct((2, N, RW), F32),
        scratch_shapes=[pltpu.VMEM((npair, LANE, LANE), F32)],
        compiler_params=_params("parallel", "parallel", "arbitrary"),
        name="rwkv_chunk_chain",
    )(A, Bm, Q, Y0)


def _rwout_kernel(y_ref, bon_ref, g_ref, lg_ref, lb_ref, seg_ref, segt_ref, o_ref):
    y = y_ref[0] + y_ref[1]
    inv = 1.0 / RWKV_HEAD_DIM
    mu = _segsum(y, seg_ref, segt_ref) * inv
    yc = y - mu
    var = _segsum(yc * yc, seg_ref, segt_ref) * inv
    yn = yc * lax.rsqrt(var + GN_EPS) * lg_ref[...] + lb_ref[...]
    o_ref[...] = ((yn + bon_ref[...]) * g_ref[...]).astype(o_ref.dtype)


def _rwout(y2, bonus, g, p, rows):
    RW = g.shape[1]
    tm = _tile(rows, 256)
    seg = (jnp.arange(RW)[:, None] // RWKV_HEAD_DIM == jnp.arange(LANE)[None, :]).astype(BF16)
    full = lambda shape: pl.BlockSpec(shape, lambda i: (0,) * len(shape))
    return pl.pallas_call(
        _rwout_kernel,
        grid=(rows // tm,),
        in_specs=[pl.BlockSpec((2, tm, RW), lambda i: (0, i, 0)),
                  pl.BlockSpec((tm, RW), lambda i: (i, 0)),
                  pl.BlockSpec((tm, RW), lambda i: (i, 0)),
                  full((1, RW)), full((1, RW)), full((RW, LANE)), full((LANE, RW))],
        out_specs=pl.BlockSpec((tm, RW), lambda i: (i, 0)),
        out_shape=jax.ShapeDtypeStruct((rows, RW), BF16),
        compiler_params=_params("parallel"),
        name="rwkv_out",
    )(y2, bonus, g, p["rwkv_ln_g"].reshape(1, RW), p["rwkv_ln_b"].reshape(1, RW), seg, seg.T)


CONV_ROWS = 64


def _conv_kernel(first_ref, last_ref, z_ref, zp_ref, zn_ref, w_ref, b_ref, lg_ref, lb_ref, o_ref, u_ref, c_ref, *, CW, K):
    i = pl.program_id(0)
    has_prev = (1 - first_ref[i]).astype(F32)
    has_next = (1 - last_ref[i]).astype(F32)
    tt = z_ref.shape[0]
    H = CONV_HALO
    nlt = CW // LANE
    glu = lambda ref, lt: ref[:, lt * LANE:(lt + 1) * LANE] * _sigmoid(ref[:, CW + lt * LANE:CW + (lt + 1) * LANE])
    span = tt + SUBLANE * ((H + K // 2) // SUBLANE)
    for lt in range(nlt):
        u_ref[0, lt, 0:H, :] = glu(zp_ref, lt) * has_prev
        u_ref[0, lt, H:H + tt, :] = glu(z_ref, lt)
        u_ref[0, lt, H + tt:, :] = glu(zn_ref, lt) * has_next
        for s in range(1, SUBLANE):
            u_ref[s, lt, 0:span, :] = u_ref[0, lt, s:s + span, :]
    off = H - K // 2
    rb = min(CONV_ROWS, tt)
    nrb = tt // rb

    def block(idx, carry):
        lt = idx // nrb
        r0 = pl.multiple_of((idx % nrb) * rb, rb)
        acc = jnp.zeros((rb, LANE), F32) + b_ref[lt]
        for k in range(K):
            s, j = (off + k) % SUBLANE, (off + k) // SUBLANE
            acc = acc + w_ref[k, lt] * u_ref[s, lt, pl.ds(r0 + SUBLANE * j, rb), :]
        c_ref[lt, pl.ds(r0, rb), :] = acc
        return carry

    lax.fori_loop(0, nlt * nrb, block, 0)
    y = _ln(jnp.concatenate([c_ref[lt] for lt in range(nlt)], axis=1)) * lg_ref[...] + lb_ref[...]
    o_ref[...] = (y * _sigmoid(y)).astype(o_ref.dtype)


def _conv(cv, p, *, B, S, Tc, rows):
    CW = cv.shape[1] // 2
    K = p["conv_w"].shape[0]
    assert K // 2 <= CONV_HALO
    nlt = CW // LANE
    tt = _tile(math.gcd(S, Tc), 256)
    hb = tt // CONV_HALO
    nhb = cv.shape[0] // CONV_HALO
    first, last = _seq_edges(B, S, Tc, tt)
    full = lambda shape: pl.BlockSpec(shape, lambda i, f, l: (0,) * len(shape))
    grid_spec = pltpu.PrefetchScalarGridSpec(
        num_scalar_prefetch=2,
        grid=(rows // tt,),
        in_specs=[pl.BlockSpec((tt, 2 * CW), lambda i, f, l: (i, 0)),
                  pl.BlockSpec((CONV_HALO, 2 * CW), lambda i, f, l: (jnp.maximum(i * hb - 1, 0), 0)),
                  pl.BlockSpec((CONV_HALO, 2 * CW), lambda i, f, l: (jnp.minimum((i + 1) * hb, nhb - 1), 0)),
                  full((K, nlt, 1, LANE)), full((nlt, 1, LANE)), full((1, CW)), full((1, CW))],
        out_specs=pl.BlockSpec((tt, CW), lambda i, f, l: (i, 0)),
        scratch_shapes=[pltpu.VMEM((SUBLANE, nlt, tt + 2 * CONV_HALO, LANE), F32),
                        pltpu.VMEM((nlt, tt, LANE), F32)],
    )
    return pl.pallas_call(
        functools.partial(_conv_kernel, CW=CW, K=K),
        grid_spec=grid_spec,
        out_shape=jax.ShapeDtypeStruct((rows, CW), BF16),
        compiler_params=_params("parallel"),
        name="conv_module",
    )(first, last, cv, cv, cv, p["conv_w"].reshape(K, nlt, 1, LANE), p["conv_b"].reshape(nlt, 1, LANE),
      p["conv_ln_g"].reshape(1, CW), p["conv_ln_b"].reshape(1, CW))


def _merge_kernel(a_ref, r_ref, c_ref, wa_ref, wr_ref, wc_ref, g0_ref, g1_ref, g2_ref, b0_ref, b1_ref, b2_ref, o_ref):
    m = _sigmoid(g0_ref[...] + b0_ref[...]) * jnp.dot(a_ref[...], wa_ref[...], preferred_element_type=F32)
    m = m + _sigmoid(g1_ref[...] + b1_ref[...]) * jnp.dot(r_ref[...], wr_ref[...], preferred_element_type=F32)
    m = m + _sigmoid(g2_ref[...] + b2_ref[...]) * jnp.dot(c_ref[...], wc_ref[...], preferred_element_type=F32)
    o_ref[...] = m.astype(o_ref.dtype)


def _merge(att, rwo, cvo, gt, wa, wr, wc, b_gate, rows):
    D = wa.shape[1]
    tm = _tile(rows, 512)
    tn = _tile(D, 512)
    nj = D // tn
    xin = lambda w: pl.BlockSpec((tm, w), lambda j, i: (i, 0))
    win = lambda w: pl.BlockSpec((w, tn), lambda j, i: (0, j))
    gate = lambda k: pl.BlockSpec((tm, tn), lambda j, i: (i, k * nj + j))
    bias = lambda k: pl.BlockSpec((1, tn), lambda j, i: (0, k * nj + j))
    bg = b_gate.reshape(1, 3 * D)
    return pl.pallas_call(
        _merge_kernel,
        grid=(nj, rows // tm),
        in_specs=[xin(att.shape[1]), xin(rwo.shape[1]), xin(cvo.shape[1]),
                  win(wa.shape[0]), win(wr.shape[0]), win(wc.shape[0]),
                  gate(0), gate(1), gate(2), bias(0), bias(1), bias(2)],
        out_specs=pl.BlockSpec((tm, tn), lambda j, i: (i, j)),
        out_shape=jax.ShapeDtypeStruct((rows, D), BF16),
        compiler_params=_params("parallel", "parallel"),
        name="branch_merge",
    )(att, rwo, cvo, wa, wr, wc, gt, gt, gt, bg, bg, bg)


def _outproj_kernel(m_ref, w_ref, x_ref, gm_ref, sh_ref, sc_ref, lg_ref, lb_ref, wrt_ref, brt_ref,
                    x_o, h_o, lgt_o, *, alpha):
    y = jnp.dot(m_ref[...], w_ref[...], preferred_element_type=F32)
    x1 = _ln(alpha * x_ref[...] + gm_ref[0] * y) * lg_ref[...] + lb_ref[...]
    x_o[...] = x1
    h2 = _ln(x1) * (1.0 + sc_ref[0]) + sh_ref[0]
    h_o[...] = h2.astype(h_o.dtype)
    hi = h2.astype(BF16)
    lo = (h2 - hi.astype(F32)).astype(BF16)
    r1 = jnp.dot(hi, wrt_ref[...], preferred_element_type=F32)
    r2 = jnp.dot(lo, wrt_ref[:, :LANE], preferred_element_type=F32)
    lgt_o[...] = r1[:, :LANE] + r1[:, LANE:] + r2 + brt_ref[...]


def _outproj(m, w_out, xa, mods3, ln_g, ln_b, w_rt, b_rt, rows, modrow, tm, alpha):
    D = w_out.shape[0]
    full = lambda shape: pl.BlockSpec(shape, lambda i: (0,) * len(shape))
    rowb = lambda w: pl.BlockSpec((tm, w), lambda i: (i, 0))
    mod = lambda blk: pl.BlockSpec((1, 1, D), lambda i: (modrow(i), 0, blk))
    return pl.pallas_call(
        functools.partial(_outproj_kernel, alpha=alpha),
        grid=(rows // tm,),
        in_specs=[rowb(D), full((D, D)), rowb(D), mod(2), mod(3), mod(4), full((1, D)), full((1, D)),
                  full((D, 2 * LANE)), full((1, LANE))],
        out_specs=[rowb(D), rowb(D), rowb(LANE)],
        out_shape=[jax.ShapeDtypeStruct((rows, D), F32), jax.ShapeDtypeStruct((rows, D), BF16),
                   jax.ShapeDtypeStruct((rows, LANE), F32)],
        compiler_params=_params("parallel"),
        name="out_proj",
    )(m, w_out, xa, mods3, mods3, mods3, ln_g.reshape(1, D), ln_b.reshape(1, D), w_rt, b_rt)


def _route_kernel(l_ref, tri_ref, id_o, w_o, cnt_o, cnt_ref):
    lg = l_ref[...]
    lane = lax.broadcasted_iota(jnp.int32, lg.shape, 1)
    neg = -jnp.inf
    big = jnp.int32(2 ** 30)
    isg = jnp.logical_and(lane >= N_EXPERTS, lane < N_EXPERTS + N_GROUPS)
    gl = jnp.where(isg, lg, neg)
    gmax = jnp.max(gl, -1, keepdims=True)
    grp = jnp.min(jnp.where(gl == gmax, lane - N_EXPERTS, big), -1, keepdims=True)
    grp_w = 1.0 / jnp.sum(jnp.exp(gl - gmax), -1, keepdims=True)
    ing = jnp.logical_and(lane < N_EXPERTS, jnp.right_shift(lane, int(math.log2(EXPERTS_PER_GROUP))) == grp)
    el = jnp.where(ing, lg, neg)
    e1 = jnp.max(el, -1, keepdims=True)
    i1 = jnp.min(jnp.where(el == e1, lane, big), -1, keepdims=True)
    psum = jnp.sum(jnp.exp(el - e1), -1, keepdims=True)
    el2 = jnp.where(lane == i1, neg, el)
    e2 = jnp.max(el2, -1, keepdims=True)
    i2 = jnp.min(jnp.where(el2 == e2, lane, big), -1, keepdims=True)
    p1 = 1.0 / psum
    p2 = jnp.exp(e2 - e1) / psum
    tot = p1 + p2
    w_o[...] = jnp.where(lane == 0, grp_w * p1 / tot, jnp.where(lane == 1, grp_w * p2 / tot, 0.0))

    @pl.when(pl.program_id(0) == 0)
    def _():
        cnt_ref[...] = jnp.zeros_like(cnt_ref)

    hit = jnp.logical_or(lane == i1, lane == i2)
    before = cnt_ref[...] + jnp.dot(tri_ref[...], jnp.where(hit, 1.0, 0.0).astype(BF16), preferred_element_type=F32)
    r1 = jnp.sum(jnp.where(lane == i1, before, 0.0), -1, keepdims=True).astype(jnp.int32)
    r2 = jnp.sum(jnp.where(lane == i2, before, 0.0), -1, keepdims=True).astype(jnp.int32)
    id_o[...] = jnp.where(lane == 0, i1, jnp.where(lane == 1, i2, jnp.where(lane == 2, r1, jnp.where(lane == 3, r2, 0))))
    cnt_ref[...] = cnt_ref[...] + jnp.sum(jnp.where(hit, 1.0, 0.0), axis=0, keepdims=True)
    cnt_o[...] = cnt_ref[...]


def _route(logits):
    n = logits.shape[0]
    tm = _tile(n, 1024)
    tri = (jnp.arange(tm)[None, :] < jnp.arange(tm)[:, None]).astype(BF16)
    blk = lambda: pl.BlockSpec((tm, LANE), lambda i: (i, 0))
    return pl.pallas_call(
        _route_kernel,
        grid=(n // tm,),
        in_specs=[blk(), pl.BlockSpec((tm, tm), lambda i: (0, 0))],
        out_specs=[blk(), blk(), pl.BlockSpec((1, LANE), lambda i: (0, 0))],
        out_shape=[jax.ShapeDtypeStruct((n, LANE), jnp.int32), jax.ShapeDtypeStruct((n, LANE), F32),
                   jax.ShapeDtypeStruct((1, LANE), F32)],
        scratch_shapes=[pltpu.VMEM((1, LANE), F32)],
        compiler_params=_params("arbitrary"),
        name="route",
    )(logits, tri)


def _moe_kernel(be_ref, act_ref, x_ref, wg_ref, wu_ref, wd_ref, o_ref):
    i = pl.program_id(0)

    @pl.when(act_ref[i] > 0)
    def _():
        x = x_ref[...]
        gte = jnp.dot(x, wg_ref[0], preferred_element_type=F32)
        up = jnp.dot(x, wu_ref[0], preferred_element_type=F32)
        h = (gte * _sigmoid(gte) * up).astype(BF16)
        o_ref[...] = jnp.dot(h, wd_ref[0], preferred_element_type=F32)

    @pl.when(act_ref[i] == 0)
    def _():
        o_ref[...] = jnp.zeros_like(o_ref)


def _moe_ffn(xg, block_e, active, wg, wu, wd):
    R, D = xg.shape
    FF = wg.shape[2]
    nb = R // MOE_BLOCK
    grid_spec = pltpu.PrefetchScalarGridSpec(
        num_scalar_prefetch=2,
        grid=(nb,),
        in_specs=[pl.BlockSpec((MOE_BLOCK, D), lambda i, be, ac: (i, 0)),
                  pl.BlockSpec((1, D, FF), lambda i, be, ac: (be[i], 0, 0)),
                  pl.BlockSpec((1, D, FF), lambda i, be, ac: (be[i], 0, 0)),
                  pl.BlockSpec((1, FF, D), lambda i, be, ac: (be[i], 0, 0))],
        out_specs=pl.BlockSpec((MOE_BLOCK, D), lambda i, be, ac: (i, 0)),
    )
    return pl.pallas_call(
        _moe_kernel,
        grid_spec=grid_spec,
        out_shape=jax.ShapeDtypeStruct((R, D), F32),
        compiler_params=_params("arbitrary"),
        name="moe_ffn",
    )(block_e, active, xg, wg, wu, wd)


def _dispatch(ids, rank, counts):
    n = ids.shape[0]
    A = 2 * n
    padded = (counts + MOE_BLOCK - 1) // MOE_BLOCK * MOE_BLOCK
    pad_end = jnp.cumsum(padded)
    pad_start = pad_end - padded
    start = jnp.cumsum(counts) - counts
    dest = jnp.take(pad_start, ids, mode="clip") + rank
    nb = -(-A // MOE_BLOCK) + N_EXPERTS
    blk_start = jnp.arange(nb, dtype=jnp.int32) * MOE_BLOCK
    block_e = jnp.minimum(jnp.sum(pad_end[None, :] <= blk_start[:, None], axis=1), N_EXPERTS - 1).astype(jnp.int32)
    active = (blk_start < pad_end[-1]).astype(jnp.int32)
    order = jnp.argsort(ids.reshape(-1))
    slot_e = jnp.repeat(block_e, MOE_BLOCK)
    slot_r = jnp.arange(nb * MOE_BLOCK, dtype=jnp.int32) - jnp.take(pad_start, slot_e)
    src = jnp.clip(jnp.take(start, slot_e) + slot_r, 0, A - 1)
    slot_tok = jnp.where(slot_r < jnp.take(counts, slot_e), jnp.take(order, src) // 2, 0).astype(jnp.int32)
    return slot_tok, dest.astype(jnp.int32), block_e, active


def _final_kernel(x_ref, y0_ref, y1_ref, w_ref, gm_ref, lg_ref, lb_ref, o_ref, *, alpha):
    w = w_ref[...]
    f = w[:, 0:1] * y0_ref[...] + w[:, 1:2] * y1_ref[...]
    o_ref[...] = _ln(alpha * x_ref[...] + gm_ref[0] * f) * lg_ref[...] + lb_ref[...]


def _final(x1, y0, y1, wts, mods3, ln_g, ln_b, rows, modrow, tm, alpha):
    D = x1.shape[1]
    rowb = lambda w: pl.BlockSpec((tm, w), lambda i: (i, 0))
    full = lambda shape: pl.BlockSpec(shape, lambda i: (0,) * len(shape))
    return pl.pallas_call(
        functools.partial(_final_kernel, alpha=alpha),
        grid=(rows // tm,),
        in_specs=[rowb(D), rowb(D), rowb(D), rowb(LANE),
                  pl.BlockSpec((1, 1, D), lambda i: (modrow(i), 0, 5)), full((1, D)), full((1, D))],
        out_specs=rowb(D),
        out_shape=jax.ShapeDtypeStruct((rows, D), F32),
        compiler_params=_params("parallel"),
        name="ffn_residual",
    )(x1, y0, y1, wts, mods3, ln_g.reshape(1, D), ln_b.reshape(1, D))


def _rope_tables(S):
    rows = S // GRID_W
    row = jnp.repeat(jnp.arange(rows, dtype=F32), GRID_W)
    col = jnp.tile(jnp.arange(GRID_W, dtype=F32), rows)
    nf = ATT_HEAD_DIM // 4
    inv = ROPE_THETA ** (-jnp.arange(nf, dtype=F32) / nf)
    ar, ac = row[:, None] * inv, col[:, None] * inv
    cos = jnp.concatenate([jnp.cos(ar), jnp.cos(ar), jnp.cos(ac), jnp.cos(ac)], -1)
    sin = jnp.concatenate([-jnp.sin(ar), jnp.sin(ar), -jnp.sin(ac), jnp.sin(ac)], -1)
    return cos, sin


def kernel(x, c, ctx, c_ctx, w_mod, b_mod, w_in, b_gate, q_norm, k_norm, w_att_o, rwkv_mu, rwkv_w0, rwkv_w2, rwkv_a0, rwkv_a2, rwkv_g2, rwkv_k_k, rwkv_k_a, rwkv_r_k, rwkv_ln_g, rwkv_ln_b, w_rwkv_o, conv_w, conv_b, conv_ln_g, conv_ln_b, w_conv_o, w_out, ln1_g, ln1_b, w_group, b_group, w_router, b_router, w_e_gate, w_e_up, w_e_down, ln2_g, ln2_b):
    B, S, D = x.shape
    Tc = ctx.shape[1]
    depth = w_mod.shape[0]
    NL, NC = B * S, B * Tc
    N = NL + NC
    alpha = (2 * depth) ** 0.25
    att_w = w_att_o.shape[1]
    heads = att_w // ATT_HEAD_DIM
    rw_in = rwkv_mu.shape[-1]
    conv_cw = w_conv_o.shape[1]
    kv_w = (w_in.shape[-1] - att_w - rw_in - 2 * conv_cw - 3 * D) // 2
    assert kv_w == heads // ATT_GROUP * ATT_HEAD_DIM
    o_rw = att_w + 2 * kv_w
    o_cv = o_rw + rw_in
    o_gt = o_cv + 2 * conv_cw

    tmod = _tile(math.gcd(S, Tc), 256)
    tps = S // tmod
    nlt = NL // tmod
    modrow = lambda i: jnp.where(i < nlt, i // tps, B)
    cos_t, sin_t = _rope_tables(S)

    R = -(-(B + 1) // SUBLANE) * SUBLANE
    cv = jnp.zeros((R, D), F32).at[:B].set(c).at[B].set(c_ctx)
    xa = jnp.concatenate([x.reshape(NL, D), ctx.reshape(NC, D)], 0)

    for l in range(depth):
        last = l == depth - 1
        rows = NL if last else N
        p = dict(rwkv_mu=rwkv_mu[l], rwkv_w0=rwkv_w0[l], rwkv_w2=rwkv_w2[l], rwkv_a0=rwkv_a0[l], rwkv_a2=rwkv_a2[l],
                 rwkv_g2=rwkv_g2[l], rwkv_k_k=rwkv_k_k[l], rwkv_k_a=rwkv_k_a[l], rwkv_r_k=rwkv_r_k[l],
                 rwkv_ln_g=rwkv_ln_g[l], rwkv_ln_b=rwkv_ln_b[l], conv_w=conv_w[l], conv_b=conv_b[l],
                 conv_ln_g=conv_ln_g[l], conv_ln_b=conv_ln_b[l])
        mods3 = _adaln(cv, w_mod[l], b_mod[l]).reshape(R, 1, 6 * D)
        wl = w_in[l]
        h = _lnmod(xa, mods3, 0, 1, N, modrow, tmod)
        qkv = _matmul(h, wl[:, :o_rw].astype(BF16), 512)
        rw = _matmul(h, wl[:, o_rw:o_cv].astype(BF16), 1152)
        cvx = _matmul(h, wl[:, o_cv:o_gt].astype(BF16), 1024, rows)
        gt = _matmul(h, wl[:, o_gt:].astype(BF16), 1024, rows)

        qn, kn = q_norm[l].reshape(1, LANE), k_norm[l].reshape(1, LANE)
        att = _attention(qkv, qn, kn, cos_t, sin_t, B=B, S=S, Tc=Tc, heads=heads, latent_queries=True)
        if not last:
            att_c = _attention(qkv, qn, kn, cos_t, sin_t, B=B, S=S, Tc=Tc, heads=heads, latent_queries=False)
            att = jnp.concatenate([att, att_c], 0)

        r, v, kk, kd, lw, bb, g, bonus = _rwprep(rw, p, B=B, S=S, Tc=Tc)
        Am, Bm, Qm, Y0 = _scanprep(r, v, kk, kd, lw, bb)
        y2 = _scanserial(Am, Bm, Qm, Y0, B=B, S=S, Tc=Tc)
        rwo = _rwout(y2, bonus, g, p, rows)
        cvo = _conv(cvx, p, B=B, S=S, Tc=Tc, rows=rows)

        m = _merge(att, rwo, cvo, gt, w_att_o[l].astype(BF16), w_rwkv_o[l].astype(BF16), w_conv_o[l].astype(BF16),
                   b_gate[l], rows)
        w_rt = jnp.zeros((D, LANE), F32).at[:, :N_EXPERTS].set(w_router[l]).at[:, N_EXPERTS:N_EXPERTS + N_GROUPS].set(w_group[l])
        b_rt = jnp.zeros((1, LANE), F32).at[0, :N_EXPERTS].set(b_router[l]).at[0, N_EXPERTS:N_EXPERTS + N_GROUPS].set(b_group[l])
        w_rt_hi = w_rt.astype(BF16)
        w_rt = jnp.concatenate([w_rt_hi, (w_rt - w_rt_hi.astype(F32)).astype(BF16)], 1)
        x1, h2, logits = _outproj(m, w_out[l].astype(BF16), xa, mods3, ln1_g[l], ln1_b[l], w_rt, b_rt, rows, modrow,
                                  tmod, alpha)

        ids128, wts128, cnt128 = _route(logits)
        slot_tok, dest, block_e, active = _dispatch(ids128[:, :2], ids128[:, 2:4], cnt128[0, :N_EXPERTS].astype(jnp.int32))
        xg = jnp.take(h2, slot_tok, axis=0, mode="clip")
        yb = _moe_ffn(xg, block_e, active, w_e_gate[l].astype(BF16), w_e_up[l].astype(BF16), w_e_down[l].astype(BF16))
        y0 = jnp.take(yb, dest[:, 0], axis=0, mode="clip")
        y1 = jnp.take(yb, dest[:, 1], axis=0, mode="clip")
        xa = _final(x1, y0, y1, wts128, mods3, ln2_g[l], ln2_b[l], rows, modrow, tmod, alpha)

    return xa[:NL].reshape(B, S, D)
```

```python
import functools
import math

import numpy as np
import jax
import jax.numpy as jnp
from jax import lax
from jax.experimental import pallas as pl
from jax.experimental.pallas import tpu as pltpu

F32 = jnp.float32
BF16 = jnp.bfloat16
HI = lax.Precision.HIGHEST

LANE = 128
SUBLANE = 8
ATT_HEAD_DIM = 128
ATT_GROUP = 4
ATT_HEADS_PER_STEP = 2
ROPE_THETA = 10000.0
GRID_W = 64
RWKV_HEAD_DIM = 64
CHUNK = 64
CONV_HALO = 16
SHIFT_HALO = 8
N_GROUPS = 4
EXPERTS_PER_GROUP = 8
N_EXPERTS = N_GROUPS * EXPERTS_PER_GROUP
MOE_BLOCK = 256
GN_EPS = 64e-5
LN_EPS = 1e-6
RMS_EPS = 1e-6
VMEM_LIMIT = 48 * 1024 * 1024

NT_DIMS = (((1,), (1,)), ((), ()))
TN_DIMS = (((0,), (0,)), ((), ()))


def _params(*sem):
    return pltpu.CompilerParams(dimension_semantics=sem, vmem_limit_bytes=VMEM_LIMIT)


def _sigmoid(x):
    return 1.0 / (1.0 + jnp.exp(-x))


def _ln(x):
    mu = jnp.mean(x, -1, keepdims=True)
    xc = x - mu
    var = jnp.mean(xc * xc, -1, keepdims=True)
    return xc * lax.rsqrt(var + LN_EPS)


def _tile(n, pref):
    t = min(n, pref)
    while n % t:
        t -= SUBLANE
    return t


def _adaln_kernel(c_ref, w_ref, b_ref, o_ref):
    c = c_ref[...]
    s = c * _sigmoid(c)
    o_ref[...] = jnp.dot(s, w_ref[0], precision=HI, preferred_element_type=F32) + b_ref[0]


def _adaln(cv, w_mod, b_mod, layer):
    R, D = cv.shape
    L, _, W = w_mod.shape
    tn = _tile(W, 1024)
    return pl.pallas_call(
        _adaln_kernel,
        grid=(W // tn,),
        in_specs=[pl.BlockSpec((R, D), lambda j: (0, 0)),
                  pl.BlockSpec((1, D, tn), lambda j: (layer, 0, j)),
                  pl.BlockSpec((1, 1, tn), lambda j: (layer, 0, j))],
        out_specs=pl.BlockSpec((R, tn), lambda j: (0, j)),
        out_shape=jax.ShapeDtypeStruct((R, W), F32),
        compiler_params=_params("parallel"),
        name="adaln",
    )(cv, w_mod, b_mod.reshape(L, 1, W))


def _lnmod_kernel(x_ref, sh_ref, sc_ref, o_ref):
    o_ref[...] = (_ln(x_ref[...]) * (1.0 + sc_ref[0]) + sh_ref[0]).astype(o_ref.dtype)


def _lnmod(xa, mods3, shift_blk, scale_blk, rows, modrow, tm):
    D = xa.shape[1]
    return pl.pallas_call(
        _lnmod_kernel,
        grid=(rows // tm,),
        in_specs=[pl.BlockSpec((tm, D), lambda i: (i, 0)),
                  pl.BlockSpec((1, 1, D), lambda i: (modrow(i), 0, shift_blk)),
                  pl.BlockSpec((1, 1, D), lambda i: (modrow(i), 0, scale_blk))],
        out_specs=pl.BlockSpec((tm, D), lambda i: (i, 0)),
        out_shape=jax.ShapeDtypeStruct((rows, D), BF16),
        compiler_params=_params("parallel"),
        name="lnmod",
    )(xa, mods3, mods3)


def _mm_kernel(x_ref, w_ref, o_ref):
    o_ref[...] = jnp.dot(x_ref[...], w_ref[...], preferred_element_type=F32).astype(o_ref.dtype)


def _matmul(x, w, tn_pref, rows=None, out_dtype=F32, tm_pref=1024):
    M, K = x.shape
    M = rows or M
    N = w.shape[1]
    tm = _tile(M, tm_pref)
    tn = _tile(N, tn_pref)
    return pl.pallas_call(
        _mm_kernel,
        grid=(N // tn, M // tm),
        in_specs=[pl.BlockSpec((tm, K), lambda j, i: (i, 0)),
                  pl.BlockSpec((K, tn), lambda j, i: (0, j))],
        out_specs=pl.BlockSpec((tm, tn), lambda j, i: (i, j)),
        out_shape=jax.ShapeDtypeStruct((M, N), out_dtype),
        compiler_params=_params("parallel", "parallel"),
        name="matmul",
    )(x, w)


def _rms(t, g):
    return t * lax.rsqrt(jnp.mean(t * t, -1, keepdims=True) + RMS_EPS) * g


def _rope(t, cos, sin_signed):
    lane = lax.broadcasted_iota(jnp.int32, t.shape, 1)
    first = (lane & 63) < 32
    partner = jnp.where(first, pltpu.roll(t, LANE - 32, 1), pltpu.roll(t, 32, 1))
    return t * cos + partner * sin_signed


def _attn_kernel(*refs, rope, has_lat, tq, scale, hg):
    if has_lat:
        (q_ref, kc_ref, vc_ref, qn_ref, kn_ref, kl_ref, vl_ref, cq_ref, sq_ref, ck_ref, sk_ref,
         o_ref, kcb, vcb, klb, vlb) = refs
    else:
        q_ref, kc_ref, vc_ref, qn_ref, kn_ref, o_ref, kcb, vcb = refs

    @pl.when(pl.program_id(2) == 0)
    def _():
        kcb[...] = _rms(kc_ref[...], kn_ref[...]).astype(BF16)
        vcb[:, :LANE] = vc_ref[...].astype(BF16)
        vcb[:, LANE:] = jnp.ones(vc_ref.shape, BF16)
        if has_lat:
            klb[...] = _rope(_rms(kl_ref[...], kn_ref[...]), ck_ref[...], sk_ref[...]).astype(BF16)
            vlb[:, :LANE] = vl_ref[...].astype(BF16)
            vlb[:, LANE:] = jnp.ones(vl_ref.shape, BF16)

    groups = [range(g0, g0 + hg) for g0 in range(0, ATT_GROUP, hg)]
    scores = {}

    def qk(gi):
        qs = []
        for g in groups[gi]:
            qg = _rms(q_ref[:, g * LANE:(g + 1) * LANE], qn_ref[...])
            if rope:
                qg = _rope(qg, cq_ref[...], sq_ref[...])
            qs.append((qg * (scale * math.log2(math.e))).astype(BF16))
        qa = jnp.concatenate(qs, axis=0)
        sc = lax.dot_general(qa, kcb[...], NT_DIMS, preferred_element_type=F32)
        sl = lax.dot_general(qa, klb[...], NT_DIMS, preferred_element_type=F32) if has_lat else None
        scores[gi] = (sc, sl)

    def softmax_pv(gi):
        sc, sl = scores.pop(gi)
        m = jnp.max(sc, -1, keepdims=True)
        if has_lat:
            m = jnp.maximum(m, jnp.max(sl, -1, keepdims=True))
        od = jnp.dot(jnp.exp2(sc - m).astype(BF16), vcb[...], preferred_element_type=F32)
        if has_lat:
            od = od + jnp.dot(jnp.exp2(sl - m).astype(BF16), vlb[...], preferred_element_type=F32)
        o = od[:, :LANE] / od[:, LANE:]
        for n, g in enumerate(groups[gi]):
            o_ref[:, g * LANE:(g + 1) * LANE] = o[n * tq:(n + 1) * tq].astype(o_ref.dtype)

    qk(0)
    for gi in range(1, len(groups)):
        qk(gi)
        softmax_pv(gi - 1)
    softmax_pv(len(groups) - 1)


def _attention(qkv, qn, kn, cos_t, sin_t, *, B, S, Tc, heads, latent_queries):
    NL = B * S
    kvh = heads // ATT_GROUP
    GW = ATT_GROUP * LANE
    scale = ATT_HEAD_DIM ** -0.5
    Tq = S if latent_queries else Tc
    tq = _tile(Tq, 128)
    nq = Tq // tq
    qbase = 0 if latent_queries else NL // tq
    cbase = NL // Tc
    common = [pl.BlockSpec((tq, GW), lambda b, j, i: (qbase + b * nq + i, j)),
              pl.BlockSpec((Tc, LANE), lambda b, j, i: (cbase + b, heads + j)),
              pl.BlockSpec((Tc, LANE), lambda b, j, i: (cbase + b, heads + kvh + j)),
              pl.BlockSpec((1, LANE), lambda b, j, i: (0, 0)),
              pl.BlockSpec((1, LANE), lambda b, j, i: (0, 0))]
    args = [qkv, qkv, qkv, qn, kn]
    scratch = [pltpu.VMEM((Tc, LANE), BF16), pltpu.VMEM((Tc, 2 * LANE), BF16)]
    if latent_queries:
        common += [pl.BlockSpec((S, LANE), lambda b, j, i: (b, heads + j)),
                   pl.BlockSpec((S, LANE), lambda b, j, i: (b, heads + kvh + j)),
                   pl.BlockSpec((tq, LANE), lambda b, j, i: (i, 0)),
                   pl.BlockSpec((tq, LANE), lambda b, j, i: (i, 0)),
                   pl.BlockSpec((S, LANE), lambda b, j, i: (0, 0)),
                   pl.BlockSpec((S, LANE), lambda b, j, i: (0, 0))]
        args += [qkv, qkv, cos_t, sin_t, cos_t, sin_t]
        scratch += [pltpu.VMEM((S, LANE), BF16), pltpu.VMEM((S, 2 * LANE), BF16)]
    return pl.pallas_call(
        functools.partial(_attn_kernel, rope=latent_queries, has_lat=latent_queries, tq=tq, scale=scale, hg=ATT_HEADS_PER_STEP),
        grid=(B, kvh, nq),
        in_specs=common,
        out_specs=pl.BlockSpec((tq, GW), lambda b, j, i: (b * nq + i, j)),
        out_shape=jax.ShapeDtypeStruct((B * Tq, heads * LANE), BF16),
        scratch_shapes=scratch,
        compiler_params=_params("parallel", "parallel", "arbitrary"),
        name="attention_lat" if latent_queries else "attention_ctx",
    )(*args)


def _segsum(x, seg_ref, segt_ref):
    return _split_dot(_split_dot(x, seg_ref[...]), segt_ref[...])


def _split_dot(x, w):
    hi = x.astype(BF16)
    lo = (x - hi.astype(F32)).astype(BF16)
    return jnp.dot(hi, w, preferred_element_type=F32) + jnp.dot(lo, w, preferred_element_type=F32)


def _rwprep_kernel(first_ref, last_ref, z_ref, zp_ref, zn_ref, mu_ref, w0_ref, w2_ref, a0_ref, a2_ref, g2_ref,
                   kkp_ref, ka_ref, rk_ref, seg_ref, segt_ref,
                   r_o, v_o, kk_o, kd_o, lw_o, bb_o, g_o, bon_o, *, RW, R2W, R2A, RG):
    i = pl.program_id(0)
    has_prev = (1 - first_ref[i]).astype(F32)
    has_next = (1 - last_ref[i]).astype(F32)
    tt = z_ref.shape[0]
    row = lax.broadcasted_iota(jnp.int32, (tt, 1), 0)

    def shifted(lo, hi):
        z = z_ref[:, lo:hi]
        hp = zp_ref[SHIFT_HALO - 1:SHIFT_HALO, lo:hi] * has_prev
        hn = zn_ref[0:1, lo:hi] * has_next
        prev = jnp.where(row == 0, hp, pltpu.roll(z, 1, 0))
        nxt = jnp.where(row == tt - 1, hn, pltpu.roll(z, tt - 1, 0))
        return z + mu_ref[0:1, lo:hi] * (prev - z) + mu_ref[1:2, lo:hi] * (nxt - z)

    o = 3 * RW
    r = shifted(0, RW)
    k = shifted(RW, 2 * RW)
    v = shifted(2 * RW, o)
    wl = shifted(o, o + R2W)
    al = shifted(o + R2W, o + R2W + R2A)
    gl = shifted(o + R2W + R2A, o + R2W + R2A + RG)

    u = -(w0_ref[...] + _bdot(jnp.tanh(wl), w2_ref[...]))
    softplus = jnp.maximum(u, 0.0) + jnp.log(1.0 + jnp.exp(-jnp.abs(u)))
    lw_o[...] = -jnp.exp(-softplus - 0.5)
    a = _sigmoid(a0_ref[...] + _bdot(al, a2_ref[...]))
    g_o[...] = _bdot(_sigmoid(gl), g2_ref[...])

    kk = k * kkp_ref[...]
    kk = kk / jnp.maximum(jnp.sqrt(_segsum(kk * kk, seg_ref, segt_ref)), 1e-12)
    ka = ka_ref[...]
    kd0 = k * (1.0 + (a[:, :RW] - 1.0) * ka)
    kd1 = k * (1.0 + (a[:, RW:] - 1.0) * ka)
    r_o[...] = r
    v_o[...] = v
    kk_o[...] = kk
    kd_o[:, :RW] = kd0
    kd_o[:, RW:] = kd1
    bb_o[:, :RW] = kk * a[:, :RW]
    bb_o[:, RW:] = kk * a[:, RW:]
    bon_o[...] = _segsum(r * rk_ref[...] * (kd0 + kd1), seg_ref, segt_ref) * v


def _blockdiag2(w):
    z = jnp.zeros_like(w[0])
    return jnp.concatenate([jnp.concatenate([w[0], z], 1), jnp.concatenate([z, w[1]], 1)], 0)


def _seq_edges(B, S, Tc, tt):
    first, last = [], []
    for n, L in ((B, S), (B, Tc)):
        per = L // tt
        for _ in range(n):
            first += [1] + [0] * (per - 1)
            last += [0] * (per - 1) + [1]
    return jnp.asarray(np.array(first, np.int32)), jnp.asarray(np.array(last, np.int32))


def _rwprep(rw, p, *, B, S, Tc):
    N, RWIN = rw.shape
    RW = p["rwkv_k_k"].shape[0]
    R2W = 2 * p["rwkv_w2"].shape[1]
    R2A = 2 * p["rwkv_a2"].shape[1]
    RG = p["rwkv_g2"].shape[0]
    tt = _tile(math.gcd(S, Tc), 128)
    nt = N // tt
    hb = tt // SHIFT_HALO
    nhb = N // SHIFT_HALO
    first, last = _seq_edges(B, S, Tc, tt)
    nh = RW // RWKV_HEAD_DIM
    seg = (jnp.arange(RW)[:, None] // RWKV_HEAD_DIM == jnp.arange(LANE)[None, :]).astype(BF16)
    full = lambda shape: pl.BlockSpec(shape, lambda i, f, l: (0,) * len(shape))
    row = lambda w: pl.BlockSpec((tt, w), lambda i, f, l: (i, 0))
    out_w = [RW, RW, RW, 2 * RW, 2 * RW, 2 * RW, RW, RW]
    assert nh <= LANE
    grid_spec = pltpu.PrefetchScalarGridSpec(
        num_scalar_prefetch=2,
        grid=(nt,),
        in_specs=[row(RWIN),
                  pl.BlockSpec((SHIFT_HALO, RWIN), lambda i, f, l: (jnp.maximum(i * hb - 1, 0), 0)),
                  pl.BlockSpec((SHIFT_HALO, RWIN), lambda i, f, l: (jnp.minimum((i + 1) * hb, nhb - 1), 0)),
                  full((2, RWIN)), full((1, 2 * RW)), full((R2W, 2 * RW)), full((1, 2 * RW)), full((R2A, 2 * RW)),
                  full((RG, RW)), full((1, RW)), full((1, RW)), full((1, RW)), full((RW, LANE)), full((LANE, RW))],
        out_specs=[row(w) for w in out_w],
    )
    return pl.pallas_call(
        functools.partial(_rwprep_kernel, RW=RW, R2W=R2W, R2A=R2A, RG=RG),
        grid_spec=grid_spec,
        out_shape=[jax.ShapeDtypeStruct((N, w), F32) for w in out_w],
        compiler_params=_params("parallel"),
        name="rwkv_prep",
    )(first, last, rw, rw, rw, p["rwkv_mu"], p["rwkv_w0"].reshape(1, 2 * RW), _blockdiag2(p["rwkv_w2"]),
      p["rwkv_a0"].reshape(1, 2 * RW), _blockdiag2(p["rwkv_a2"]), p["rwkv_g2"],
      p["rwkv_k_k"].reshape(1, RW), p["rwkv_k_a"].reshape(1, RW), p["rwkv_r_k"].reshape(1, RW), seg, seg.T)


def _bdot(a, b):
    return jnp.dot(a.astype(BF16), b.astype(BF16), preferred_element_type=F32)


def _scanprep_kernel(r_ref, v_ref, kk_ref, kd_ref, lw_ref, bb_ref, A_o, B_o, Q_o, Y_o, *, G, ngrp):
    C = CHUNK
    HD = RWKV_HEAD_DIM
    shift = int(math.log2(C))
    hshift = int(math.log2(HD))
    sign = 1 - 2 * pl.program_id(0)
    ti = lax.broadcasted_iota(jnp.int32, (G, G), 0)
    si = lax.broadcasted_iota(jnp.int32, (G, G), 1)
    same = jnp.right_shift(ti, shift) == jnp.right_shift(si, shift)
    strict = jnp.logical_and(same, sign * (ti - si) > 0)
    ti2 = lax.broadcasted_iota(jnp.int32, (G, 2 * G), 0)
    si2 = jnp.bitwise_and(lax.broadcasted_iota(jnp.int32, (G, 2 * G), 1), G - 1)
    incl2 = jnp.logical_and(jnp.right_shift(ti2, shift) == jnp.right_shift(si2, shift), sign * (ti2 - si2) >= 0)
    eye = (si == ti).astype(F32)
    head0 = lax.broadcasted_iota(jnp.int32, (G, LANE), 1) < HD
    hi2 = lax.broadcasted_iota(jnp.int32, (LANE, LANE), 0)
    hj2 = lax.broadcasted_iota(jnp.int32, (LANE, LANE), 1)
    same_head = jnp.right_shift(hi2, hshift) == jnp.right_shift(hj2, hshift)
    eye_h = (hi2 == hj2).astype(F32)
    sel = lambda a0, a1: jnp.where(head0, a0, a1)
    both = (0, 1)
    fwd = (pl.program_id(0) == 0).astype(F32)
    trow = jnp.bitwise_and(lax.broadcasted_iota(jnp.int32, (G, LANE), 0), C - 1)

    st = []
    for g in range(ngrp):
        rows = slice(g * G, (g + 1) * G)
        lw = lw_ref[rows, :]
        F = lw
        for k in (1 << i for i in range(shift)):
            F = F + jnp.where(trow >= k, pltpu.roll(F, k, 0), 0.0)
        tot = jnp.concatenate([jnp.broadcast_to(F[(c + 1) * C - 1:(c + 1) * C], (C, LANE)) for c in range(G // C)], axis=0)
        P = fwd * F + (1.0 - fwd) * (tot - F + lw)
        eP = jnp.exp(P)
        enP = jnp.exp(-P)
        at = -kk_ref[rows, :] * jnp.exp(P - lw)
        bt = (bb_ref[rows, :] * enP).astype(BF16)
        kt = (kd_ref[rows, :] * enP).astype(BF16)
        rt = r_ref[rows, :] * eP
        vv = v_ref[rows, :].astype(BF16)
        rest = jnp.exp(tot - P)
        bk = jnp.concatenate([bt, kt], axis=0)
        L, lak, grbk = [], [], []
        for h in both:
            hm = head0 if h == 0 else jnp.logical_not(head0)
            ar = jnp.concatenate([jnp.where(hm, at, 0.0), jnp.where(hm, rt, 0.0)], axis=0).astype(BF16)
            gm = lax.dot_general(ar, bk, NT_DIMS, preferred_element_type=F32)
            L.append(jnp.where(strict, gm[:G, :G], 0.0))
            lak.append(jnp.where(strict, gm[:G, G:], 0.0).astype(BF16))
            grbk.append(jnp.where(incl2, gm[G:], 0.0).astype(BF16))
        st.append(dict(at=at.astype(BF16), rt=rt, vv=vv, L=L, lak=lak, grbk=grbk,
                       bkd=[jnp.concatenate([(bb_ref[rows, :] * rest)[c * C:(c + 1) * C],
                                             (kd_ref[rows, :] * rest)[c * C:(c + 1) * C]], axis=0).astype(BF16)
                            for c in range(G // C)],
                       dec=jnp.exp(tot)))

    for s in st:
        s["T"] = [eye + s["L"][h] for h in both]
        lb = [s["L"][h].astype(BF16) for h in both]
        s["Lp"] = [jnp.dot(lb[h], lb[h], preferred_element_type=F32).astype(BF16) for h in both]
    for lvl in range(1, shift):
        for s in st:
            for h in both:
                if lvl < shift - 1:
                    x = jnp.concatenate([s["Lp"][h], s["T"][h].astype(BF16)], axis=1)
                    res = jnp.dot(s["Lp"][h], x, preferred_element_type=F32)
                    s["Lp"][h] = res[:, :G].astype(BF16)
                    s["T"][h] = s["T"][h] + res[:, G:]
                else:
                    s["T"][h] = s["T"][h] + jnp.dot(s["Lp"][h], s["T"][h].astype(BF16), preferred_element_type=F32)
    for s in st:
        s["lv"] = sel(*[jnp.dot(s["lak"][h], s["vv"], preferred_element_type=F32) for h in both]).astype(BF16)
    for s in st:
        x = jnp.concatenate([s["at"], s["lv"]], axis=1)
        tw = [jnp.dot(s["T"][h].astype(BF16), x, preferred_element_type=F32) for h in both]
        s["tw"] = jnp.concatenate([sel(tw[0][:, :LANE], tw[1][:, :LANE]), sel(tw[0][:, LANE:], tw[1][:, LANE:])],
                                  axis=1).astype(BF16)
    for g, s in enumerate(st):
        rows = slice(g * G, (g + 1) * G)
        zv = jnp.concatenate([jnp.zeros((G, LANE), BF16), s["vv"]], axis=1)
        rhs = jnp.concatenate([s["tw"], zv], axis=0)
        qy = [jnp.dot(s["grbk"][h], rhs, preferred_element_type=F32) for h in both]
        Q_o[0, rows, :] = (s["rt"] + sel(qy[0][:, :LANE], qy[1][:, :LANE])).astype(Q_o.dtype)
        Y_o[0, rows, :] = sel(qy[0][:, LANE:], qy[1][:, LANE:])
        for c in range(G // C):
            cr = slice(c * C, (c + 1) * C)
            rc = jnp.concatenate([s["tw"][cr], zv[cr]], axis=0)
            ab = lax.dot_general(s["bkd"][c], rc, TN_DIMS, preferred_element_type=F32)
            k = g * (G // C) + c
            A_o[0, k, 0] = jnp.where(same_head, eye_h * s["dec"][c * C:c * C + 1] + ab[:, :LANE], 0.0).astype(A_o.dtype)
            B_o[0, k, 0] = jnp.where(same_head, ab[:, LANE:], 0.0)


def _scanprep(r, v, kk, kd, lw, bb):
    N, RW = r.shape
    hp = RW // LANE
    G = _tile(N, 2 * CHUNK)
    ngrp = 4 if N % (4 * G) == 0 else 1
    rb = G * ngrp
    nchunks = N // CHUNK
    shared = lambda: pl.BlockSpec((rb, LANE), lambda d, i, h: (i, h))
    perdir = lambda: pl.BlockSpec((rb, LANE), lambda d, i, h: (i, d * hp + h))
    rowo = lambda: pl.BlockSpec((1, rb, LANE), lambda d, i, h: (d, i, h))
    mato = lambda: pl.BlockSpec((1, rb // CHUNK, 1, LANE, LANE), lambda d, i, h: (d, i, h, 0, 0))
    return pl.pallas_call(
        functools.partial(_scanprep_kernel, G=G, ngrp=ngrp),
        grid=(2, N // rb, hp),
        in_specs=[shared(), shared(), shared(), perdir(), perdir(), perdir()],
        out_specs=[mato(), mato(), rowo(), rowo()],
        out_shape=[jax.ShapeDtypeStruct((2, nchunks, hp, LANE, LANE), BF16),
                   jax.ShapeDtypeStruct((2, nchunks, hp, LANE, LANE), F32),
                   jax.ShapeDtypeStruct((2, N, RW), BF16),
                   jax.ShapeDtypeStruct((2, N, RW), F32)],
        compiler_params=_params("parallel", "parallel", "parallel"),
        name="rwkv_chunk_maps",
    )(r, v, kk, kd, lw, bb)


def _scanserial_kernel(A_ref, B_ref, Q_ref, Y0_ref, y_ref, s_ref, *, npair, cb):
    @pl.when(pl.program_id(2) == 0)
    def _():
        s_ref[...] = jnp.zeros_like(s_ref)

    C = CHUNK
    d = pl.program_id(0)
    states = [s_ref[h] for h in range(npair)]
    for k in range(cb):
        c = jnp.where(d == 0, k, cb - 1 - k)
        rows = pl.ds(pl.multiple_of(c * C, C), C)
        for h in range(npair):
            ls = slice(h * LANE, (h + 1) * LANE)
            aq = jnp.concatenate([A_ref[0, c, h], Q_ref[0, rows, ls]], axis=0)
            res = jnp.dot(aq, states[h].astype(BF16), preferred_element_type=F32)
            y_ref[0, rows, ls] = res[LANE:] + Y0_ref[0, rows, ls]
            states[h] = res[:LANE] + B_ref[0, c, h]
    for h in range(npair):
        s_ref[h] = states[h]


def _scanserial(A, Bm, Q, Y0, *, B, S, Tc):
    _, nchunks, npair, _, _ = A.shape
    C = CHUNK
    N = nchunks * C
    RW = npair * LANE
    cb = math.gcd(math.gcd(S // C, Tc // C), 4)
    nl, nc = S // (C * cb), Tc // (C * cb)
    cbase = B * nl

    def chunk_of(d, b, p):
        kc = jnp.where(d == 0, p, nc - 1 - p)
        kl = jnp.where(d == 0, p - nc, nl - 1 - (p - nc))
        return jnp.where(p < nc, cbase + b * nc + kc, b * nl + kl)

    mat = lambda: pl.BlockSpec((1, cb, npair, LANE, LANE), lambda d, b, p: (d, chunk_of(d, b, p), 0, 0, 0))
    row = lambda: pl.BlockSpec((1, cb * C, RW), lambda d, b, p: (d, chunk_of(d, b, p), 0))
    return pl.pallas_call(
        functools.partial(_scanserial_kernel, npair=npair, cb=cb),
        grid=(2, B, nl + nc),
        in_specs=[mat(), mat(), row(), row()],
        out_specs=row(),
        out_shape=jax.ShapeDtypeStruct((2, N, RW), F32),
        scratch_shapes=[pltpu.VMEM((npair, LANE, LANE), F32)],
        compiler_params=_params("parallel", "parallel", "arbitrary"),
        name="rwkv_chunk_chain",
    )(A, Bm, Q, Y0)


def _rwout_kernel(y_ref, bon_ref, g_ref, lg_ref, lb_ref, seg_ref, segt_ref, o_ref):
    y = y_ref[0] + y_ref[1]
    inv = 1.0 / RWKV_HEAD_DIM
    mu = _segsum(y, seg_ref, segt_ref) * inv
    yc = y - mu
    var = _segsum(yc * yc, seg_ref, segt_ref) * inv
    yn = yc * lax.rsqrt(var + GN_EPS) * lg_ref[...] + lb_ref[...]
    o_ref[...] = ((yn + bon_ref[...]) * g_ref[...]).astype(o_ref.dtype)


def _rwout(y2, bonus, g, p, rows):
    RW = g.shape[1]
    tm = _tile(rows, 256)
    seg = (jnp.arange(RW)[:, None] // RWKV_HEAD_DIM == jnp.arange(LANE)[None, :]).astype(BF16)
    full = lambda shape: pl.BlockSpec(shape, lambda i: (0,) * len(shape))
    return pl.pallas_call(
        _rwout_kernel,
        grid=(rows // tm,),
        in_specs=[pl.BlockSpec((2, tm, RW), lambda i: (0, i, 0)),
                  pl.BlockSpec((tm, RW), lambda i: (i, 0)),
                  pl.BlockSpec((tm, RW), lambda i: (i, 0)),
                  full((1, RW)), full((1, RW)), full((RW, LANE)), full((LANE, RW))],
        out_specs=pl.BlockSpec((tm, RW), lambda i: (i, 0)),
        out_shape=jax.ShapeDtypeStruct((rows, RW), BF16),
        compiler_params=_params("parallel"),
        name="rwkv_out",
    )(y2, bonus, g, p["rwkv_ln_g"].reshape(1, RW), p["rwkv_ln_b"].reshape(1, RW), seg, seg.T)


CONV_ROWS = 64


def _conv_kernel(first_ref, last_ref, z_ref, zp_ref, zn_ref, w_ref, b_ref, lg_ref, lb_ref, o_ref, u_ref, c_ref, *, CW, K):
    i = pl.program_id(0)
    has_prev = (1 - first_ref[i]).astype(F32)
    has_next = (1 - last_ref[i]).astype(F32)
    tt = z_ref.shape[0]
    H = CONV_HALO
    nlt = CW // LANE
    glu = lambda ref, lt: ref[:, lt * LANE:(lt + 1) * LANE] * _sigmoid(ref[:, CW + lt * LANE:CW + (lt + 1) * LANE])
    span = tt + SUBLANE * ((H + K // 2) // SUBLANE)
    for lt in range(nlt):
        u_ref[0, lt, 0:H, :] = glu(zp_ref, lt) * has_prev
        u_ref[0, lt, H:H + tt, :] = glu(z_ref, lt)
        u_ref[0, lt, H + tt:, :] = glu(zn_ref, lt) * has_next
        for s in range(1, SUBLANE):
            u_ref[s, lt, 0:span, :] = u_ref[0, lt, s:s + span, :]
    off = H - K // 2
    rb = min(CONV_ROWS, tt)
    nrb = tt // rb

    def block(idx, carry):
        lt = idx // nrb
        r0 = pl.multiple_of((idx % nrb) * rb, rb)
        acc = jnp.zeros((rb, LANE), F32) + b_ref[lt]
        for k in range(K):
            s, j = (off + k) % SUBLANE, (off + k) // SUBLANE
            acc = acc + w_ref[k, lt] * u_ref[s, lt, pl.ds(r0 + SUBLANE * j, rb), :]
        c_ref[lt, pl.ds(r0, rb), :] = acc
        return carry

    lax.fori_loop(0, nlt * nrb, block, 0)
    y = _ln(jnp.concatenate([c_ref[lt] for lt in range(nlt)], axis=1)) * lg_ref[...] + lb_ref[...]
    o_ref[...] = (y * _sigmoid(y)).astype(o_ref.dtype)


def _conv(cv, p, *, B, S, Tc, rows):
    CW = cv.shape[1] // 2
    K = p["conv_w"].shape[0]
    assert K // 2 <= CONV_HALO
    nlt = CW // LANE
    tt = _tile(math.gcd(S, Tc), 256)
    hb = tt // CONV_HALO
    nhb = cv.shape[0] // CONV_HALO
    first, last = _seq_edges(B, S, Tc, tt)
    full = lambda shape: pl.BlockSpec(shape, lambda i, f, l: (0,) * len(shape))
    grid_spec = pltpu.PrefetchScalarGridSpec(
        num_scalar_prefetch=2,
        grid=(rows // tt,),
        in_specs=[pl.BlockSpec((tt, 2 * CW), lambda i, f, l: (i, 0)),
                  pl.BlockSpec((CONV_HALO, 2 * CW), lambda i, f, l: (jnp.maximum(i * hb - 1, 0), 0)),
                  pl.BlockSpec((CONV_HALO, 2 * CW), lambda i, f, l: (jnp.minimum((i + 1) * hb, nhb - 1), 0)),
                  full((K, nlt, 1, LANE)), full((nlt, 1, LANE)), full((1, CW)), full((1, CW))],
        out_specs=pl.BlockSpec((tt, CW), lambda i, f, l: (i, 0)),
        scratch_shapes=[pltpu.VMEM((SUBLANE, nlt, tt + 2 * CONV_HALO, LANE), F32),
                        pltpu.VMEM((nlt, tt, LANE), F32)],
    )
    return pl.pallas_call(
        functools.partial(_conv_kernel, CW=CW, K=K),
        grid_spec=grid_spec,
        out_shape=jax.ShapeDtypeStruct((rows, CW), BF16),
        compiler_params=_params("parallel"),
        name="conv_module",
    )(first, last, cv, cv, cv, p["conv_w"].reshape(K, nlt, 1, LANE), p["conv_b"].reshape(nlt, 1, LANE),
      p["conv_ln_g"].reshape(1, CW), p["conv_ln_b"].reshape(1, CW))


def _merge_kernel(a_ref, r_ref, c_ref, wa_ref, wr_ref, wc_ref, g0_ref, g1_ref, g2_ref, b0_ref, b1_ref, b2_ref, o_ref):
    m = _sigmoid(g0_ref[...] + b0_ref[...]) * jnp.dot(a_ref[...], wa_ref[...], preferred_element_type=F32)
    m = m + _sigmoid(g1_ref[...] + b1_ref[...]) * jnp.dot(r_ref[...], wr_ref[...], preferred_element_type=F32)
    m = m + _sigmoid(g2_ref[...] + b2_ref[...]) * jnp.dot(c_ref[...], wc_ref[...], preferred_element_type=F32)
    o_ref[...] = m.astype(o_ref.dtype)


def _merge(att, rwo, cvo, gt, wa, wr, wc, b_gate, rows):
    D = wa.shape[1]
    tm = _tile(rows, 512)
    tn = _tile(D, 512)
    nj = D // tn
    xin = lambda w: pl.BlockSpec((tm, w), lambda j, i: (i, 0))
    win = lambda w: pl.BlockSpec((w, tn), lambda j, i: (0, j))
    gate = lambda k: pl.BlockSpec((tm, tn), lambda j, i: (i, k * nj + j))
    bias = lambda k: pl.BlockSpec((1, tn), lambda j, i: (0, k * nj + j))
    bg = b_gate.reshape(1, 3 * D)
    return pl.pallas_call(
        _merge_kernel,
        grid=(nj, rows // tm),
        in_specs=[xin(att.shape[1]), xin(rwo.shape[1]), xin(cvo.shape[1]),
                  win(wa.shape[0]), win(wr.shape[0]), win(wc.shape[0]),
                  gate(0), gate(1), gate(2), bias(0), bias(1), bias(2)],
        out_specs=pl.BlockSpec((tm, tn), lambda j, i: (i, j)),
        out_shape=jax.ShapeDtypeStruct((rows, D), BF16),
        compiler_params=_params("parallel", "parallel"),
        name="branch_merge",
    )(att, rwo, cvo, wa, wr, wc, gt, gt, gt, bg, bg, bg)


def _outproj_kernel(m_ref, w_ref, x_ref, gm_ref, sh_ref, sc_ref, lg_ref, lb_ref, wrt_ref, brt_ref,
                    x_o, h_o, lgt_o, *, alpha):
    y = jnp.dot(m_ref[...], w_ref[...], preferred_element_type=F32)
    x1 = _ln(alpha * x_ref[...] + gm_ref[0] * y) * lg_ref[...] + lb_ref[...]
    x_o[...] = x1
    h2 = _ln(x1) * (1.0 + sc_ref[0]) + sh_ref[0]
    h_o[...] = h2.astype(h_o.dtype)
    hi = h2.astype(BF16)
    lo = (h2 - hi.astype(F32)).astype(BF16)
    r1 = jnp.dot(hi, wrt_ref[...], preferred_element_type=F32)
    r2 = jnp.dot(lo, wrt_ref[:, :LANE], preferred_element_type=F32)
    lgt_o[...] = r1[:, :LANE] + r1[:, LANE:] + r2 + brt_ref[...]


def _outproj(m, w_out, xa, mods3, ln_g, ln_b, w_rt, b_rt, rows, modrow, tm, alpha):
    D = w_out.shape[0]
    full = lambda shape: pl.BlockSpec(shape, lambda i: (0,) * len(shape))
    rowb = lambda w: pl.BlockSpec((tm, w), lambda i: (i, 0))
    mod = lambda blk: pl.BlockSpec((1, 1, D), lambda i: (modrow(i), 0, blk))
    return pl.pallas_call(
        functools.partial(_outproj_kernel, alpha=alpha),
        grid=(rows // tm,),
        in_specs=[rowb(D), full((D, D)), rowb(D), mod(2), mod(3), mod(4), full((1, D)), full((1, D)),
                  full((D, 2 * LANE)), full((1, LANE))],
        out_specs=[rowb(D), rowb(D), rowb(LANE)],
        out_shape=[jax.ShapeDtypeStruct((rows, D), F32), jax.ShapeDtypeStruct((rows, D), BF16),
                   jax.ShapeDtypeStruct((rows, LANE), F32)],
        compiler_params=_params("parallel"),
        name="out_proj",
    )(m, w_out, xa, mods3, mods3, mods3, ln_g.reshape(1, D), ln_b.reshape(1, D), w_rt, b_rt)


def _route_kernel(l_ref, tri_ref, id_o, w_o, cnt_o, cnt_ref):
    lg = l_ref[...]
    lane = lax.broadcasted_iota(jnp.int32, lg.shape, 1)
    neg = -jnp.inf
    big = jnp.int32(2 ** 30)
    isg = jnp.logical_and(lane >= N_EXPERTS, lane < N_EXPERTS + N_GROUPS)
    gl = jnp.where(isg, lg, neg)
    gmax = jnp.max(gl, -1, keepdims=True)
    grp = jnp.min(jnp.where(gl == gmax, lane - N_EXPERTS, big), -1, keepdims=True)
    grp_w = 1.0 / jnp.sum(jnp.exp(gl - gmax), -1, keepdims=True)
    ing = jnp.logical_and(lane < N_EXPERTS, jnp.right_shift(lane, int(math.log2(EXPERTS_PER_GROUP))) == grp)
    el = jnp.where(ing, lg, neg)
    e1 = jnp.max(el, -1, keepdims=True)
    i1 = jnp.min(jnp.where(el == e1, lane, big), -1, keepdims=True)
    psum = jnp.sum(jnp.exp(el - e1), -1, keepdims=True)
    el2 = jnp.where(lane == i1, neg, el)
    e2 = jnp.max(el2, -1, keepdims=True)
    i2 = jnp.min(jnp.where(el2 == e2, lane, big), -1, keepdims=True)
    p1 = 1.0 / psum
    p2 = jnp.exp(e2 - e1) / psum
    tot = p1 + p2
    w_o[...] = jnp.where(lane == 0, grp_w * p1 / tot, jnp.where(lane == 1, grp_w * p2 / tot, 0.0))

    @pl.when(pl.program_id(0) == 0)
    def _():
        cnt_ref[...] = jnp.zeros_like(cnt_ref)

    hit = jnp.logical_or(lane == i1, lane == i2)
    before = cnt_ref[...] + jnp.dot(tri_ref[...], jnp.where(hit, 1.0, 0.0).astype(BF16), preferred_element_type=F32)
    r1 = jnp.sum(jnp.where(lane == i1, before, 0.0), -1, keepdims=True).astype(jnp.int32)
    r2 = jnp.sum(jnp.where(lane == i2, before, 0.0), -1, keepdims=True).astype(jnp.int32)
    id_o[...] = jnp.where(lane == 0, i1, jnp.where(lane == 1, i2, jnp.where(lane == 2, r1, jnp.where(lane == 3, r2, 0))))
    cnt_ref[...] = cnt_ref[...] + jnp.sum(jnp.where(hit, 1.0, 0.0), axis=0, keepdims=True)
    cnt_o[...] = cnt_ref[...]


def _route(logits):
    n = logits.shape[0]
    tm = _tile(n, 1024)
    tri = (jnp.arange(tm)[None, :] < jnp.arange(tm)[:, None]).astype(BF16)
    blk = lambda: pl.BlockSpec((tm, LANE), lambda i: (i, 0))
    return pl.pallas_call(
        _route_kernel,
        grid=(n // tm,),
        in_specs=[blk(), pl.BlockSpec((tm, tm), lambda i: (0, 0))],
        out_specs=[blk(), blk(), pl.BlockSpec((1, LANE), lambda i: (0, 0))],
        out_shape=[jax.ShapeDtypeStruct((n, LANE), jnp.int32), jax.ShapeDtypeStruct((n, LANE), F32),
                   jax.ShapeDtypeStruct((1, LANE), F32)],
        scratch_shapes=[pltpu.VMEM((1, LANE), F32)],
        compiler_params=_params("arbitrary"),
        name="route",
    )(logits, tri)


def _moe_kernel(be_ref, act_ref, x_ref, wg_ref, wu_ref, wd_ref, o_ref, wgb, wub, wdb):
    i = pl.program_id(0)

    @pl.when(jnp.logical_or(i == 0, be_ref[i] != be_ref[jnp.maximum(i - 1, 0)]))
    def _():
        wgb[...] = wg_ref[0, 0].astype(BF16)
        wub[...] = wu_ref[0, 0].astype(BF16)
        wdb[...] = wd_ref[0, 0].astype(BF16)

    @pl.when(act_ref[i] > 0)
    def _():
        x = x_ref[...]
        gte = jnp.dot(x, wgb[...], preferred_element_type=F32)
        up = jnp.dot(x, wub[...], preferred_element_type=F32)
        h = (gte * _sigmoid(gte) * up).astype(BF16)
        o_ref[...] = jnp.dot(h, wdb[...], preferred_element_type=F32)

    @pl.when(act_ref[i] == 0)
    def _():
        o_ref[...] = jnp.zeros_like(o_ref)


def _moe_ffn(xg, block_e, active, wg, wu, wd, layer):
    R, D = xg.shape
    FF = wg.shape[3]
    nb = R // MOE_BLOCK
    once = pl.Buffered(1)
    grid_spec = pltpu.PrefetchScalarGridSpec(
        num_scalar_prefetch=2,
        grid=(nb,),
        in_specs=[pl.BlockSpec((MOE_BLOCK, D), lambda i, be, ac: (i, 0)),
                  pl.BlockSpec((1, 1, D, FF), lambda i, be, ac: (layer, be[i], 0, 0), pipeline_mode=once),
                  pl.BlockSpec((1, 1, D, FF), lambda i, be, ac: (layer, be[i], 0, 0), pipeline_mode=once),
                  pl.BlockSpec((1, 1, FF, D), lambda i, be, ac: (layer, be[i], 0, 0), pipeline_mode=once)],
        out_specs=pl.BlockSpec((MOE_BLOCK, D), lambda i, be, ac: (i, 0)),
        scratch_shapes=[pltpu.VMEM((D, FF), BF16), pltpu.VMEM((D, FF), BF16), pltpu.VMEM((FF, D), BF16)],
    )
    return pl.pallas_call(
        _moe_kernel,
        grid_spec=grid_spec,
        out_shape=jax.ShapeDtypeStruct((R, D), F32),
        compiler_params=_params("arbitrary"),
        name="moe_ffn",
    )(block_e, active, xg, wg, wu, wd)


def _dispatch(ids, rank, counts):
    n = ids.shape[0]
    A = 2 * n
    padded = (counts + MOE_BLOCK - 1) // MOE_BLOCK * MOE_BLOCK
    pad_end = jnp.cumsum(padded)
    pad_start = pad_end - padded
    start = jnp.cumsum(counts) - counts
    dest = jnp.take(pad_start, ids, mode="clip") + rank
    nb = -(-A // MOE_BLOCK) + N_EXPERTS
    blk_start = jnp.arange(nb, dtype=jnp.int32) * MOE_BLOCK
    block_e = jnp.minimum(jnp.sum(pad_end[None, :] <= blk_start[:, None], axis=1), N_EXPERTS - 1).astype(jnp.int32)
    active = (blk_start < pad_end[-1]).astype(jnp.int32)
    order = jnp.argsort(ids.reshape(-1))
    slot_e = jnp.repeat(block_e, MOE_BLOCK)
    slot_r = jnp.arange(nb * MOE_BLOCK, dtype=jnp.int32) - jnp.take(pad_start, slot_e)
    src = jnp.clip(jnp.take(start, slot_e) + slot_r, 0, A - 1)
    slot_tok = jnp.where(slot_r < jnp.take(counts, slot_e), jnp.take(order, src) // 2, 0).astype(jnp.int32)
    return slot_tok, dest.astype(jnp.int32), block_e, active


def _final_kernel(x_ref, y0_ref, y1_ref, w_ref, gm_ref, lg_ref, lb_ref, o_ref, *, alpha):
    w = w_ref[...]
    f = w[:, 0:1] * y0_ref[...] + w[:, 1:2] * y1_ref[...]
    o_ref[...] = _ln(alpha * x_ref[...] + gm_ref[0] * f) * lg_ref[...] + lb_ref[...]


def _final(x1, y0, y1, wts, mods3, ln_g, ln_b, rows, modrow, tm, alpha):
    D = x1.shape[1]
    rowb = lambda w: pl.BlockSpec((tm, w), lambda i: (i, 0))
    full = lambda shape: pl.BlockSpec(shape, lambda i: (0,) * len(shape))
    return pl.pallas_call(
        functools.partial(_final_kernel, alpha=alpha),
        grid=(rows // tm,),
        in_specs=[rowb(D), rowb(D), rowb(D), rowb(LANE),
                  pl.BlockSpec((1, 1, D), lambda i: (modrow(i), 0, 5)), full((1, D)), full((1, D))],
        out_specs=rowb(D),
        out_shape=jax.ShapeDtypeStruct((rows, D), F32),
        compiler_params=_params("parallel"),
        name="ffn_residual",
    )(x1, y0, y1, wts, mods3, ln_g.reshape(1, D), ln_b.reshape(1, D))


def _rope_tables(S):
    rows = S // GRID_W
    row = jnp.repeat(jnp.arange(rows, dtype=F32), GRID_W)
    col = jnp.tile(jnp.arange(GRID_W, dtype=F32), rows)
    nf = ATT_HEAD_DIM // 4
    inv = ROPE_THETA ** (-jnp.arange(nf, dtype=F32) / nf)
    ar, ac = row[:, None] * inv, col[:, None] * inv
    cos = jnp.concatenate([jnp.cos(ar), jnp.cos(ar), jnp.cos(ac), jnp.cos(ac)], -1)
    sin = jnp.concatenate([-jnp.sin(ar), jnp.sin(ar), -jnp.sin(ac), jnp.sin(ac)], -1)
    return cos, sin


def kernel(x, c, ctx, c_ctx, w_mod, b_mod, w_in, b_gate, q_norm, k_norm, w_att_o, rwkv_mu, rwkv_w0, rwkv_w2, rwkv_a0, rwkv_a2, rwkv_g2, rwkv_k_k, rwkv_k_a, rwkv_r_k, rwkv_ln_g, rwkv_ln_b, w_rwkv_o, conv_w, conv_b, conv_ln_g, conv_ln_b, w_conv_o, w_out, ln1_g, ln1_b, w_group, b_group, w_router, b_router, w_e_gate, w_e_up, w_e_down, ln2_g, ln2_b):
    B, S, D = x.shape
    Tc = ctx.shape[1]
    depth = w_mod.shape[0]
    NL, NC = B * S, B * Tc
    N = NL + NC
    alpha = (2 * depth) ** 0.25
    att_w = w_att_o.shape[1]
    heads = att_w // ATT_HEAD_DIM
    rw_in = rwkv_mu.shape[-1]
    conv_cw = w_conv_o.shape[1]
    kv_w = (w_in.shape[-1] - att_w - rw_in - 2 * conv_cw - 3 * D) // 2
    assert kv_w == heads // ATT_GROUP * ATT_HEAD_DIM
    o_rw = att_w + 2 * kv_w
    o_cv = o_rw + rw_in
    o_gt = o_cv + 2 * conv_cw

    tmod = _tile(math.gcd(S, Tc), 256)
    tps = S // tmod
    nlt = NL // tmod
    modrow = lambda i: jnp.where(i < nlt, i // tps, B)
    cos_t, sin_t = _rope_tables(S)

    R = -(-(B + 1) // SUBLANE) * SUBLANE
    cv = jnp.zeros((R, D), F32).at[:B].set(c).at[B].set(c_ctx)
    xa = jnp.concatenate([x.reshape(NL, D), ctx.reshape(NC, D)], 0)

    for l in range(depth):
        last = l == depth - 1
        rows = NL if last else N
        p = dict(rwkv_mu=rwkv_mu[l], rwkv_w0=rwkv_w0[l], rwkv_w2=rwkv_w2[l], rwkv_a0=rwkv_a0[l], rwkv_a2=rwkv_a2[l],
                 rwkv_g2=rwkv_g2[l], rwkv_k_k=rwkv_k_k[l], rwkv_k_a=rwkv_k_a[l], rwkv_r_k=rwkv_r_k[l],
                 rwkv_ln_g=rwkv_ln_g[l], rwkv_ln_b=rwkv_ln_b[l], conv_w=conv_w[l], conv_b=conv_b[l],
                 conv_ln_g=conv_ln_g[l], conv_ln_b=conv_ln_b[l])
        mods3 = _adaln(cv, w_mod, b_mod, l).reshape(R, 1, 6 * D)
        wl = w_in[l]
        h = _lnmod(xa, mods3, 0, 1, N, modrow, tmod)
        qkv = _matmul(h, wl[:, :o_rw].astype(BF16), 512)
        rw = _matmul(h, wl[:, o_rw:o_cv].astype(BF16), 1152)
        cvx = _matmul(h, wl[:, o_cv:o_gt].astype(BF16), 1024, rows)
        gt = _matmul(h, wl[:, o_gt:].astype(BF16), 1024, rows)

        qn, kn = q_norm[l].reshape(1, LANE), k_norm[l].reshape(1, LANE)
        att = _attention(qkv, qn, kn, cos_t, sin_t, B=B, S=S, Tc=Tc, heads=heads, latent_queries=True)
        if not last:
            att_c = _attention(qkv, qn, kn, cos_t, sin_t, B=B, S=S, Tc=Tc, heads=heads, latent_queries=False)
            att = jnp.concatenate([att, att_c], 0)

        r, v, kk, kd, lw, bb, g, bonus = _rwprep(rw, p, B=B, S=S, Tc=Tc)
        Am, Bm, Qm, Y0 = _scanprep(r, v, kk, kd, lw, bb)
        y2 = _scanserial(Am, Bm, Qm, Y0, B=B, S=S, Tc=Tc)
        rwo = _rwout(y2, bonus, g, p, rows)
        cvo = _conv(cvx, p, B=B, S=S, Tc=Tc, rows=rows)

        m = _merge(att, rwo, cvo, gt, w_att_o[l].astype(BF16), w_rwkv_o[l].astype(BF16), w_conv_o[l].astype(BF16),
                   b_gate[l], rows)
        w_rt = jnp.zeros((D, LANE), F32).at[:, :N_EXPERTS].set(w_router[l]).at[:, N_EXPERTS:N_EXPERTS + N_GROUPS].set(w_group[l])
        b_rt = jnp.zeros((1, LANE), F32).at[0, :N_EXPERTS].set(b_router[l]).at[0, N_EXPERTS:N_EXPERTS + N_GROUPS].set(b_group[l])
        w_rt_hi = w_rt.astype(BF16)
        w_rt = jnp.concatenate([w_rt_hi, (w_rt - w_rt_hi.astype(F32)).astype(BF16)], 1)
        x1, h2, logits = _outproj(m, w_out[l].astype(BF16), xa, mods3, ln1_g[l], ln1_b[l], w_rt, b_rt, rows, modrow,
                                  tmod, alpha)

        ids128, wts128, cnt128 = _route(logits)
        slot_tok, dest, block_e, active = _dispatch(ids128[:, :2], ids128[:, 2:4], cnt128[0, :N_EXPERTS].astype(jnp.int32))
        xg = jnp.take(h2, slot_tok, axis=0, mode="clip")
        yb = _moe_ffn(xg, block_e, active, w_e_gate, w_e_up, w_e_down, l)
        y0 = jnp.take(yb, dest[:, 0], axis=0, mode="clip")
        y1 = jnp.take(yb, dest[:, 1], axis=0, mode="clip")
        xa = _final(x1, y0, y1, wts128, mods3, ln2_g[l], ln2_b[l], rows, modrow, tmod, alpha)

    return xa[:NL].reshape(B, S, D)
```

```python
import functools
import math

import numpy as np
import jax
import jax.numpy as jnp
from jax import lax
from jax.experimental import pallas as pl
from jax.experimental.pallas import tpu as pltpu

F32 = jnp.float32
BF16 = jnp.bfloat16
HI = lax.Precision.HIGHEST

LANE = 128
SUBLANE = 8
ATT_HEAD_DIM = 128
ATT_GROUP = 4
ATT_HEADS_PER_STEP = 2
ROPE_THETA = 10000.0
GRID_W = 64
RWKV_HEAD_DIM = 64
CHUNK = 64
CONV_HALO = 16
SHIFT_HALO = 8
N_GROUPS = 4
EXPERTS_PER_GROUP = 8
N_EXPERTS = N_GROUPS * EXPERTS_PER_GROUP
MOE_BLOCK = 256
GN_EPS = 64e-5
LN_EPS = 1e-6
RMS_EPS = 1e-6
VMEM_LIMIT = 56 * 1024 * 1024

NT_DIMS = (((1,), (1,)), ((), ()))
TN_DIMS = (((0,), (0,)), ((), ()))


def _params(*sem):
    return pltpu.CompilerParams(dimension_semantics=sem, vmem_limit_bytes=VMEM_LIMIT)


def _sigmoid(x):
    return 1.0 / (1.0 + jnp.exp(-x))


def _ln(x):
    mu = jnp.mean(x, -1, keepdims=True)
    xc = x - mu
    var = jnp.mean(xc * xc, -1, keepdims=True)
    return xc * lax.rsqrt(var + LN_EPS)


def _tile(n, pref):
    t = min(n, pref)
    while n % t:
        t -= SUBLANE
    return t


def _adaln_kernel(c_ref, w_ref, b_ref, o_ref):
    c = c_ref[...]
    s = c * _sigmoid(c)
    o_ref[...] = jnp.dot(s, w_ref[0], precision=HI, preferred_element_type=F32) + b_ref[0]


def _adaln(cv, w_mod, b_mod, layer):
    R, D = cv.shape
    L, _, W = w_mod.shape
    tn = _tile(W, 1024)
    return pl.pallas_call(
        _adaln_kernel,
        grid=(W // tn,),
        in_specs=[pl.BlockSpec((R, D), lambda j: (0, 0)),
                  pl.BlockSpec((1, D, tn), lambda j: (layer, 0, j)),
                  pl.BlockSpec((1, 1, tn), lambda j: (layer, 0, j))],
        out_specs=pl.BlockSpec((R, tn), lambda j: (0, j)),
        out_shape=jax.ShapeDtypeStruct((R, W), F32),
        compiler_params=_params("parallel"),
        name="adaln",
    )(cv, w_mod, b_mod.reshape(L, 1, W))


def _stream_specs(src, tm, D):
    lat, tail, nlt, off = src
    return [lat, tail], [pl.BlockSpec((tm, D), lambda i: (jnp.minimum(i, nlt - 1), 0)),
                         pl.BlockSpec((tm, D), lambda i: (jnp.maximum(i - nlt, 0) + off, 0))]


def _lnmod_kernel(xl_ref, xt_ref, sh_ref, sc_ref, o_ref, *, nlt):
    x = jnp.where(pl.program_id(0) < nlt, xl_ref[...], xt_ref[...])
    o_ref[...] = (_ln(x) * (1.0 + sc_ref[0]) + sh_ref[0]).astype(o_ref.dtype)


def _lnmod(src, mods3, shift_blk, scale_blk, rows, modrow, tm):
    D = src[0].shape[1]
    arrs, specs = _stream_specs(src, tm, D)
    return pl.pallas_call(
        functools.partial(_lnmod_kernel, nlt=src[2]),
        grid=(rows // tm,),
        in_specs=specs + [pl.BlockSpec((1, 1, D), lambda i: (modrow(i), 0, shift_blk)),
                          pl.BlockSpec((1, 1, D), lambda i: (modrow(i), 0, scale_blk))],
        out_specs=pl.BlockSpec((tm, D), lambda i: (i, 0)),
        out_shape=jax.ShapeDtypeStruct((rows, D), BF16),
        compiler_params=_params("parallel"),
        name="lnmod",
    )(*arrs, mods3, mods3)


def _mm_kernel(x_ref, w_ref, o_ref):
    o_ref[...] = jnp.dot(x_ref[...], w_ref[...], preferred_element_type=F32).astype(o_ref.dtype)


def _matmul(x, w, tn_pref, rows=None, out_dtype=F32, tm_pref=1024):
    M, K = x.shape
    M = rows or M
    N = w.shape[1]
    tm = _tile(M, tm_pref)
    tn = _tile(N, tn_pref)
    return pl.pallas_call(
        _mm_kernel,
        grid=(N // tn, M // tm),
        in_specs=[pl.BlockSpec((tm, K), lambda j, i: (i, 0)),
                  pl.BlockSpec((K, tn), lambda j, i: (0, j))],
        out_specs=pl.BlockSpec((tm, tn), lambda j, i: (i, j)),
        out_shape=jax.ShapeDtypeStruct((M, N), out_dtype),
        compiler_params=_params("parallel", "parallel"),
        name="matmul",
    )(x, w)


def _rms(t, g):
    return t * lax.rsqrt(jnp.mean(t * t, -1, keepdims=True) + RMS_EPS) * g


def _rope(t, cos, sin_signed):
    lane = lax.broadcasted_iota(jnp.int32, t.shape, 1)
    first = (lane & 63) < 32
    partner = jnp.where(first, pltpu.roll(t, LANE - 32, 1), pltpu.roll(t, 32, 1))
    return t * cos + partner * sin_signed


def _attn_kernel(*refs, rope, has_lat, tq, scale, hg):
    if has_lat:
        (q_ref, kc_ref, vc_ref, qn_ref, kn_ref, kl_ref, vl_ref, cq_ref, sq_ref, ck_ref, sk_ref,
         o_ref, kcb, vcb, klb, vlb) = refs
    else:
        q_ref, kc_ref, vc_ref, qn_ref, kn_ref, o_ref, kcb, vcb = refs

    @pl.when(pl.program_id(2) == 0)
    def _():
        kcb[...] = _rms(kc_ref[...], kn_ref[...]).astype(BF16)
        vcb[:, :LANE] = vc_ref[...].astype(BF16)
        vcb[:, LANE:] = jnp.ones(vc_ref.shape, BF16)
        if has_lat:
            klb[...] = _rope(_rms(kl_ref[...], kn_ref[...]), ck_ref[...], sk_ref[...]).astype(BF16)
            vlb[:, :LANE] = vl_ref[...].astype(BF16)
            vlb[:, LANE:] = jnp.ones(vl_ref.shape, BF16)

    groups = [range(g0, g0 + hg) for g0 in range(0, ATT_GROUP, hg)]
    scores = {}

    def qk(gi):
        qs = []
        for g in groups[gi]:
            qg = _rms(q_ref[:, g * LANE:(g + 1) * LANE], qn_ref[...])
            if rope:
                qg = _rope(qg, cq_ref[...], sq_ref[...])
            qs.append((qg * (scale * math.log2(math.e))).astype(BF16))
        qa = jnp.concatenate(qs, axis=0)
        sc = lax.dot_general(qa, kcb[...], NT_DIMS, preferred_element_type=F32)
        sl = lax.dot_general(qa, klb[...], NT_DIMS, preferred_element_type=F32) if has_lat else None
        scores[gi] = (sc, sl)

    def softmax_pv(gi):
        sc, sl = scores.pop(gi)
        m = jnp.max(sc, -1, keepdims=True)
        if has_lat:
            m = jnp.maximum(m, jnp.max(sl, -1, keepdims=True))
        od = jnp.dot(jnp.exp2(sc - m).astype(BF16), vcb[...], preferred_element_type=F32)
        if has_lat:
            od = od + jnp.dot(jnp.exp2(sl - m).astype(BF16), vlb[...], preferred_element_type=F32)
        o = od[:, :LANE] / od[:, LANE:]
        for n, g in enumerate(groups[gi]):
            o_ref[:, g * LANE:(g + 1) * LANE] = o[n * tq:(n + 1) * tq].astype(o_ref.dtype)

    qk(0)
    for gi in range(1, len(groups)):
        qk(gi)
        softmax_pv(gi - 1)
    softmax_pv(len(groups) - 1)


def _attention(qkv, qn, kn, cos_t, sin_t, *, B, S, Tc, heads, latent_queries):
    NL = B * S
    kvh = heads // ATT_GROUP
    GW = ATT_GROUP * LANE
    scale = ATT_HEAD_DIM ** -0.5
    Tq = S if latent_queries else Tc
    tq = _tile(Tq, 256)
    nq = Tq // tq
    qbase = 0 if latent_queries else NL // tq
    cbase = NL // Tc
    common = [pl.BlockSpec((tq, GW), lambda b, j, i: (qbase + b * nq + i, j)),
              pl.BlockSpec((Tc, LANE), lambda b, j, i: (cbase + b, heads + j)),
              pl.BlockSpec((Tc, LANE), lambda b, j, i: (cbase + b, heads + kvh + j)),
              pl.BlockSpec((1, LANE), lambda b, j, i: (0, 0)),
              pl.BlockSpec((1, LANE), lambda b, j, i: (0, 0))]
    args = [qkv, qkv, qkv, qn, kn]
    scratch = [pltpu.VMEM((Tc, LANE), BF16), pltpu.VMEM((Tc, 2 * LANE), BF16)]
    if latent_queries:
        common += [pl.BlockSpec((S, LANE), lambda b, j, i: (b, heads + j)),
                   pl.BlockSpec((S, LANE), lambda b, j, i: (b, heads + kvh + j)),
                   pl.BlockSpec((tq, LANE), lambda b, j, i: (i, 0)),
                   pl.BlockSpec((tq, LANE), lambda b, j, i: (i, 0)),
                   pl.BlockSpec((S, LANE), lambda b, j, i: (0, 0)),
                   pl.BlockSpec((S, LANE), lambda b, j, i: (0, 0))]
        args += [qkv, qkv, cos_t, sin_t, cos_t, sin_t]
        scratch += [pltpu.VMEM((S, LANE), BF16), pltpu.VMEM((S, 2 * LANE), BF16)]
    return pl.pallas_call(
        functools.partial(_attn_kernel, rope=latent_queries, has_lat=latent_queries, tq=tq, scale=scale, hg=ATT_HEADS_PER_STEP),
        grid=(B, kvh, nq),
        in_specs=common,
        out_specs=pl.BlockSpec((tq, GW), lambda b, j, i: (b * nq + i, j)),
        out_shape=jax.ShapeDtypeStruct((B * Tq, heads * LANE), BF16),
        scratch_shapes=scratch,
        compiler_params=_params("parallel", "parallel", "arbitrary"),
        name="attention_lat" if latent_queries else "attention_ctx",
    )(*args)


def _segsum(x, seg_ref, segt_ref):
    return _split_dot(_split_dot(x, seg_ref[...]), segt_ref[...])


def _split_dot(x, w):
    hi = x.astype(BF16)
    lo = (x - hi.astype(F32)).astype(BF16)
    return jnp.dot(hi, w, preferred_element_type=F32) + jnp.dot(lo, w, preferred_element_type=F32)


def _rwprep_kernel(first_ref, last_ref, z_ref, zp_ref, zn_ref, mu_ref, w0_ref, w2_ref, a0_ref, a2_ref, g2_ref,
                   kkp_ref, ka_ref, rk_ref, seg_ref, segt_ref,
                   r_o, v_o, kk_o, kd_o, lw_o, bb_o, g_o, bon_o, *, RW, R2W, R2A, RG):
    i = pl.program_id(0)
    has_prev = (1 - first_ref[i]).astype(F32)
    has_next = (1 - last_ref[i]).astype(F32)
    tt = z_ref.shape[0]
    row8 = lax.broadcasted_iota(jnp.int32, (SUBLANE, 1), 0)

    def shifted(lo, hi):
        z = z_ref[:, lo:hi]
        hp = zp_ref[SHIFT_HALO - 1:SHIFT_HALO, lo:hi] * has_prev
        hn = zn_ref[0:1, lo:hi] * has_next
        prev = pltpu.roll(z, 1, 0)
        nxt = pltpu.roll(z, tt - 1, 0)
        prev = jnp.concatenate([jnp.where(row8 == 0, hp, prev[:SUBLANE]), prev[SUBLANE:]], axis=0)
        nxt = jnp.concatenate([nxt[:tt - SUBLANE], jnp.where(row8 == SUBLANE - 1, hn, nxt[tt - SUBLANE:])], axis=0)
        m0, m1 = mu_ref[0:1, lo:hi], mu_ref[1:2, lo:hi]
        return (1.0 - m0 - m1) * z + m0 * prev + m1 * nxt

    o = 3 * RW
    r = shifted(0, RW)
    k = shifted(RW, 2 * RW)
    v = shifted(2 * RW, o)
    wl = shifted(o, o + R2W)
    al = shifted(o + R2W, o + R2W + R2A)
    gl = shifted(o + R2W + R2A, o + R2W + R2A + RG)

    u = -(w0_ref[...] + _bdot(jnp.tanh(wl), w2_ref[...]))
    softplus = jnp.maximum(u, 0.0) + jnp.log(1.0 + jnp.exp(-jnp.abs(u)))
    lw_o[...] = -jnp.exp(-softplus - 0.5)
    a = _sigmoid(a0_ref[...] + _bdot(al, a2_ref[...]))
    g_o[...] = _bdot(_sigmoid(gl), g2_ref[...])

    kk = k * kkp_ref[...]
    kk = kk / jnp.maximum(jnp.sqrt(_segsum(kk * kk, seg_ref, segt_ref)), 1e-12)
    ka = ka_ref[...]
    kd0 = k * (1.0 + (a[:, :RW] - 1.0) * ka)
    kd1 = k * (1.0 + (a[:, RW:] - 1.0) * ka)
    r_o[...] = r
    v_o[...] = v
    kk_o[...] = kk
    kd_o[:, :RW] = kd0
    kd_o[:, RW:] = kd1
    bb_o[:, :RW] = kk * a[:, :RW]
    bb_o[:, RW:] = kk * a[:, RW:]
    bon_o[...] = _segsum(r * rk_ref[...] * (kd0 + kd1), seg_ref, segt_ref) * v


def _blockdiag2(w):
    z = jnp.zeros_like(w[0])
    return jnp.concatenate([jnp.concatenate([w[0], z], 1), jnp.concatenate([z, w[1]], 1)], 0)


def _seq_edges(B, S, Tc, tt):
    first, last = [], []
    for n, L in ((B, S), (B, Tc)):
        per = L // tt
        for _ in range(n):
            first += [1] + [0] * (per - 1)
            last += [0] * (per - 1) + [1]
    return jnp.asarray(np.array(first, np.int32)), jnp.asarray(np.array(last, np.int32))


def _rwprep(rw, p, *, B, S, Tc):
    N, RWIN = rw.shape
    RW = p["rwkv_k_k"].shape[0]
    R2W = 2 * p["rwkv_w2"].shape[1]
    R2A = 2 * p["rwkv_a2"].shape[1]
    RG = p["rwkv_g2"].shape[0]
    tt = _tile(math.gcd(S, Tc), 128)
    nt = N // tt
    hb = tt // SHIFT_HALO
    nhb = N // SHIFT_HALO
    first, last = _seq_edges(B, S, Tc, tt)
    nh = RW // RWKV_HEAD_DIM
    seg = (jnp.arange(RW)[:, None] // RWKV_HEAD_DIM == jnp.arange(LANE)[None, :]).astype(BF16)
    full = lambda shape: pl.BlockSpec(shape, lambda i, f, l: (0,) * len(shape))
    row = lambda w: pl.BlockSpec((tt, w), lambda i, f, l: (i, 0))
    out_w = [RW, RW, RW, 2 * RW, 2 * RW, 2 * RW, RW, RW]
    assert nh <= LANE
    grid_spec = pltpu.PrefetchScalarGridSpec(
        num_scalar_prefetch=2,
        grid=(nt,),
        in_specs=[row(RWIN),
                  pl.BlockSpec((SHIFT_HALO, RWIN), lambda i, f, l: (jnp.maximum(i * hb - 1, 0), 0)),
                  pl.BlockSpec((SHIFT_HALO, RWIN), lambda i, f, l: (jnp.minimum((i + 1) * hb, nhb - 1), 0)),
                  full((2, RWIN)), full((1, 2 * RW)), full((R2W, 2 * RW)), full((1, 2 * RW)), full((R2A, 2 * RW)),
                  full((RG, RW)), full((1, RW)), full((1, RW)), full((1, RW)), full((RW, LANE)), full((LANE, RW))],
        out_specs=[row(w) for w in out_w],
    )
    return pl.pallas_call(
        functools.partial(_rwprep_kernel, RW=RW, R2W=R2W, R2A=R2A, RG=RG),
        grid_spec=grid_spec,
        out_shape=[jax.ShapeDtypeStruct((N, w), F32) for w in out_w],
        compiler_params=_params("parallel"),
        name="rwkv_prep",
    )(first, last, rw, rw, rw, p["rwkv_mu"], p["rwkv_w0"].reshape(1, 2 * RW), _blockdiag2(p["rwkv_w2"]),
      p["rwkv_a0"].reshape(1, 2 * RW), _blockdiag2(p["rwkv_a2"]), p["rwkv_g2"],
      p["rwkv_k_k"].reshape(1, RW), p["rwkv_k_a"].reshape(1, RW), p["rwkv_r_k"].reshape(1, RW), seg, seg.T)


def _bdot(a, b):
    return jnp.dot(a.astype(BF16), b.astype(BF16), preferred_element_type=F32)


def _scanprep_kernel(r_ref, v_ref, kk_ref, kd_ref, lw_ref, bb_ref, A_o, B_o, Q_o, Y_o, *, G, ngrp):
    C = CHUNK
    HD = RWKV_HEAD_DIM
    shift = int(math.log2(C))
    hshift = int(math.log2(HD))
    sign = 1 - 2 * pl.program_id(0)
    ti = lax.broadcasted_iota(jnp.int32, (G, G), 0)
    si = lax.broadcasted_iota(jnp.int32, (G, G), 1)
    same = jnp.right_shift(ti, shift) == jnp.right_shift(si, shift)
    strict = jnp.logical_and(same, sign * (ti - si) > 0)
    ti2 = lax.broadcasted_iota(jnp.int32, (G, 2 * G), 0)
    si2 = jnp.bitwise_and(lax.broadcasted_iota(jnp.int32, (G, 2 * G), 1), G - 1)
    incl2 = jnp.logical_and(jnp.right_shift(ti2, shift) == jnp.right_shift(si2, shift), sign * (ti2 - si2) >= 0)
    eye = (si == ti).astype(F32)
    head0 = lax.broadcasted_iota(jnp.int32, (G, LANE), 1) < HD
    hi2 = lax.broadcasted_iota(jnp.int32, (LANE, LANE), 0)
    hj2 = lax.broadcasted_iota(jnp.int32, (LANE, LANE), 1)
    same_head = jnp.right_shift(hi2, hshift) == jnp.right_shift(hj2, hshift)
    eye_h = (hi2 == hj2).astype(F32)
    sel = lambda a0, a1: jnp.where(head0, a0, a1)
    both = (0, 1)
    fwd = (pl.program_id(0) == 0).astype(F32)
    trow = jnp.bitwise_and(lax.broadcasted_iota(jnp.int32, (G, LANE), 0), C - 1)

    st = []
    for g in range(ngrp):
        rows = slice(g * G, (g + 1) * G)
        lw = lw_ref[rows, :]
        F = lw
        for k in (1 << i for i in range(shift)):
            F = F + jnp.where(trow >= k, pltpu.roll(F, k, 0), 0.0)
        tot = jnp.concatenate([jnp.broadcast_to(F[(c + 1) * C - 1:(c + 1) * C], (C, LANE)) for c in range(G // C)], axis=0)
        P = fwd * F + (1.0 - fwd) * (tot - F + lw)
        eP = jnp.exp(P)
        enP = jnp.exp(-P)
        at = -kk_ref[rows, :] * jnp.exp(P - lw)
        bt = (bb_ref[rows, :] * enP).astype(BF16)
        kt = (kd_ref[rows, :] * enP).astype(BF16)
        rt = r_ref[rows, :] * eP
        vv = v_ref[rows, :].astype(BF16)
        rest = jnp.exp(tot - P)
        bk = jnp.concatenate([bt, kt], axis=0)
        L, lak, grbk = [], [], []
        for h in both:
            hm = head0 if h == 0 else jnp.logical_not(head0)
            ar = jnp.concatenate([jnp.where(hm, at, 0.0), jnp.where(hm, rt, 0.0)], axis=0).astype(BF16)
            gm = lax.dot_general(ar, bk, NT_DIMS, preferred_element_type=F32)
            L.append(jnp.where(strict, gm[:G, :G], 0.0))
            lak.append(jnp.where(strict, gm[:G, G:], 0.0).astype(BF16))
            grbk.append(jnp.where(incl2, gm[G:], 0.0).astype(BF16))
        st.append(dict(at=at.astype(BF16), rt=rt, vv=vv, L=L, lak=lak, grbk=grbk,
                       bkd=[jnp.concatenate([(bb_ref[rows, :] * rest)[c * C:(c + 1) * C],
                                             (kd_ref[rows, :] * rest)[c * C:(c + 1) * C]], axis=0).astype(BF16)
                            for c in range(G // C)],
                       dec=jnp.exp(tot)))

    for s in st:
        s["T"] = [eye + s["L"][h] for h in both]
        lb = [s["L"][h].astype(BF16) for h in both]
        s["Lp"] = [jnp.dot(lb[h], lb[h], preferred_element_type=F32).astype(BF16) for h in both]
    for lvl in range(1, shift):
        for s in st:
            for h in both:
                if lvl < shift - 1:
                    x = jnp.concatenate([s["Lp"][h], s["T"][h].astype(BF16)], axis=1)
                    res = jnp.dot(s["Lp"][h], x, preferred_element_type=F32)
                    s["Lp"][h] = res[:, :G].astype(BF16)
                    s["T"][h] = s["T"][h] + res[:, G:]
                else:
                    s["T"][h] = s["T"][h] + jnp.dot(s["Lp"][h], s["T"][h].astype(BF16), preferred_element_type=F32)
    for s in st:
        s["lv"] = sel(*[jnp.dot(s["lak"][h], s["vv"], preferred_element_type=F32) for h in both]).astype(BF16)
    for s in st:
        x = jnp.concatenate([s["at"], s["lv"]], axis=1)
        tw = [jnp.dot(s["T"][h].astype(BF16), x, preferred_element_type=F32) for h in both]
        s["tw"] = jnp.concatenate([sel(tw[0][:, :LANE], tw[1][:, :LANE]), sel(tw[0][:, LANE:], tw[1][:, LANE:])],
                                  axis=1).astype(BF16)
    for g, s in enumerate(st):
        rows = slice(g * G, (g + 1) * G)
        zv = jnp.concatenate([jnp.zeros((G, LANE), BF16), s["vv"]], axis=1)
        rhs = jnp.concatenate([s["tw"], zv], axis=0)
        qy = [jnp.dot(s["grbk"][h], rhs, preferred_element_type=F32) for h in both]
        Q_o[0, rows, :] = (s["rt"] + sel(qy[0][:, :LANE], qy[1][:, :LANE])).astype(Q_o.dtype)
        Y_o[0, rows, :] = sel(qy[0][:, LANE:], qy[1][:, LANE:])
        for c in range(G // C):
            cr = slice(c * C, (c + 1) * C)
            rc = jnp.concatenate([s["tw"][cr], zv[cr]], axis=0)
            ab = lax.dot_general(s["bkd"][c], rc, TN_DIMS, preferred_element_type=F32)
            k = g * (G // C) + c
            A_o[0, k, 0] = jnp.where(same_head, eye_h * s["dec"][c * C:c * C + 1] + ab[:, :LANE], 0.0).astype(A_o.dtype)
            B_o[0, k, 0] = jnp.where(same_head, ab[:, LANE:], 0.0)


def _scanprep(r, v, kk, kd, lw, bb):
    N, RW = r.shape
    hp = RW // LANE
    G = _tile(N, 2 * CHUNK)
    ngrp = max(n for n in (8, 4, 2, 1) if N % (n * G) == 0)
    rb = G * ngrp
    nchunks = N // CHUNK
    shared = lambda: pl.BlockSpec((rb, LANE), lambda d, i, h: (i, h))
    perdir = lambda: pl.BlockSpec((rb, LANE), lambda d, i, h: (i, d * hp + h))
    rowo = lambda: pl.BlockSpec((1, rb, LANE), lambda d, i, h: (d, i, h))
    mato = lambda: pl.BlockSpec((1, rb // CHUNK, 1, LANE, LANE), lambda d, i, h: (d, i, h, 0, 0))
    return pl.pallas_call(
        functools.partial(_scanprep_kernel, G=G, ngrp=ngrp),
        grid=(2, N // rb, hp),
        in_specs=[shared(), shared(), shared(), perdir(), perdir(), perdir()],
        out_specs=[mato(), mato(), rowo(), rowo()],
        out_shape=[jax.ShapeDtypeStruct((2, nchunks, hp, LANE, LANE), BF16),
                   jax.ShapeDtypeStruct((2, nchunks, hp, LANE, LANE), F32),
                   jax.ShapeDtypeStruct((2, N, RW), BF16),
                   jax.ShapeDtypeStruct((2, N, RW), F32)],
        compiler_params=_params("parallel", "parallel", "parallel"),
        name="rwkv_chunk_maps",
    )(r, v, kk, kd, lw, bb)


def _scanserial_kernel(A_ref, B_ref, Q_ref, Y0_ref, y_ref, s_ref, *, npair, cb):
    @pl.when(pl.program_id(2) == 0)
    def _():
        s_ref[...] = jnp.zeros_like(s_ref)

    C = CHUNK
    d = pl.program_id(0)
    states = [s_ref[h] for h in range(npair)]
    for k in range(cb):
        c = jnp.where(d == 0, k, cb - 1 - k)
        rows = pl.ds(pl.multiple_of(c * C, C), C)
        for h in range(npair):
            ls = slice(h * LANE, (h + 1) * LANE)
            aq = jnp.concatenate([A_ref[0, c, h], Q_ref[0, rows, ls]], axis=0)
            res = jnp.dot(aq, states[h].astype(BF16), preferred_element_type=F32)
            y_ref[0, rows, ls] = res[LANE:] + Y0_ref[0, rows, ls]
            states[h] = res[:LANE] + B_ref[0, c, h]
    for h in range(npair):
        s_ref[h] = states[h]


def _scanserial(A, Bm, Q, Y0, *, B, S, Tc):
    _, nchunks, npair, _, _ = A.shape
    C = CHUNK
    N = nchunks * C
    RW = npair * LANE
    cb = math.gcd(math.gcd(S // C, Tc // C), 4)
    nl, nc = S // (C * cb), Tc // (C * cb)
    cbase = B * nl

    def chunk_of(d, b, p):
        kc = jnp.where(d == 0, p, nc - 1 - p)
        kl = jnp.where(d == 0, p - nc, nl - 1 - (p - nc))
        return jnp.where(p < nc, cbase + b * nc + kc, b * nl + kl)

    mat = lambda: pl.BlockSpec((1, cb, npair, LANE, LANE), lambda d, b, p: (d, chunk_of(d, b, p), 0, 0, 0))
    row = lambda: pl.BlockSpec((1, cb * C, RW), lambda d, b, p: (d, chunk_of(d, b, p), 0))
    return pl.pallas_call(
        functools.partial(_scanserial_kernel, npair=npair, cb=cb),
        grid=(2, B, nl + nc),
        in_specs=[mat(), mat(), row(), row()],
        out_specs=row(),
        out_shape=jax.ShapeDtypeStruct((2, N, RW), F32),
        scratch_shapes=[pltpu.VMEM((npair, LANE, LANE), F32)],
        compiler_params=_params("parallel", "parallel", "arbitrary"),
        name="rwkv_chunk_chain",
    )(A, Bm, Q, Y0)


def _rwout_kernel(y_ref, bon_ref, g_ref, lg_ref, lb_ref, seg_ref, segt_ref, o_ref):
    y = y_ref[0] + y_ref[1]
    inv = 1.0 / RWKV_HEAD_DIM
    mu = _segsum(y, seg_ref, segt_ref) * inv
    yc = y - mu
    var = _segsum(yc * yc, seg_ref, segt_ref) * inv
    yn = yc * lax.rsqrt(var + GN_EPS) * lg_ref[...] + lb_ref[...]
    o_ref[...] = ((yn + bon_ref[...]) * g_ref[...]).astype(o_ref.dtype)


def _rwout(y2, bonus, g, p, rows):
    RW = g.shape[1]
    tm = _tile(rows, 256)
    seg = (jnp.arange(RW)[:, None] // RWKV_HEAD_DIM == jnp.arange(LANE)[None, :]).astype(BF16)
    full = lambda shape: pl.BlockSpec(shape, lambda i: (0,) * len(shape))
    return pl.pallas_call(
        _rwout_kernel,
        grid=(rows // tm,),
        in_specs=[pl.BlockSpec((2, tm, RW), lambda i: (0, i, 0)),
                  pl.BlockSpec((tm, RW), lambda i: (i, 0)),
                  pl.BlockSpec((tm, RW), lambda i: (i, 0)),
                  full((1, RW)), full((1, RW)), full((RW, LANE)), full((LANE, RW))],
        out_specs=pl.BlockSpec((tm, RW), lambda i: (i, 0)),
        out_shape=jax.ShapeDtypeStruct((rows, RW), BF16),
        compiler_params=_params("parallel"),
        name="rwkv_out",
    )(y2, bonus, g, p["rwkv_ln_g"].reshape(1, RW), p["rwkv_ln_b"].reshape(1, RW), seg, seg.T)


CONV_ROWS = 64


def _conv_kernel(first_ref, last_ref, z_ref, zp_ref, zn_ref, w_ref, b_ref, lg_ref, lb_ref, o_ref, u_ref, c_ref, *, CW, K):
    i = pl.program_id(0)
    has_prev = (1 - first_ref[i]).astype(F32)
    has_next = (1 - last_ref[i]).astype(F32)
    tt = z_ref.shape[0]
    H = CONV_HALO
    nlt = CW // LANE
    glu = lambda ref, lt: ref[:, lt * LANE:(lt + 1) * LANE] * _sigmoid(ref[:, CW + lt * LANE:CW + (lt + 1) * LANE])
    span = tt + SUBLANE * ((H + K // 2) // SUBLANE)
    for lt in range(nlt):
        u_ref[0, lt, 0:H, :] = glu(zp_ref, lt) * has_prev
        u_ref[0, lt, H:H + tt, :] = glu(z_ref, lt)
        u_ref[0, lt, H + tt:, :] = glu(zn_ref, lt) * has_next
        for s in range(1, SUBLANE):
            u_ref[s, lt, 0:span, :] = u_ref[0, lt, s:s + span, :]
    off = H - K // 2
    rb = min(CONV_ROWS, tt)
    nrb = tt // rb

    def block(idx, carry):
        lt = idx // nrb
        r0 = pl.multiple_of((idx % nrb) * rb, rb)
        acc = jnp.zeros((rb, LANE), F32) + b_ref[lt]
        for k in range(K):
            s, j = (off + k) % SUBLANE, (off + k) // SUBLANE
            acc = acc + w_ref[k, lt] * u_ref[s, lt, pl.ds(r0 + SUBLANE * j, rb), :]
        c_ref[lt, pl.ds(r0, rb), :] = acc
        return carry

    lax.fori_loop(0, nlt * nrb, block, 0)
    y = _ln(jnp.concatenate([c_ref[lt] for lt in range(nlt)], axis=1)) * lg_ref[...] + lb_ref[...]
    o_ref[...] = (y * _sigmoid(y)).astype(o_ref.dtype)


def _conv(cv, p, *, B, S, Tc, rows):
    CW = cv.shape[1] // 2
    K = p["conv_w"].shape[0]
    assert K // 2 <= CONV_HALO
    nlt = CW // LANE
    tt = _tile(math.gcd(S, Tc), 256)
    hb = tt // CONV_HALO
    nhb = cv.shape[0] // CONV_HALO
    first, last = _seq_edges(B, S, Tc, tt)
    full = lambda shape: pl.BlockSpec(shape, lambda i, f, l: (0,) * len(shape))
    grid_spec = pltpu.PrefetchScalarGridSpec(
        num_scalar_prefetch=2,
        grid=(rows // tt,),
        in_specs=[pl.BlockSpec((tt, 2 * CW), lambda i, f, l: (i, 0)),
                  pl.BlockSpec((CONV_HALO, 2 * CW), lambda i, f, l: (jnp.maximum(i * hb - 1, 0), 0)),
                  pl.BlockSpec((CONV_HALO, 2 * CW), lambda i, f, l: (jnp.minimum((i + 1) * hb, nhb - 1), 0)),
                  full((K, nlt, 1, LANE)), full((nlt, 1, LANE)), full((1, CW)), full((1, CW))],
        out_specs=pl.BlockSpec((tt, CW), lambda i, f, l: (i, 0)),
        scratch_shapes=[pltpu.VMEM((SUBLANE, nlt, tt + 2 * CONV_HALO, LANE), F32),
                        pltpu.VMEM((nlt, tt, LANE), F32)],
    )
    return pl.pallas_call(
        functools.partial(_conv_kernel, CW=CW, K=K),
        grid_spec=grid_spec,
        out_shape=jax.ShapeDtypeStruct((rows, CW), BF16),
        compiler_params=_params("parallel"),
        name="conv_module",
    )(first, last, cv, cv, cv, p["conv_w"].reshape(K, nlt, 1, LANE), p["conv_b"].reshape(nlt, 1, LANE),
      p["conv_ln_g"].reshape(1, CW), p["conv_ln_b"].reshape(1, CW))


def _merge_kernel(a_ref, r_ref, c_ref, wa_ref, wr_ref, wc_ref, g0_ref, g1_ref, g2_ref, b0_ref, b1_ref, b2_ref, o_ref):
    m = _sigmoid(g0_ref[...] + b0_ref[...]) * jnp.dot(a_ref[...], wa_ref[...], preferred_element_type=F32)
    m = m + _sigmoid(g1_ref[...] + b1_ref[...]) * jnp.dot(r_ref[...], wr_ref[...], preferred_element_type=F32)
    m = m + _sigmoid(g2_ref[...] + b2_ref[...]) * jnp.dot(c_ref[...], wc_ref[...], preferred_element_type=F32)
    o_ref[...] = m.astype(o_ref.dtype)


def _merge(att, rwo, cvo, gt, wa, wr, wc, b_gate, rows):
    D = wa.shape[1]
    tm = _tile(rows, 512)
    tn = _tile(D, 512)
    nj = D // tn
    xin = lambda w: pl.BlockSpec((tm, w), lambda j, i: (i, 0))
    win = lambda w: pl.BlockSpec((w, tn), lambda j, i: (0, j))
    gate = lambda k: pl.BlockSpec((tm, tn), lambda j, i: (i, k * nj + j))
    bias = lambda k: pl.BlockSpec((1, tn), lambda j, i: (0, k * nj + j))
    bg = b_gate.reshape(1, 3 * D)
    return pl.pallas_call(
        _merge_kernel,
        grid=(nj, rows // tm),
        in_specs=[xin(att.shape[1]), xin(rwo.shape[1]), xin(cvo.shape[1]),
                  win(wa.shape[0]), win(wr.shape[0]), win(wc.shape[0]),
                  gate(0), gate(1), gate(2), bias(0), bias(1), bias(2)],
        out_specs=pl.BlockSpec((tm, tn), lambda j, i: (i, j)),
        out_shape=jax.ShapeDtypeStruct((rows, D), BF16),
        compiler_params=_params("parallel", "parallel"),
        name="branch_merge",
    )(att, rwo, cvo, wa, wr, wc, gt, gt, gt, bg, bg, bg)


def _outproj_kernel(m_ref, w_ref, xl_ref, xt_ref, gm_ref, sh_ref, sc_ref, lg_ref, lb_ref, wrt_ref, brt_ref,
                    x_o, h_o, lgt_o, *, alpha, nlt):
    y = jnp.dot(m_ref[...], w_ref[...], preferred_element_type=F32)
    x = jnp.where(pl.program_id(0) < nlt, xl_ref[...], xt_ref[...])
    x1 = _ln(alpha * x + gm_ref[0] * y) * lg_ref[...] + lb_ref[...]
    x_o[...] = x1
    h2 = _ln(x1) * (1.0 + sc_ref[0]) + sh_ref[0]
    h_o[...] = h2.astype(h_o.dtype)
    hi = h2.astype(BF16)
    lo = (h2 - hi.astype(F32)).astype(BF16)
    r1 = jnp.dot(hi, wrt_ref[...], preferred_element_type=F32)
    r2 = jnp.dot(lo, wrt_ref[:, :LANE], preferred_element_type=F32)
    lgt_o[...] = r1[:, :LANE] + r1[:, LANE:] + r2 + brt_ref[...]


def _outproj(m, w_out, src, mods3, ln_g, ln_b, w_rt, b_rt, rows, modrow, tm, alpha):
    D = w_out.shape[0]
    full = lambda shape: pl.BlockSpec(shape, lambda i: (0,) * len(shape))
    rowb = lambda w: pl.BlockSpec((tm, w), lambda i: (i, 0))
    mod = lambda blk: pl.BlockSpec((1, 1, D), lambda i: (modrow(i), 0, blk))
    xarrs, xspecs = _stream_specs(src, tm, D)
    return pl.pallas_call(
        functools.partial(_outproj_kernel, alpha=alpha, nlt=src[2]),
        grid=(rows // tm,),
        in_specs=[rowb(D), full((D, D))] + xspecs + [mod(2), mod(3), mod(4), full((1, D)), full((1, D)),
                                                     full((D, 2 * LANE)), full((1, LANE))],
        out_specs=[rowb(D), rowb(D), rowb(LANE)],
        out_shape=[jax.ShapeDtypeStruct((rows, D), F32), jax.ShapeDtypeStruct((rows, D), F32),
                   jax.ShapeDtypeStruct((rows, LANE), F32)],
        compiler_params=_params("parallel"),
        name="out_proj",
    )(m, w_out, *xarrs, mods3, mods3, mods3, ln_g.reshape(1, D), ln_b.reshape(1, D), w_rt, b_rt)


def _route_kernel(l_ref, tri_ref, id_o, w_o, cnt_o, cnt_ref):
    lg = l_ref[...]
    lane = lax.broadcasted_iota(jnp.int32, lg.shape, 1)
    neg = -jnp.inf
    big = jnp.int32(2 ** 30)
    isg = jnp.logical_and(lane >= N_EXPERTS, lane < N_EXPERTS + N_GROUPS)
    gl = jnp.where(isg, lg, neg)
    gmax = jnp.max(gl, -1, keepdims=True)
    grp = jnp.min(jnp.where(gl == gmax, lane - N_EXPERTS, big), -1, keepdims=True)
    grp_w = 1.0 / jnp.sum(jnp.exp(gl - gmax), -1, keepdims=True)
    ing = jnp.logical_and(lane < N_EXPERTS, jnp.right_shift(lane, int(math.log2(EXPERTS_PER_GROUP))) == grp)
    el = jnp.where(ing, lg, neg)
    e1 = jnp.max(el, -1, keepdims=True)
    i1 = jnp.min(jnp.where(el == e1, lane, big), -1, keepdims=True)
    psum = jnp.sum(jnp.exp(el - e1), -1, keepdims=True)
    el2 = jnp.where(lane == i1, neg, el)
    e2 = jnp.max(el2, -1, keepdims=True)
    i2 = jnp.min(jnp.where(el2 == e2, lane, big), -1, keepdims=True)
    p1 = 1.0 / psum
    p2 = jnp.exp(e2 - e1) / psum
    tot = p1 + p2
    w_o[...] = jnp.where(lane == 0, grp_w * p1 / tot, jnp.where(lane == 1, grp_w * p2 / tot, 0.0))

    @pl.when(pl.program_id(0) == 0)
    def _():
        cnt_ref[...] = jnp.zeros_like(cnt_ref)

    hit = jnp.logical_or(lane == i1, lane == i2)
    before = cnt_ref[...] + jnp.dot(tri_ref[...], jnp.where(hit, 1.0, 0.0).astype(BF16), preferred_element_type=F32)
    r1 = jnp.sum(jnp.where(lane == i1, before, 0.0), -1, keepdims=True).astype(jnp.int32)
    r2 = jnp.sum(jnp.where(lane == i2, before, 0.0), -1, keepdims=True).astype(jnp.int32)
    id_o[...] = jnp.where(lane == 0, i1, jnp.where(lane == 1, i2, jnp.where(lane == 2, r1, jnp.where(lane == 3, r2, 0))))
    cnt_ref[...] = cnt_ref[...] + jnp.sum(jnp.where(hit, 1.0, 0.0), axis=0, keepdims=True)
    cnt_o[...] = cnt_ref[...]


def _route(logits):
    n = logits.shape[0]
    tm = _tile(n, 1024)
    tri = (jnp.arange(tm)[None, :] < jnp.arange(tm)[:, None]).astype(BF16)
    blk = lambda: pl.BlockSpec((tm, LANE), lambda i: (i, 0))
    return pl.pallas_call(
        _route_kernel,
        grid=(n // tm,),
        in_specs=[blk(), pl.BlockSpec((tm, tm), lambda i: (0, 0))],
        out_specs=[blk(), blk(), pl.BlockSpec((1, LANE), lambda i: (0, 0))],
        out_shape=[jax.ShapeDtypeStruct((n, LANE), jnp.int32), jax.ShapeDtypeStruct((n, LANE), F32),
                   jax.ShapeDtypeStruct((1, LANE), F32)],
        scratch_shapes=[pltpu.VMEM((1, LANE), F32)],
        compiler_params=_params("arbitrary"),
        name="route",
    )(logits, tri)


def _moe_kernel(be_ref, act_ref, x_ref, wg_ref, wu_ref, wd_ref, o_ref, wgb, wub, wdb):
    i = pl.program_id(0)

    @pl.when(jnp.logical_or(i == 0, be_ref[i] != be_ref[jnp.maximum(i - 1, 0)]))
    def _():
        wgb[...] = wg_ref[0, 0].astype(BF16)
        wub[...] = wu_ref[0, 0].astype(BF16)
        wdb[...] = wd_ref[0, 0].astype(BF16)

    @pl.when(act_ref[i] > 0)
    def _():
        x = x_ref[...].astype(BF16)
        gte = jnp.dot(x, wgb[...], preferred_element_type=F32)
        up = jnp.dot(x, wub[...], preferred_element_type=F32)
        h = (gte * _sigmoid(gte) * up).astype(BF16)
        o_ref[...] = jnp.dot(h, wdb[...], preferred_element_type=F32)

    @pl.when(act_ref[i] == 0)
    def _():
        o_ref[...] = jnp.zeros_like(o_ref)


def _moe_ffn(xg, block_e, active, wg, wu, wd, layer):
    R, D = xg.shape
    FF = wg.shape[3]
    nb = R // MOE_BLOCK
    once = pl.Buffered(1)
    grid_spec = pltpu.PrefetchScalarGridSpec(
        num_scalar_prefetch=2,
        grid=(nb,),
        in_specs=[pl.BlockSpec((MOE_BLOCK, D), lambda i, be, ac: (i, 0)),
                  pl.BlockSpec((1, 1, D, FF), lambda i, be, ac: (layer, be[i], 0, 0), pipeline_mode=once),
                  pl.BlockSpec((1, 1, D, FF), lambda i, be, ac: (layer, be[i], 0, 0), pipeline_mode=once),
                  pl.BlockSpec((1, 1, FF, D), lambda i, be, ac: (layer, be[i], 0, 0), pipeline_mode=once)],
        out_specs=pl.BlockSpec((MOE_BLOCK, D), lambda i, be, ac: (i, 0)),
        scratch_shapes=[pltpu.VMEM((D, FF), BF16), pltpu.VMEM((D, FF), BF16), pltpu.VMEM((FF, D), BF16)],
    )
    return pl.pallas_call(
        _moe_kernel,
        grid_spec=grid_spec,
        out_shape=jax.ShapeDtypeStruct((R, D), F32),
        compiler_params=_params("arbitrary"),
        name="moe_ffn",
    )(block_e, active, xg, wg, wu, wd)


def _dispatch(ids, rank, counts):
    n = ids.shape[0]
    A = 2 * n
    padded = (counts + MOE_BLOCK - 1) // MOE_BLOCK * MOE_BLOCK
    pad_end = jnp.cumsum(padded)
    pad_start = pad_end - padded
    start = jnp.cumsum(counts) - counts
    dest = jnp.take(pad_start, ids, mode="clip") + rank
    nb = -(-A // MOE_BLOCK) + N_EXPERTS
    blk_start = jnp.arange(nb, dtype=jnp.int32) * MOE_BLOCK
    block_e = jnp.minimum(jnp.sum(pad_end[None, :] <= blk_start[:, None], axis=1), N_EXPERTS - 1).astype(jnp.int32)
    active = (blk_start < pad_end[-1]).astype(jnp.int32)
    order = jnp.argsort(ids.reshape(-1))
    slot_e = jnp.repeat(block_e, MOE_BLOCK)
    slot_r = jnp.arange(nb * MOE_BLOCK, dtype=jnp.int32) - jnp.take(pad_start, slot_e)
    src = jnp.clip(jnp.take(start, slot_e) + slot_r, 0, A - 1)
    slot_tok = jnp.where(slot_r < jnp.take(counts, slot_e), jnp.take(order, src) // 2, 0).astype(jnp.int32)
    return slot_tok, dest.astype(jnp.int32), block_e, active


def _final_kernel(x_ref, y0_ref, y1_ref, w_ref, gm_ref, lg_ref, lb_ref, o_ref, *, alpha):
    w = w_ref[...]
    f = w[:, 0:1] * y0_ref[...] + w[:, 1:2] * y1_ref[...]
    o_ref[...] = _ln(alpha * x_ref[...] + gm_ref[0] * f) * lg_ref[...] + lb_ref[...]


def _final(x1, y0, y1, wts, mods3, ln_g, ln_b, rows, modrow, tm, alpha):
    D = x1.shape[1]
    rowb = lambda w: pl.BlockSpec((tm, w), lambda i: (i, 0))
    full = lambda shape: pl.BlockSpec(shape, lambda i: (0,) * len(shape))
    return pl.pallas_call(
        functools.partial(_final_kernel, alpha=alpha),
        grid=(rows // tm,),
        in_specs=[rowb(D), rowb(D), rowb(D), rowb(LANE),
                  pl.BlockSpec((1, 1, D), lambda i: (modrow(i), 0, 5)), full((1, D)), full((1, D))],
        out_specs=rowb(D),
        out_shape=jax.ShapeDtypeStruct((rows, D), F32),
        compiler_params=_params("parallel"),
        name="ffn_residual",
    )(x1, y0, y1, wts, mods3, ln_g.reshape(1, D), ln_b.reshape(1, D))


def _rope_tables(S):
    rows = S // GRID_W
    row = jnp.repeat(jnp.arange(rows, dtype=F32), GRID_W)
    col = jnp.tile(jnp.arange(GRID_W, dtype=F32), rows)
    nf = ATT_HEAD_DIM // 4
    inv = ROPE_THETA ** (-jnp.arange(nf, dtype=F32) / nf)
    ar, ac = row[:, None] * inv, col[:, None] * inv
    cos = jnp.concatenate([jnp.cos(ar), jnp.cos(ar), jnp.cos(ac), jnp.cos(ac)], -1)
    sin = jnp.concatenate([-jnp.sin(ar), jnp.sin(ar), -jnp.sin(ac), jnp.sin(ac)], -1)
    return cos, sin


def kernel(x, c, ctx, c_ctx, w_mod, b_mod, w_in, b_gate, q_norm, k_norm, w_att_o, rwkv_mu, rwkv_w0, rwkv_w2, rwkv_a0, rwkv_a2, rwkv_g2, rwkv_k_k, rwkv_k_a, rwkv_r_k, rwkv_ln_g, rwkv_ln_b, w_rwkv_o, conv_w, conv_b, conv_ln_g, conv_ln_b, w_conv_o, w_out, ln1_g, ln1_b, w_group, b_group, w_router, b_router, w_e_gate, w_e_up, w_e_down, ln2_g, ln2_b):
    B, S, D = x.shape
    Tc = ctx.shape[1]
    depth = w_mod.shape[0]
    NL, NC = B * S, B * Tc
    N = NL + NC
    alpha = (2 * depth) ** 0.25
    att_w = w_att_o.shape[1]
    heads = att_w // ATT_HEAD_DIM
    rw_in = rwkv_mu.shape[-1]
    conv_cw = w_conv_o.shape[1]
    kv_w = (w_in.shape[-1] - att_w - rw_in - 2 * conv_cw - 3 * D) // 2
    assert kv_w == heads // ATT_GROUP * ATT_HEAD_DIM
    o_rw = att_w + 2 * kv_w
    o_cv = o_rw + rw_in
    o_gt = o_cv + 2 * conv_cw

    tmod = _tile(math.gcd(S, Tc), 256)
    tps = S // tmod
    nlt = NL // tmod
    modrow = lambda i: jnp.where(i < nlt, i // tps, B)
    cos_t, sin_t = _rope_tables(S)

    R = -(-(B + 1) // SUBLANE) * SUBLANE
    cv = jnp.zeros((R, D), F32).at[:B].set(c).at[B].set(c_ctx)
    src = (x.reshape(NL, D), ctx.reshape(NC, D), nlt, 0)

    for l in range(depth):
        last = l == depth - 1
        rows = NL if last else N
        p = dict(rwkv_mu=rwkv_mu[l], rwkv_w0=rwkv_w0[l], rwkv_w2=rwkv_w2[l], rwkv_a0=rwkv_a0[l], rwkv_a2=rwkv_a2[l],
                 rwkv_g2=rwkv_g2[l], rwkv_k_k=rwkv_k_k[l], rwkv_k_a=rwkv_k_a[l], rwkv_r_k=rwkv_r_k[l],
                 rwkv_ln_g=rwkv_ln_g[l], rwkv_ln_b=rwkv_ln_b[l], conv_w=conv_w[l], conv_b=conv_b[l],
                 conv_ln_g=conv_ln_g[l], conv_ln_b=conv_ln_b[l])
        mods3 = _adaln(cv, w_mod, b_mod, l).reshape(R, 1, 6 * D)
        wl = w_in[l]
        h = _lnmod(src, mods3, 0, 1, N, modrow, tmod)
        qkv = _matmul(h, wl[:, :o_rw].astype(BF16), 512)
        rw = _matmul(h, wl[:, o_rw:o_cv].astype(BF16), 1152)
        cvx = _matmul(h, wl[:, o_cv:o_gt].astype(BF16), 1024, rows)
        gt = _matmul(h, wl[:, o_gt:].astype(BF16), 1024, rows)

        qn, kn = q_norm[l].reshape(1, LANE), k_norm[l].reshape(1, LANE)
        att = _attention(qkv, qn, kn, cos_t, sin_t, B=B, S=S, Tc=Tc, heads=heads, latent_queries=True)
        if not last:
            att_c = _attention(qkv, qn, kn, cos_t, sin_t, B=B, S=S, Tc=Tc, heads=heads, latent_queries=False)
            att = jnp.concatenate([att, att_c], 0)

        r, v, kk, kd, lw, bb, g, bonus = _rwprep(rw, p, B=B, S=S, Tc=Tc)
        Am, Bm, Qm, Y0 = _scanprep(r, v, kk, kd, lw, bb)
        y2 = _scanserial(Am, Bm, Qm, Y0, B=B, S=S, Tc=Tc)
        rwo = _rwout(y2, bonus, g, p, rows)
        cvo = _conv(cvx, p, B=B, S=S, Tc=Tc, rows=rows)

        m = _merge(att, rwo, cvo, gt, w_att_o[l].astype(BF16), w_rwkv_o[l].astype(BF16), w_conv_o[l].astype(BF16),
                   b_gate[l], rows)
        w_rt = jnp.zeros((D, LANE), F32).at[:, :N_EXPERTS].set(w_router[l]).at[:, N_EXPERTS:N_EXPERTS + N_GROUPS].set(w_group[l])
        b_rt = jnp.zeros((1, LANE), F32).at[0, :N_EXPERTS].set(b_router[l]).at[0, N_EXPERTS:N_EXPERTS + N_GROUPS].set(b_group[l])
        w_rt_hi = w_rt.astype(BF16)
        w_rt = jnp.concatenate([w_rt_hi, (w_rt - w_rt_hi.astype(F32)).astype(BF16)], 1)
        x1, h2, logits = _outproj(m, w_out[l].astype(BF16), src, mods3, ln1_g[l], ln1_b[l], w_rt, b_rt, rows, modrow,
                                  tmod, alpha)

        ids128, wts128, cnt128 = _route(logits)
        slot_tok, dest, block_e, active = _dispatch(ids128[:, :2], ids128[:, 2:4], cnt128[0, :N_EXPERTS].astype(jnp.int32))
        xg = jnp.take(h2, slot_tok, axis=0, mode="clip")
        yb = _moe_ffn(xg, block_e, active, w_e_gate, w_e_up, w_e_down, l)
        y0 = jnp.take(yb, dest[:, 0], axis=0, mode="clip")
        y1 = jnp.take(yb, dest[:, 1], axis=0, mode="clip")
        xa = _final(x1, y0, y1, wts128, mods3, ln2_g[l], ln2_b[l], rows, modrow, tmod, alpha)
        src = (xa, xa, nlt, nlt)

    return xa[:NL].reshape(B, S, D)
```

```python
import functools
import math

import numpy as np
import jax
import jax.numpy as jnp
from jax import lax
from jax.experimental import pallas as pl
from jax.experimental.pallas import tpu as pltpu

F32 = jnp.float32
BF16 = jnp.bfloat16
HI = lax.Precision.HIGHEST

LANE = 128
SUBLANE = 8
ATT_HEAD_DIM = 128
ATT_GROUP = 4
ATT_HEADS_PER_STEP = 2
ROPE_THETA = 10000.0
GRID_W = 64
RWKV_HEAD_DIM = 64
CHUNK = 64
CONV_HALO = 16
SHIFT_HALO = 8
N_GROUPS = 4
EXPERTS_PER_GROUP = 8
N_EXPERTS = N_GROUPS * EXPERTS_PER_GROUP
MOE_BLOCK = 256
GN_EPS = 64e-5
LN_EPS = 1e-6
RMS_EPS = 1e-6
VMEM_LIMIT = 56 * 1024 * 1024

NT_DIMS = (((1,), (1,)), ((), ()))
TN_DIMS = (((0,), (0,)), ((), ()))


def _params(*sem):
    return pltpu.CompilerParams(dimension_semantics=sem, vmem_limit_bytes=VMEM_LIMIT)


def _sigmoid(x):
    return 1.0 / (1.0 + jnp.exp(-x))


def _ln(x):
    mu = jnp.mean(x, -1, keepdims=True)
    xc = x - mu
    var = jnp.mean(xc * xc, -1, keepdims=True)
    return xc * lax.rsqrt(var + LN_EPS)


def _tile(n, pref):
    t = min(n, pref)
    while n % t:
        t -= SUBLANE
    return t


def _adaln_kernel(c_ref, w_ref, b_ref, o_ref):
    c = c_ref[...]
    s = c * _sigmoid(c)
    o_ref[...] = jnp.dot(s, w_ref[0], precision=HI, preferred_element_type=F32) + b_ref[0]


def _adaln(cv, w_mod, b_mod, layer):
    R, D = cv.shape
    L, _, W = w_mod.shape
    tn = _tile(W, 1024)
    return pl.pallas_call(
        _adaln_kernel,
        grid=(W // tn,),
        in_specs=[pl.BlockSpec((R, D), lambda j: (0, 0)),
                  pl.BlockSpec((1, D, tn), lambda j: (layer, 0, j)),
                  pl.BlockSpec((1, 1, tn), lambda j: (layer, 0, j))],
        out_specs=pl.BlockSpec((R, tn), lambda j: (0, j)),
        out_shape=jax.ShapeDtypeStruct((R, W), F32),
        compiler_params=_params("parallel"),
        name="adaln",
    )(cv, w_mod, b_mod.reshape(L, 1, W))


def _stream_specs(src, tm, D):
    lat, tail, nlt, off = src
    return [lat, tail], [pl.BlockSpec((tm, D), lambda i: (jnp.minimum(i, nlt - 1), 0)),
                         pl.BlockSpec((tm, D), lambda i: (jnp.maximum(i - nlt, 0) + off, 0))]


def _lnmod_kernel(xl_ref, xt_ref, sh_ref, sc_ref, o_ref, *, nlt):
    x = jnp.where(pl.program_id(0) < nlt, xl_ref[...], xt_ref[...])
    o_ref[...] = (_ln(x) * (1.0 + sc_ref[0]) + sh_ref[0]).astype(o_ref.dtype)


def _lnmod(src, mods3, shift_blk, scale_blk, rows, modrow, tm):
    D = src[0].shape[1]
    arrs, specs = _stream_specs(src, tm, D)
    return pl.pallas_call(
        functools.partial(_lnmod_kernel, nlt=src[2]),
        grid=(rows // tm,),
        in_specs=specs + [pl.BlockSpec((1, 1, D), lambda i: (modrow(i), 0, shift_blk)),
                          pl.BlockSpec((1, 1, D), lambda i: (modrow(i), 0, scale_blk))],
        out_specs=pl.BlockSpec((tm, D), lambda i: (i, 0)),
        out_shape=jax.ShapeDtypeStruct((rows, D), BF16),
        compiler_params=_params("parallel"),
        name="lnmod",
    )(*arrs, mods3, mods3)


def _mm_kernel(x_ref, w_ref, o_ref):
    o_ref[...] = jnp.dot(x_ref[...], w_ref[...], preferred_element_type=F32).astype(o_ref.dtype)


def _matmul(x, w, tn_pref, rows=None, out_dtype=F32, tm_pref=1024):
    M, K = x.shape
    M = rows or M
    N = w.shape[1]
    tm = _tile(M, tm_pref)
    tn = _tile(N, tn_pref)
    return pl.pallas_call(
        _mm_kernel,
        grid=(N // tn, M // tm),
        in_specs=[pl.BlockSpec((tm, K), lambda j, i: (i, 0)),
                  pl.BlockSpec((K, tn), lambda j, i: (0, j))],
        out_specs=pl.BlockSpec((tm, tn), lambda j, i: (i, j)),
        out_shape=jax.ShapeDtypeStruct((M, N), out_dtype),
        compiler_params=_params("parallel", "parallel"),
        name="matmul",
    )(x, w)


def _rms(t, g):
    return t * lax.rsqrt(jnp.mean(t * t, -1, keepdims=True) + RMS_EPS) * g


def _rope(t, cos, sin_signed):
    lane = lax.broadcasted_iota(jnp.int32, t.shape, 1)
    first = (lane & 63) < 32
    partner = jnp.where(first, pltpu.roll(t, LANE - 32, 1), pltpu.roll(t, 32, 1))
    return t * cos + partner * sin_signed


def _attn_kernel(*refs, rope, has_lat, tq, scale, hg):
    if has_lat:
        (q_ref, kc_ref, vc_ref, qn_ref, kn_ref, kl_ref, vl_ref, cq_ref, sq_ref, ck_ref, sk_ref,
         o_ref, kcb, vcb, klb, vlb) = refs
    else:
        q_ref, kc_ref, vc_ref, qn_ref, kn_ref, o_ref, kcb, vcb = refs

    @pl.when(pl.program_id(2) == 0)
    def _():
        kcb[...] = _rms(kc_ref[...], kn_ref[...]).astype(BF16)
        vcb[:, :LANE] = vc_ref[...].astype(BF16)
        vcb[:, LANE:] = jnp.ones(vc_ref.shape, BF16)
        if has_lat:
            klb[...] = _rope(_rms(kl_ref[...], kn_ref[...]), ck_ref[...], sk_ref[...]).astype(BF16)
            vlb[:, :LANE] = vl_ref[...].astype(BF16)
            vlb[:, LANE:] = jnp.ones(vl_ref.shape, BF16)

    groups = [range(g0, g0 + hg) for g0 in range(0, ATT_GROUP, hg)]
    scores = {}

    def qk(gi):
        qs = []
        for g in groups[gi]:
            qg = _rms(q_ref[:, g * LANE:(g + 1) * LANE], qn_ref[...])
            if rope:
                qg = _rope(qg, cq_ref[...], sq_ref[...])
            qs.append((qg * (scale * math.log2(math.e))).astype(BF16))
        qa = jnp.concatenate(qs, axis=0)
        sc = lax.dot_general(qa, kcb[...], NT_DIMS, preferred_element_type=F32)
        sl = lax.dot_general(qa, klb[...], NT_DIMS, preferred_element_type=F32) if has_lat else None
        scores[gi] = (sc, sl)

    def softmax_pv(gi):
        sc, sl = scores.pop(gi)
        m = jnp.max(sc, -1, keepdims=True)
        if has_lat:
            m = jnp.maximum(m, jnp.max(sl, -1, keepdims=True))
        od = jnp.dot(jnp.exp2(sc - m).astype(BF16), vcb[...], preferred_element_type=F32)
        if has_lat:
            od = od + jnp.dot(jnp.exp2(sl - m).astype(BF16), vlb[...], preferred_element_type=F32)
        o = od[:, :LANE] / od[:, LANE:]
        for n, g in enumerate(groups[gi]):
            o_ref[:, g * LANE:(g + 1) * LANE] = o[n * tq:(n + 1) * tq].astype(o_ref.dtype)

    qk(0)
    for gi in range(1, len(groups)):
        qk(gi)
        softmax_pv(gi - 1)
    softmax_pv(len(groups) - 1)


def _attention(qkv, qn, kn, cos_t, sin_t, *, B, S, Tc, heads, latent_queries):
    NL = B * S
    kvh = heads // ATT_GROUP
    GW = ATT_GROUP * LANE
    scale = ATT_HEAD_DIM ** -0.5
    Tq = S if latent_queries else Tc
    tq = _tile(Tq, 256)
    nq = Tq // tq
    qbase = 0 if latent_queries else NL // tq
    cbase = NL // Tc
    common = [pl.BlockSpec((tq, GW), lambda b, j, i: (qbase + b * nq + i, j)),
              pl.BlockSpec((Tc, LANE), lambda b, j, i: (cbase + b, heads + j)),
              pl.BlockSpec((Tc, LANE), lambda b, j, i: (cbase + b, heads + kvh + j)),
              pl.BlockSpec((1, LANE), lambda b, j, i: (0, 0)),
              pl.BlockSpec((1, LANE), lambda b, j, i: (0, 0))]
    args = [qkv, qkv, qkv, qn, kn]
    scratch = [pltpu.VMEM((Tc, LANE), BF16), pltpu.VMEM((Tc, 2 * LANE), BF16)]
    if latent_queries:
        common += [pl.BlockSpec((S, LANE), lambda b, j, i: (b, heads + j)),
                   pl.BlockSpec((S, LANE), lambda b, j, i: (b, heads + kvh + j)),
                   pl.BlockSpec((tq, LANE), lambda b, j, i: (i, 0)),
                   pl.BlockSpec((tq, LANE), lambda b, j, i: (i, 0)),
                   pl.BlockSpec((S, LANE), lambda b, j, i: (0, 0)),
                   pl.BlockSpec((S, LANE), lambda b, j, i: (0, 0))]
        args += [qkv, qkv, cos_t, sin_t, cos_t, sin_t]
        scratch += [pltpu.VMEM((S, LANE), BF16), pltpu.VMEM((S, 2 * LANE), BF16)]
    return pl.pallas_call(
        functools.partial(_attn_kernel, rope=latent_queries, has_lat=latent_queries, tq=tq, scale=scale, hg=ATT_HEADS_PER_STEP),
        grid=(B, kvh, nq),
        in_specs=common,
        out_specs=pl.BlockSpec((tq, GW), lambda b, j, i: (b * nq + i, j)),
        out_shape=jax.ShapeDtypeStruct((B * Tq, heads * LANE), BF16),
        scratch_shapes=scratch,
        compiler_params=_params("parallel", "parallel", "arbitrary"),
        name="attention_lat" if latent_queries else "attention_ctx",
    )(*args)


def _segsum(x, seg_ref, segt_ref):
    return _split_dot(_split_dot(x, seg_ref[...]), segt_ref[...])


def _split_dot(x, w):
    hi = x.astype(BF16)
    lo = (x - hi.astype(F32)).astype(BF16)
    return jnp.dot(hi, w, preferred_element_type=F32) + jnp.dot(lo, w, preferred_element_type=F32)


def _rwprep_kernel(first_ref, last_ref, z_ref, zp_ref, zn_ref, mu_ref, w0_ref, w2_ref, a0_ref, a2_ref, g2_ref,
                   kkp_ref, ka_ref, rk_ref, seg_ref, segt_ref,
                   r_o, v_o, kk_o, kd_o, lw_o, bb_o, g_o, bon_o, *, RW, R2W, R2A, RG):
    i = pl.program_id(0)
    has_prev = (1 - first_ref[i]).astype(F32)
    has_next = (1 - last_ref[i]).astype(F32)
    tt = z_ref.shape[0]
    row8 = lax.broadcasted_iota(jnp.int32, (SUBLANE, 1), 0)

    def shifted(lo, hi):
        z = z_ref[:, lo:hi]
        hp = zp_ref[SHIFT_HALO - 1:SHIFT_HALO, lo:hi] * has_prev
        hn = zn_ref[0:1, lo:hi] * has_next
        prev = pltpu.roll(z, 1, 0)
        nxt = pltpu.roll(z, tt - 1, 0)
        prev = jnp.concatenate([jnp.where(row8 == 0, hp, prev[:SUBLANE]), prev[SUBLANE:]], axis=0)
        nxt = jnp.concatenate([nxt[:tt - SUBLANE], jnp.where(row8 == SUBLANE - 1, hn, nxt[tt - SUBLANE:])], axis=0)
        m0, m1 = mu_ref[0:1, lo:hi], mu_ref[1:2, lo:hi]
        return (1.0 - m0 - m1) * z + m0 * prev + m1 * nxt

    o = 3 * RW
    r = shifted(0, RW)
    k = shifted(RW, 2 * RW)
    v = shifted(2 * RW, o)
    wl = shifted(o, o + R2W)
    al = shifted(o + R2W, o + R2W + R2A)
    gl = shifted(o + R2W + R2A, o + R2W + R2A + RG)

    u = -(w0_ref[...] + _bdot(jnp.tanh(wl), w2_ref[...]))
    softplus = jnp.maximum(u, 0.0) + jnp.log(1.0 + jnp.exp(-jnp.abs(u)))
    lw_o[...] = -jnp.exp(-softplus - 0.5)
    a = _sigmoid(a0_ref[...] + _bdot(al, a2_ref[...]))
    g_o[...] = _bdot(_sigmoid(gl), g2_ref[...])

    kk = k * kkp_ref[...]
    kk = kk / jnp.maximum(jnp.sqrt(_segsum(kk * kk, seg_ref, segt_ref)), 1e-12)
    ka = ka_ref[...]
    kd0 = k * (1.0 + (a[:, :RW] - 1.0) * ka)
    kd1 = k * (1.0 + (a[:, RW:] - 1.0) * ka)
    r_o[...] = r
    v_o[...] = v
    kk_o[...] = kk
    kd_o[:, :RW] = kd0
    kd_o[:, RW:] = kd1
    bb_o[:, :RW] = kk * a[:, :RW]
    bb_o[:, RW:] = kk * a[:, RW:]
    bon_o[...] = _segsum(r * rk_ref[...] * (kd0 + kd1), seg_ref, segt_ref) * v


def _blockdiag2(w):
    z = jnp.zeros_like(w[0])
    return jnp.concatenate([jnp.concatenate([w[0], z], 1), jnp.concatenate([z, w[1]], 1)], 0)


def _seq_edges(B, S, Tc, tt):
    first, last = [], []
    for n, L in ((B, S), (B, Tc)):
        per = L // tt
        for _ in range(n):
            first += [1] + [0] * (per - 1)
            last += [0] * (per - 1) + [1]
    return jnp.asarray(np.array(first, np.int32)), jnp.asarray(np.array(last, np.int32))


def _rwprep(rw, p, *, B, S, Tc):
    N, RWIN = rw.shape
    RW = p["rwkv_k_k"].shape[0]
    R2W = 2 * p["rwkv_w2"].shape[1]
    R2A = 2 * p["rwkv_a2"].shape[1]
    RG = p["rwkv_g2"].shape[0]
    tt = _tile(math.gcd(S, Tc), 128)
    nt = N // tt
    hb = tt // SHIFT_HALO
    nhb = N // SHIFT_HALO
    first, last = _seq_edges(B, S, Tc, tt)
    nh = RW // RWKV_HEAD_DIM
    seg = (jnp.arange(RW)[:, None] // RWKV_HEAD_DIM == jnp.arange(LANE)[None, :]).astype(BF16)
    full = lambda shape: pl.BlockSpec(shape, lambda i, f, l: (0,) * len(shape))
    row = lambda w: pl.BlockSpec((tt, w), lambda i, f, l: (i, 0))
    out_w = [RW, RW, RW, 2 * RW, 2 * RW, 2 * RW, RW, RW]
    assert nh <= LANE
    grid_spec = pltpu.PrefetchScalarGridSpec(
        num_scalar_prefetch=2,
        grid=(nt,),
        in_specs=[row(RWIN),
                  pl.BlockSpec((SHIFT_HALO, RWIN), lambda i, f, l: (jnp.maximum(i * hb - 1, 0), 0)),
                  pl.BlockSpec((SHIFT_HALO, RWIN), lambda i, f, l: (jnp.minimum((i + 1) * hb, nhb - 1), 0)),
                  full((2, RWIN)), full((1, 2 * RW)), full((R2W, 2 * RW)), full((1, 2 * RW)), full((R2A, 2 * RW)),
                  full((RG, RW)), full((1, RW)), full((1, RW)), full((1, RW)), full((RW, LANE)), full((LANE, RW))],
        out_specs=[row(w) for w in out_w],
    )
    return pl.pallas_call(
        functools.partial(_rwprep_kernel, RW=RW, R2W=R2W, R2A=R2A, RG=RG),
        grid_spec=grid_spec,
        out_shape=[jax.ShapeDtypeStruct((N, w), F32) for w in out_w],
        compiler_params=_params("parallel"),
        name="rwkv_prep",
    )(first, last, rw, rw, rw, p["rwkv_mu"], p["rwkv_w0"].reshape(1, 2 * RW), _blockdiag2(p["rwkv_w2"]),
      p["rwkv_a0"].reshape(1, 2 * RW), _blockdiag2(p["rwkv_a2"]), p["rwkv_g2"],
      p["rwkv_k_k"].reshape(1, RW), p["rwkv_k_a"].reshape(1, RW), p["rwkv_r_k"].reshape(1, RW), seg, seg.T)


def _bdot(a, b):
    return jnp.dot(a.astype(BF16), b.astype(BF16), preferred_element_type=F32)


def _scanprep_kernel(r_ref, v_ref, kk_ref, kd_ref, lw_ref, bb_ref, A_o, B_o, Q_o, Y_o, *, G, ngrp):
    C = CHUNK
    HD = RWKV_HEAD_DIM
    shift = int(math.log2(C))
    hshift = int(math.log2(HD))
    sign = 1 - 2 * pl.program_id(0)
    ti = lax.broadcasted_iota(jnp.int32, (G, G), 0)
    si = lax.broadcasted_iota(jnp.int32, (G, G), 1)
    same = jnp.right_shift(ti, shift) == jnp.right_shift(si, shift)
    strict = jnp.logical_and(same, sign * (ti - si) > 0)
    ti2 = lax.broadcasted_iota(jnp.int32, (G, 2 * G), 0)
    si2 = jnp.bitwise_and(lax.broadcasted_iota(jnp.int32, (G, 2 * G), 1), G - 1)
    incl2 = jnp.logical_and(jnp.right_shift(ti2, shift) == jnp.right_shift(si2, shift), sign * (ti2 - si2) >= 0)
    eye = (si == ti).astype(F32)
    head0 = lax.broadcasted_iota(jnp.int32, (G, LANE), 1) < HD
    hi2 = lax.broadcasted_iota(jnp.int32, (LANE, LANE), 0)
    hj2 = lax.broadcasted_iota(jnp.int32, (LANE, LANE), 1)
    same_head = jnp.right_shift(hi2, hshift) == jnp.right_shift(hj2, hshift)
    eye_h = (hi2 == hj2).astype(F32)
    sel = lambda a0, a1: jnp.where(head0, a0, a1)
    both = (0, 1)
    fwd = (pl.program_id(0) == 0).astype(F32)
    trow = jnp.bitwise_and(lax.broadcasted_iota(jnp.int32, (G, LANE), 0), C - 1)

    st = []
    for g in range(ngrp):
        rows = slice(g * G, (g + 1) * G)
        lw = lw_ref[rows, :]
        F = lw
        for k in (1 << i for i in range(shift)):
            F = F + jnp.where(trow >= k, pltpu.roll(F, k, 0), 0.0)
        tot = jnp.concatenate([jnp.broadcast_to(F[(c + 1) * C - 1:(c + 1) * C], (C, LANE)) for c in range(G // C)], axis=0)
        P = fwd * F + (1.0 - fwd) * (tot - F + lw)
        eP = jnp.exp(P)
        enP = jnp.exp(-P)
        at = -kk_ref[rows, :] * jnp.exp(P - lw)
        bt = (bb_ref[rows, :] * enP).astype(BF16)
        kt = (kd_ref[rows, :] * enP).astype(BF16)
        rt = r_ref[rows, :] * eP
        vv = v_ref[rows, :].astype(BF16)
        rest = jnp.exp(tot - P)
        bk = jnp.concatenate([bt, kt], axis=0)
        L, lak, grbk = [], [], []
        for h in both:
            hm = head0 if h == 0 else jnp.logical_not(head0)
            ar = jnp.concatenate([jnp.where(hm, at, 0.0), jnp.where(hm, rt, 0.0)], axis=0).astype(BF16)
            gm = lax.dot_general(ar, bk, NT_DIMS, preferred_element_type=F32)
            L.append(jnp.where(strict, gm[:G, :G], 0.0))
            lak.append(jnp.where(strict, gm[:G, G:], 0.0).astype(BF16))
            grbk.append(jnp.where(incl2, gm[G:], 0.0).astype(BF16))
        st.append(dict(at=at.astype(BF16), rt=rt, vv=vv, L=L, lak=lak, grbk=grbk,
                       bkd=[jnp.concatenate([(bb_ref[rows, :] * rest)[c * C:(c + 1) * C],
                                             (kd_ref[rows, :] * rest)[c * C:(c + 1) * C]], axis=0).astype(BF16)
                            for c in range(G // C)],
                       dec=jnp.exp(tot)))

    for s in st:
        s["T"] = [eye + s["L"][h] for h in both]
        lb = [s["L"][h].astype(BF16) for h in both]
        s["Lp"] = [jnp.dot(lb[h], lb[h], preferred_element_type=F32).astype(BF16) for h in both]
    for lvl in range(1, shift):
        for s in st:
            for h in both:
                if lvl < shift - 1:
                    x = jnp.concatenate([s["Lp"][h], s["T"][h].astype(BF16)], axis=1)
                    res = jnp.dot(s["Lp"][h], x, preferred_element_type=F32)
                    s["Lp"][h] = res[:, :G].astype(BF16)
                    s["T"][h] = s["T"][h] + res[:, G:]
                else:
                    s["T"][h] = s["T"][h] + jnp.dot(s["Lp"][h], s["T"][h].astype(BF16), preferred_element_type=F32)
    for s in st:
        s["lv"] = sel(*[jnp.dot(s["lak"][h], s["vv"], preferred_element_type=F32) for h in both]).astype(BF16)
    for s in st:
        x = jnp.concatenate([s["at"], s["lv"]], axis=1)
        tw = [jnp.dot(s["T"][h].astype(BF16), x, preferred_element_type=F32) for h in both]
        s["tw"] = jnp.concatenate([sel(tw[0][:, :LANE], tw[1][:, :LANE]), sel(tw[0][:, LANE:], tw[1][:, LANE:])],
                                  axis=1).astype(BF16)
    for g, s in enumerate(st):
        rows = slice(g * G, (g + 1) * G)
        zv = jnp.concatenate([jnp.zeros((G, LANE), BF16), s["vv"]], axis=1)
        rhs = jnp.concatenate([s["tw"], zv], axis=0)
        qy = [jnp.dot(s["grbk"][h], rhs, preferred_element_type=F32) for h in both]
        Q_o[0, rows, :] = (s["rt"] + sel(qy[0][:, :LANE], qy[1][:, :LANE])).astype(Q_o.dtype)
        Y_o[0, rows, :] = sel(qy[0][:, LANE:], qy[1][:, LANE:])
        for c in range(G // C):
            cr = slice(c * C, (c + 1) * C)
            rc = jnp.concatenate([s["tw"][cr], zv[cr]], axis=0)
            ab = lax.dot_general(s["bkd"][c], rc, TN_DIMS, preferred_element_type=F32)
            k = g * (G // C) + c
            A_o[0, k, 0] = jnp.where(same_head, eye_h * s["dec"][c * C:c * C + 1] + ab[:, :LANE], 0.0).astype(A_o.dtype)
            B_o[0, k, 0] = jnp.where(same_head, ab[:, LANE:], 0.0)


def _scanprep(r, v, kk, kd, lw, bb):
    N, RW = r.shape
    hp = RW // LANE
    G = _tile(N, 2 * CHUNK)
    ngrp = max(n for n in (8, 4, 2, 1) if N % (n * G) == 0)
    rb = G * ngrp
    nchunks = N // CHUNK
    shared = lambda: pl.BlockSpec((rb, LANE), lambda d, i, h: (i, h))
    perdir = lambda: pl.BlockSpec((rb, LANE), lambda d, i, h: (i, d * hp + h))
    rowo = lambda: pl.BlockSpec((1, rb, LANE), lambda d, i, h: (d, i, h))
    mato = lambda: pl.BlockSpec((1, rb // CHUNK, 1, LANE, LANE), lambda d, i, h: (d, i, h, 0, 0))
    return pl.pallas_call(
        functools.partial(_scanprep_kernel, G=G, ngrp=ngrp),
        grid=(2, N // rb, hp),
        in_specs=[shared(), shared(), shared(), perdir(), perdir(), perdir()],
        out_specs=[mato(), mato(), rowo(), rowo()],
        out_shape=[jax.ShapeDtypeStruct((2, nchunks, hp, LANE, LANE), BF16),
                   jax.ShapeDtypeStruct((2, nchunks, hp, LANE, LANE), F32),
                   jax.ShapeDtypeStruct((2, N, RW), BF16),
                   jax.ShapeDtypeStruct((2, N, RW), F32)],
        compiler_params=_params("parallel", "parallel", "parallel"),
        name="rwkv_chunk_maps",
    )(r, v, kk, kd, lw, bb)


def _scanserial_kernel(A_ref, B_ref, Q_ref, Y0_ref, y_ref, s_ref, *, npair, cb):
    @pl.when(pl.program_id(2) == 0)
    def _():
        s_ref[...] = jnp.zeros_like(s_ref)

    C = CHUNK
    d = pl.program_id(0)
    states = [s_ref[h] for h in range(npair)]
    for k in range(cb):
        c = jnp.where(d == 0, k, cb - 1 - k)
        rows = pl.ds(pl.multiple_of(c * C, C), C)
        for h in range(npair):
            ls = slice(h * LANE, (h + 1) * LANE)
            aq = jnp.concatenate([A_ref[0, c, h], Q_ref[0, rows, ls]], axis=0)
            res = jnp.dot(aq, states[h].astype(BF16), preferred_element_type=F32)
            y_ref[0, rows, ls] = res[LANE:] + Y0_ref[0, rows, ls]
            states[h] = res[:LANE] + B_ref[0, c, h]
    for h in range(npair):
        s_ref[h] = states[h]


def _scanserial(A, Bm, Q, Y0, *, B, S, Tc):
    _, nchunks, npair, _, _ = A.shape
    C = CHUNK
    N = nchunks * C
    RW = npair * LANE
    cb = math.gcd(math.gcd(S // C, Tc // C), 4)
    nl, nc = S // (C * cb), Tc // (C * cb)
    cbase = B * nl

    def chunk_of(d, b, p):
        kc = jnp.where(d == 0, p, nc - 1 - p)
        kl = jnp.where(d == 0, p - nc, nl - 1 - (p - nc))
        return jnp.where(p < nc, cbase + b * nc + kc, b * nl + kl)

    mat = lambda: pl.BlockSpec((1, cb, npair, LANE, LANE), lambda d, b, p: (d, chunk_of(d, b, p), 0, 0, 0))
    row = lambda: pl.BlockSpec((1, cb * C, RW), lambda d, b, p: (d, chunk_of(d, b, p), 0))
    return pl.pallas_call(
        functools.partial(_scanserial_kernel, npair=npair, cb=cb),
        grid=(2, B, nl + nc),
        in_specs=[mat(), mat(), row(), row()],
        out_specs=row(),
        out_shape=jax.ShapeDtypeStruct((2, N, RW), F32),
        scratch_shapes=[pltpu.VMEM((npair, LANE, LANE), F32)],
        compiler_params=_params("parallel", "parallel", "arbitrary"),
        name="rwkv_chunk_chain",
    )(A, Bm, Q, Y0)


def _rwout_kernel(y_ref, bon_ref, g_ref, lg_ref, lb_ref, seg_ref, segt_ref, o_ref):
    y = y_ref[0] + y_ref[1]
    inv = 1.0 / RWKV_HEAD_DIM
    mu = _segsum(y, seg_ref, segt_ref) * inv
    yc = y - mu
    var = _segsum(yc * yc, seg_ref, segt_ref) * inv
    yn = yc * lax.rsqrt(var + GN_EPS) * lg_ref[...] + lb_ref[...]
    o_ref[...] = ((yn + bon_ref[...]) * g_ref[...]).astype(o_ref.dtype)


def _rwout(y2, bonus, g, p, rows):
    RW = g.shape[1]
    tm = _tile(rows, 256)
    seg = (jnp.arange(RW)[:, None] // RWKV_HEAD_DIM == jnp.arange(LANE)[None, :]).astype(BF16)
    full = lambda shape: pl.BlockSpec(shape, lambda i: (0,) * len(shape))
    return pl.pallas_call(
        _rwout_kernel,
        grid=(rows // tm,),
        in_specs=[pl.BlockSpec((2, tm, RW), lambda i: (0, i, 0)),
                  pl.BlockSpec((tm, RW), lambda i: (i, 0)),
                  pl.BlockSpec((tm, RW), lambda i: (i, 0)),
                  full((1, RW)), full((1, RW)), full((RW, LANE)), full((LANE, RW))],
        out_specs=pl.BlockSpec((tm, RW), lambda i: (i, 0)),
        out_shape=jax.ShapeDtypeStruct((rows, RW), BF16),
        compiler_params=_params("parallel"),
        name="rwkv_out",
    )(y2, bonus, g, p["rwkv_ln_g"].reshape(1, RW), p["rwkv_ln_b"].reshape(1, RW), seg, seg.T)


CONV_ROWS = 64


def _conv_kernel(first_ref, last_ref, z_ref, zp_ref, zn_ref, w_ref, b_ref, lg_ref, lb_ref, o_ref, u_ref, c_ref, *, CW, K):
    i = pl.program_id(0)
    has_prev = (1 - first_ref[i]).astype(F32)
    has_next = (1 - last_ref[i]).astype(F32)
    tt = z_ref.shape[0]
    H = CONV_HALO
    nlt = CW // LANE
    glu = lambda ref, lt: ref[:, lt * LANE:(lt + 1) * LANE] * _sigmoid(ref[:, CW + lt * LANE:CW + (lt + 1) * LANE])
    span = tt + SUBLANE * ((H + K // 2) // SUBLANE)
    for lt in range(nlt):
        u_ref[0, lt, 0:H, :] = glu(zp_ref, lt) * has_prev
        u_ref[0, lt, H:H + tt, :] = glu(z_ref, lt)
        u_ref[0, lt, H + tt:, :] = glu(zn_ref, lt) * has_next
        for s in range(1, SUBLANE):
            u_ref[s, lt, 0:span, :] = u_ref[0, lt, s:s + span, :]
    off = H - K // 2
    rb = min(CONV_ROWS, tt)
    nrb = tt // rb

    def block(idx, carry):
        lt = idx // nrb
        r0 = pl.multiple_of((idx % nrb) * rb, rb)
        acc = jnp.zeros((rb, LANE), F32) + b_ref[lt]
        for k in range(K):
            s, j = (off + k) % SUBLANE, (off + k) // SUBLANE
            acc = acc + w_ref[k, lt] * u_ref[s, lt, pl.ds(r0 + SUBLANE * j, rb), :]
        c_ref[lt, pl.ds(r0, rb), :] = acc
        return carry

    lax.fori_loop(0, nlt * nrb, block, 0)
    y = _ln(jnp.concatenate([c_ref[lt] for lt in range(nlt)], axis=1)) * lg_ref[...] + lb_ref[...]
    o_ref[...] = (y * _sigmoid(y)).astype(o_ref.dtype)


def _conv(cv, p, *, B, S, Tc, rows):
    CW = cv.shape[1] // 2
    K = p["conv_w"].shape[0]
    assert K // 2 <= CONV_HALO
    nlt = CW // LANE
    tt = _tile(math.gcd(S, Tc), 256)
    hb = tt // CONV_HALO
    nhb = cv.shape[0] // CONV_HALO
    first, last = _seq_edges(B, S, Tc, tt)
    full = lambda shape: pl.BlockSpec(shape, lambda i, f, l: (0,) * len(shape))
    grid_spec = pltpu.PrefetchScalarGridSpec(
        num_scalar_prefetch=2,
        grid=(rows // tt,),
        in_specs=[pl.BlockSpec((tt, 2 * CW), lambda i, f, l: (i, 0)),
                  pl.BlockSpec((CONV_HALO, 2 * CW), lambda i, f, l: (jnp.maximum(i * hb - 1, 0), 0)),
                  pl.BlockSpec((CONV_HALO, 2 * CW), lambda i, f, l: (jnp.minimum((i + 1) * hb, nhb - 1), 0)),
                  full((K, nlt, 1, LANE)), full((nlt, 1, LANE)), full((1, CW)), full((1, CW))],
        out_specs=pl.BlockSpec((tt, CW), lambda i, f, l: (i, 0)),
        scratch_shapes=[pltpu.VMEM((SUBLANE, nlt, tt + 2 * CONV_HALO, LANE), F32),
                        pltpu.VMEM((nlt, tt, LANE), F32)],
    )
    return pl.pallas_call(
        functools.partial(_conv_kernel, CW=CW, K=K),
        grid_spec=grid_spec,
        out_shape=jax.ShapeDtypeStruct((rows, CW), BF16),
        compiler_params=_params("parallel"),
        name="conv_module",
    )(first, last, cv, cv, cv, p["conv_w"].reshape(K, nlt, 1, LANE), p["conv_b"].reshape(nlt, 1, LANE),
      p["conv_ln_g"].reshape(1, CW), p["conv_ln_b"].reshape(1, CW))


def _merge_kernel(a_ref, r_ref, c_ref, wa_ref, wr_ref, wc_ref, g0_ref, g1_ref, g2_ref, b0_ref, b1_ref, b2_ref, o_ref):
    m = _sigmoid(g0_ref[...] + b0_ref[...]) * jnp.dot(a_ref[...], wa_ref[...], preferred_element_type=F32)
    m = m + _sigmoid(g1_ref[...] + b1_ref[...]) * jnp.dot(r_ref[...], wr_ref[...], preferred_element_type=F32)
    m = m + _sigmoid(g2_ref[...] + b2_ref[...]) * jnp.dot(c_ref[...], wc_ref[...], preferred_element_type=F32)
    o_ref[...] = m.astype(o_ref.dtype)


def _merge(att, rwo, cvo, gt, wa, wr, wc, b_gate, rows):
    D = wa.shape[1]
    tm = _tile(rows, 512)
    tn = _tile(D, 512)
    nj = D // tn
    xin = lambda w: pl.BlockSpec((tm, w), lambda j, i: (i, 0))
    win = lambda w: pl.BlockSpec((w, tn), lambda j, i: (0, j))
    gate = lambda k: pl.BlockSpec((tm, tn), lambda j, i: (i, k * nj + j))
    bias = lambda k: pl.BlockSpec((1, tn), lambda j, i: (0, k * nj + j))
    bg = b_gate.reshape(1, 3 * D)
    return pl.pallas_call(
        _merge_kernel,
        grid=(nj, rows // tm),
        in_specs=[xin(att.shape[1]), xin(rwo.shape[1]), xin(cvo.shape[1]),
                  win(wa.shape[0]), win(wr.shape[0]), win(wc.shape[0]),
                  gate(0), gate(1), gate(2), bias(0), bias(1), bias(2)],
        out_specs=pl.BlockSpec((tm, tn), lambda j, i: (i, j)),
        out_shape=jax.ShapeDtypeStruct((rows, D), BF16),
        compiler_params=_params("parallel", "parallel"),
        name="branch_merge",
    )(att, rwo, cvo, wa, wr, wc, gt, gt, gt, bg, bg, bg)


def _outproj_kernel(m_ref, w_ref, xl_ref, xt_ref, gm_ref, sh_ref, sc_ref, lg_ref, lb_ref, wrt_ref, brt_ref,
                    x_o, h_o, lgt_o, *, alpha, nlt):
    y = jnp.dot(m_ref[...], w_ref[...], preferred_element_type=F32)
    x = jnp.where(pl.program_id(0) < nlt, xl_ref[...], xt_ref[...])
    x1 = _ln(alpha * x + gm_ref[0] * y) * lg_ref[...] + lb_ref[...]
    x_o[...] = x1
    h2 = _ln(x1) * (1.0 + sc_ref[0]) + sh_ref[0]
    half = h2.shape[1] // 2
    bits = lambda t: pltpu.bitcast(t.astype(BF16).astype(F32), jnp.uint32)
    h_o[...] = bits(h2[:, :half]) | lax.shift_right_logical(bits(h2[:, half:]), jnp.uint32(16))
    hi = h2.astype(BF16)
    lo = (h2 - hi.astype(F32)).astype(BF16)
    r1 = jnp.dot(hi, wrt_ref[...], preferred_element_type=F32)
    r2 = jnp.dot(lo, wrt_ref[:, :LANE], preferred_element_type=F32)
    lgt_o[...] = r1[:, :LANE] + r1[:, LANE:] + r2 + brt_ref[...]


def _outproj(m, w_out, src, mods3, ln_g, ln_b, w_rt, b_rt, rows, modrow, tm, alpha):
    D = w_out.shape[0]
    full = lambda shape: pl.BlockSpec(shape, lambda i: (0,) * len(shape))
    rowb = lambda w: pl.BlockSpec((tm, w), lambda i: (i, 0))
    mod = lambda blk: pl.BlockSpec((1, 1, D), lambda i: (modrow(i), 0, blk))
    xarrs, xspecs = _stream_specs(src, tm, D)
    return pl.pallas_call(
        functools.partial(_outproj_kernel, alpha=alpha, nlt=src[2]),
        grid=(rows // tm,),
        in_specs=[rowb(D), full((D, D))] + xspecs + [mod(2), mod(3), mod(4), full((1, D)), full((1, D)),
                                                     full((D, 2 * LANE)), full((1, LANE))],
        out_specs=[rowb(D), rowb(D // 2), rowb(LANE)],
        out_shape=[jax.ShapeDtypeStruct((rows, D), F32), jax.ShapeDtypeStruct((rows, D // 2), jnp.uint32),
                   jax.ShapeDtypeStruct((rows, LANE), F32)],
        compiler_params=_params("parallel"),
        name="out_proj",
    )(m, w_out, *xarrs, mods3, mods3, mods3, ln_g.reshape(1, D), ln_b.reshape(1, D), w_rt, b_rt)


def _route_kernel(l_ref, tri_ref, id_o, w_o, cnt_o, cnt_ref):
    lg = l_ref[...]
    lane = lax.broadcasted_iota(jnp.int32, lg.shape, 1)
    neg = -jnp.inf
    big = jnp.int32(2 ** 30)
    isg = jnp.logical_and(lane >= N_EXPERTS, lane < N_EXPERTS + N_GROUPS)
    gl = jnp.where(isg, lg, neg)
    gmax = jnp.max(gl, -1, keepdims=True)
    grp = jnp.min(jnp.where(gl == gmax, lane - N_EXPERTS, big), -1, keepdims=True)
    grp_w = 1.0 / jnp.sum(jnp.exp(gl - gmax), -1, keepdims=True)
    ing = jnp.logical_and(lane < N_EXPERTS, jnp.right_shift(lane, int(math.log2(EXPERTS_PER_GROUP))) == grp)
    el = jnp.where(ing, lg, neg)
    e1 = jnp.max(el, -1, keepdims=True)
    i1 = jnp.min(jnp.where(el == e1, lane, big), -1, keepdims=True)
    psum = jnp.sum(jnp.exp(el - e1), -1, keepdims=True)
    el2 = jnp.where(lane == i1, neg, el)
    e2 = jnp.max(el2, -1, keepdims=True)
    i2 = jnp.min(jnp.where(el2 == e2, lane, big), -1, keepdims=True)
    p1 = 1.0 / psum
    p2 = jnp.exp(e2 - e1) / psum
    tot = p1 + p2
    w_o[...] = jnp.where(lane == 0, grp_w * p1 / tot, jnp.where(lane == 1, grp_w * p2 / tot, 0.0))

    @pl.when(pl.program_id(0) == 0)
    def _():
        cnt_ref[...] = jnp.zeros_like(cnt_ref)

    hit = jnp.logical_or(lane == i1, lane == i2)
    before = cnt_ref[...] + jnp.dot(tri_ref[...], jnp.where(hit, 1.0, 0.0).astype(BF16), preferred_element_type=F32)
    r1 = jnp.sum(jnp.where(lane == i1, before, 0.0), -1, keepdims=True).astype(jnp.int32)
    r2 = jnp.sum(jnp.where(lane == i2, before, 0.0), -1, keepdims=True).astype(jnp.int32)
    id_o[...] = jnp.where(lane == 0, i1, jnp.where(lane == 1, i2, jnp.where(lane == 2, r1, jnp.where(lane == 3, r2, 0))))
    cnt_ref[...] = cnt_ref[...] + jnp.sum(jnp.where(hit, 1.0, 0.0), axis=0, keepdims=True)
    cnt_o[...] = cnt_ref[...]


def _route(logits):
    n = logits.shape[0]
    tm = _tile(n, 1024)
    tri = (jnp.arange(tm)[None, :] < jnp.arange(tm)[:, None]).astype(BF16)
    blk = lambda: pl.BlockSpec((tm, LANE), lambda i: (i, 0))
    return pl.pallas_call(
        _route_kernel,
        grid=(n // tm,),
        in_specs=[blk(), pl.BlockSpec((tm, tm), lambda i: (0, 0))],
        out_specs=[blk(), blk(), pl.BlockSpec((1, LANE), lambda i: (0, 0))],
        out_shape=[jax.ShapeDtypeStruct((n, LANE), jnp.int32), jax.ShapeDtypeStruct((n, LANE), F32),
                   jax.ShapeDtypeStruct((1, LANE), F32)],
        scratch_shapes=[pltpu.VMEM((1, LANE), F32)],
        compiler_params=_params("arbitrary"),
        name="route",
    )(logits, tri)


def _row_gather(tok_ref, h_hbm, xbuf, sem, slot, start):
    if not start:
        pltpu.make_async_copy(h_hbm.at[pl.ds(0, MOE_BLOCK)], xbuf.at[slot], sem.at[slot]).wait()
        return

    def body(r, carry):
        pltpu.make_async_copy(h_hbm.at[pl.ds(tok_ref[0, 0, r], 1)], xbuf.at[slot, pl.ds(r, 1)], sem.at[slot]).start()
        return carry

    lax.fori_loop(0, MOE_BLOCK, body, 0, unroll=8)


def _moe_kernel(be_ref, act_ref, cur_ref, nxt_ref, h_hbm, wg_ref, wu_ref, wd_ref, o_ref, wgb, wub, wdb, xbuf, sem):
    i = pl.program_id(0)
    nb = pl.num_programs(0)
    slot = i % 2
    active = act_ref[i] > 0

    @pl.when(jnp.logical_and(i == 0, active))
    def _():
        _row_gather(cur_ref, h_hbm, xbuf, sem, 0, start=True)

    @pl.when(jnp.logical_and(i + 1 < nb, act_ref[jnp.minimum(i + 1, nb - 1)] > 0))
    def _():
        _row_gather(nxt_ref, h_hbm, xbuf, sem, 1 - slot, start=True)

    @pl.when(jnp.logical_or(i == 0, be_ref[i] != be_ref[jnp.maximum(i - 1, 0)]))
    def _():
        wgb[...] = wg_ref[0, 0].astype(BF16)
        wub[...] = wu_ref[0, 0].astype(BF16)
        wdb[...] = wd_ref[0, 0].astype(BF16)

    @pl.when(active)
    def _():
        _row_gather(None, h_hbm, xbuf, sem, slot, start=False)
        w = xbuf[slot]
        xa = pltpu.bitcast(w & jnp.uint32(0xFFFF0000), F32)
        xb = pltpu.bitcast(lax.shift_left(w, jnp.uint32(16)), F32)
        x = jnp.concatenate([xa, xb], axis=1).astype(BF16)
        gte = jnp.dot(x, wgb[...], preferred_element_type=F32)
        up = jnp.dot(x, wub[...], preferred_element_type=F32)
        h = (gte * _sigmoid(gte) * up).astype(BF16)
        o_ref[...] = jnp.dot(h, wdb[...], preferred_element_type=F32)

    @pl.when(jnp.logical_not(active))
    def _():
        o_ref[...] = jnp.zeros_like(o_ref)


def _moe_ffn(hpk, slot_tok, block_e, active, wg, wu, wd, layer):
    D = 2 * hpk.shape[1]
    FF = wg.shape[3]
    nb = slot_tok.shape[0] // MOE_BLOCK
    once = pl.Buffered(1)
    toks = slot_tok.reshape(nb, 1, MOE_BLOCK)
    grid_spec = pltpu.PrefetchScalarGridSpec(
        num_scalar_prefetch=2,
        grid=(nb,),
        in_specs=[pl.BlockSpec((1, 1, MOE_BLOCK), lambda i, be, ac: (i, 0, 0), memory_space=pltpu.SMEM),
                  pl.BlockSpec((1, 1, MOE_BLOCK), lambda i, be, ac: (jnp.minimum(i + 1, nb - 1), 0, 0),
                               memory_space=pltpu.SMEM),
                  pl.BlockSpec(memory_space=pl.ANY),
                  pl.BlockSpec((1, 1, D, FF), lambda i, be, ac: (layer, be[i], 0, 0), pipeline_mode=once),
                  pl.BlockSpec((1, 1, D, FF), lambda i, be, ac: (layer, be[i], 0, 0), pipeline_mode=once),
                  pl.BlockSpec((1, 1, FF, D), lambda i, be, ac: (layer, be[i], 0, 0), pipeline_mode=once)],
        out_specs=pl.BlockSpec((MOE_BLOCK, D), lambda i, be, ac: (i, 0)),
        scratch_shapes=[pltpu.VMEM((D, FF), BF16), pltpu.VMEM((D, FF), BF16), pltpu.VMEM((FF, D), BF16),
                        pltpu.VMEM((2, MOE_BLOCK, D // 2), jnp.uint32), pltpu.SemaphoreType.DMA((2,))],
    )
    return pl.pallas_call(
        _moe_kernel,
        grid_spec=grid_spec,
        out_shape=jax.ShapeDtypeStruct((nb * MOE_BLOCK, D), F32),
        compiler_params=_params("arbitrary"),
        name="moe_ffn",
    )(block_e, active, toks, toks, hpk, wg, wu, wd)


def _dispatch(ids, rank, counts):
    n = ids.shape[0]
    A = 2 * n
    padded = (counts + MOE_BLOCK - 1) // MOE_BLOCK * MOE_BLOCK
    pad_end = jnp.cumsum(padded)
    pad_start = pad_end - padded
    start = jnp.cumsum(counts) - counts
    dest = jnp.take(pad_start, ids, mode="clip") + rank
    nb = -(-A // MOE_BLOCK) + N_EXPERTS
    blk_start = jnp.arange(nb, dtype=jnp.int32) * MOE_BLOCK
    block_e = jnp.minimum(jnp.sum(pad_end[None, :] <= blk_start[:, None], axis=1), N_EXPERTS - 1).astype(jnp.int32)
    active = (blk_start < pad_end[-1]).astype(jnp.int32)
    order = jnp.argsort(ids.reshape(-1))
    slot_e = jnp.repeat(block_e, MOE_BLOCK)
    slot_r = jnp.arange(nb * MOE_BLOCK, dtype=jnp.int32) - jnp.take(pad_start, slot_e)
    src = jnp.clip(jnp.take(start, slot_e) + slot_r, 0, A - 1)
    slot_tok = jnp.where(slot_r < jnp.take(counts, slot_e), jnp.take(order, src) // 2, 0).astype(jnp.int32)
    return slot_tok, dest.astype(jnp.int32), block_e, active


def _final_kernel(x_ref, y0_ref, y1_ref, w_ref, gm_ref, lg_ref, lb_ref, o_ref, *, alpha):
    w = w_ref[...]
    f = w[:, 0:1] * y0_ref[...] + w[:, 1:2] * y1_ref[...]
    o_ref[...] = _ln(alpha * x_ref[...] + gm_ref[0] * f) * lg_ref[...] + lb_ref[...]


def _final(x1, y0, y1, wts, mods3, ln_g, ln_b, rows, modrow, tm, alpha):
    D = x1.shape[1]
    rowb = lambda w: pl.BlockSpec((tm, w), lambda i: (i, 0))
    full = lambda shape: pl.BlockSpec(shape, lambda i: (0,) * len(shape))
    return pl.pallas_call(
        functools.partial(_final_kernel, alpha=alpha),
        grid=(rows // tm,),
        in_specs=[rowb(D), rowb(D), rowb(D), rowb(LANE),
                  pl.BlockSpec((1, 1, D), lambda i: (modrow(i), 0, 5)), full((1, D)), full((1, D))],
        out_specs=rowb(D),
        out_shape=jax.ShapeDtypeStruct((rows, D), F32),
        compiler_params=_params("parallel"),
        name="ffn_residual",
    )(x1, y0, y1, wts, mods3, ln_g.reshape(1, D), ln_b.reshape(1, D))


def _rope_tables(S):
    rows = S // GRID_W
    row = jnp.repeat(jnp.arange(rows, dtype=F32), GRID_W)
    col = jnp.tile(jnp.arange(GRID_W, dtype=F32), rows)
    nf = ATT_HEAD_DIM // 4
    inv = ROPE_THETA ** (-jnp.arange(nf, dtype=F32) / nf)
    ar, ac = row[:, None] * inv, col[:, None] * inv
    cos = jnp.concatenate([jnp.cos(ar), jnp.cos(ar), jnp.cos(ac), jnp.cos(ac)], -1)
    sin = jnp.concatenate([-jnp.sin(ar), jnp.sin(ar), -jnp.sin(ac), jnp.sin(ac)], -1)
    return cos, sin


def kernel(x, c, ctx, c_ctx, w_mod, b_mod, w_in, b_gate, q_norm, k_norm, w_att_o, rwkv_mu, rwkv_w0, rwkv_w2, rwkv_a0, rwkv_a2, rwkv_g2, rwkv_k_k, rwkv_k_a, rwkv_r_k, rwkv_ln_g, rwkv_ln_b, w_rwkv_o, conv_w, conv_b, conv_ln_g, conv_ln_b, w_conv_o, w_out, ln1_g, ln1_b, w_group, b_group, w_router, b_router, w_e_gate, w_e_up, w_e_down, ln2_g, ln2_b):
    B, S, D = x.shape
    Tc = ctx.shape[1]
    depth = w_mod.shape[0]
    NL, NC = B * S, B * Tc
    N = NL + NC
    alpha = (2 * depth) ** 0.25
    att_w = w_att_o.shape[1]
    heads = att_w // ATT_HEAD_DIM
    rw_in = rwkv_mu.shape[-1]
    conv_cw = w_conv_o.shape[1]
    kv_w = (w_in.shape[-1] - att_w - rw_in - 2 * conv_cw - 3 * D) // 2
    assert kv_w == heads // ATT_GROUP * ATT_HEAD_DIM
    o_rw = att_w + 2 * kv_w
    o_cv = o_rw + rw_in
    o_gt = o_cv + 2 * conv_cw

    tmod = _tile(math.gcd(S, Tc), 256)
    tps = S // tmod
    nlt = NL // tmod
    modrow = lambda i: jnp.where(i < nlt, i // tps, B)
    cos_t, sin_t = _rope_tables(S)

    R = -(-(B + 1) // SUBLANE) * SUBLANE
    cv = jnp.zeros((R, D), F32).at[:B].set(c).at[B].set(c_ctx)
    src = (x.reshape(NL, D), ctx.reshape(NC, D), nlt, 0)

    for l in range(depth):
        last = l == depth - 1
        rows = NL if last else N
        p = dict(rwkv_mu=rwkv_mu[l], rwkv_w0=rwkv_w0[l], rwkv_w2=rwkv_w2[l], rwkv_a0=rwkv_a0[l], rwkv_a2=rwkv_a2[l],
                 rwkv_g2=rwkv_g2[l], rwkv_k_k=rwkv_k_k[l], rwkv_k_a=rwkv_k_a[l], rwkv_r_k=rwkv_r_k[l],
                 rwkv_ln_g=rwkv_ln_g[l], rwkv_ln_b=rwkv_ln_b[l], conv_w=conv_w[l], conv_b=conv_b[l],
                 conv_ln_g=conv_ln_g[l], conv_ln_b=conv_ln_b[l])
        mods3 = _adaln(cv, w_mod, b_mod, l).reshape(R, 1, 6 * D)
        wl = w_in[l]
        h = _lnmod(src, mods3, 0, 1, N, modrow, tmod)
        qkv = _matmul(h, wl[:, :o_rw].astype(BF16), 512)
        rw = _matmul(h, wl[:, o_rw:o_cv].astype(BF16), 1152)
        cvx = _matmul(h, wl[:, o_cv:o_gt].astype(BF16), 1024, rows)
        gt = _matmul(h, wl[:, o_gt:].astype(BF16), 1024, rows)

        qn, kn = q_norm[l].reshape(1, LANE), k_norm[l].reshape(1, LANE)
        att = _attention(qkv, qn, kn, cos_t, sin_t, B=B, S=S, Tc=Tc, heads=heads, latent_queries=True)
        if not last:
            att_c = _attention(qkv, qn, kn, cos_t, sin_t, B=B, S=S, Tc=Tc, heads=heads, latent_queries=False)
            att = jnp.concatenate([att, att_c], 0)

        r, v, kk, kd, lw, bb, g, bonus = _rwprep(rw, p, B=B, S=S, Tc=Tc)
        Am, Bm, Qm, Y0 = _scanprep(r, v, kk, kd, lw, bb)
        y2 = _scanserial(Am, Bm, Qm, Y0, B=B, S=S, Tc=Tc)
        rwo = _rwout(y2, bonus, g, p, rows)
        cvo = _conv(cvx, p, B=B, S=S, Tc=Tc, rows=rows)

        m = _merge(att, rwo, cvo, gt, w_att_o[l].astype(BF16), w_rwkv_o[l].astype(BF16), w_conv_o[l].astype(BF16),
                   b_gate[l], rows)
        w_rt = jnp.zeros((D, LANE), F32).at[:, :N_EXPERTS].set(w_router[l]).at[:, N_EXPERTS:N_EXPERTS + N_GROUPS].set(w_group[l])
        b_rt = jnp.zeros((1, LANE), F32).at[0, :N_EXPERTS].set(b_router[l]).at[0, N_EXPERTS:N_EXPERTS + N_GROUPS].set(b_group[l])
        w_rt_hi = w_rt.astype(BF16)
        w_rt = jnp.concatenate([w_rt_hi, (w_rt - w_rt_hi.astype(F32)).astype(BF16)], 1)
        x1, h2, logits = _outproj(m, w_out[l].astype(BF16), src, mods3, ln1_g[l], ln1_b[l], w_rt, b_rt, rows, modrow,
                                  tmod, alpha)

        ids128, wts128, cnt128 = _route(logits)
        slot_tok, dest, block_e, active = _dispatch(ids128[:, :2], ids128[:, 2:4], cnt128[0, :N_EXPERTS].astype(jnp.int32))
        yb = _moe_ffn(h2, slot_tok, block_e, active, w_e_gate, w_e_up, w_e_down, l)
        y0 = jnp.take(yb, dest[:, 0], axis=0, mode="clip")
        y1 = jnp.take(yb, dest[:, 1], axis=0, mode="clip")
        xa = _final(x1, y0, y1, wts128, mods3, ln2_g[l], ln2_b[l], rows, modrow, tmod, alpha)
        src = (xa, xa, nlt, nlt)

    return xa[:NL].reshape(B, S, D)
```

```python
import functools
import math

import numpy as np
import jax
import jax.numpy as jnp
from jax import lax
from jax.experimental import pallas as pl
from jax.experimental.pallas import tpu as pltpu

F32 = jnp.float32
BF16 = jnp.bfloat16
HI = lax.Precision.HIGHEST

LANE = 128
SUBLANE = 8
ATT_HEAD_DIM = 128
ATT_GROUP = 4
ATT_HEADS_PER_STEP = 2
ROPE_THETA = 10000.0
GRID_W = 64
RWKV_HEAD_DIM = 64
CHUNK = 64
CONV_HALO = 16
SHIFT_HALO = 8
N_GROUPS = 4
EXPERTS_PER_GROUP = 8
N_EXPERTS = N_GROUPS * EXPERTS_PER_GROUP
MOE_BLOCK = 256
GN_EPS = 64e-5
LN_EPS = 1e-6
RMS_EPS = 1e-6
VMEM_LIMIT = 56 * 1024 * 1024

NT_DIMS = (((1,), (1,)), ((), ()))
TN_DIMS = (((0,), (0,)), ((), ()))


def _params(*sem):
    return pltpu.CompilerParams(dimension_semantics=sem, vmem_limit_bytes=VMEM_LIMIT)


def _sigmoid(x):
    return 1.0 / (1.0 + jnp.exp(-x))


def _ln(x):
    mu = jnp.mean(x, -1, keepdims=True)
    xc = x - mu
    var = jnp.mean(xc * xc, -1, keepdims=True)
    return xc * lax.rsqrt(var + LN_EPS)


def _tile(n, pref):
    t = min(n, pref)
    while n % t:
        t -= SUBLANE
    return t


def _adaln_kernel(c_ref, w_ref, b_ref, o_ref):
    c = c_ref[...]
    s = c * _sigmoid(c)
    o_ref[...] = jnp.dot(s, w_ref[0], precision=HI, preferred_element_type=F32) + b_ref[0]


def _adaln(cv, w_mod, b_mod, layer):
    R, D = cv.shape
    L, _, W = w_mod.shape
    tn = _tile(W, 1024)
    return pl.pallas_call(
        _adaln_kernel,
        grid=(W // tn,),
        in_specs=[pl.BlockSpec((R, D), lambda j: (0, 0)),
                  pl.BlockSpec((1, D, tn), lambda j: (layer, 0, j)),
                  pl.BlockSpec((1, 1, tn), lambda j: (layer, 0, j))],
        out_specs=pl.BlockSpec((R, tn), lambda j: (0, j)),
        out_shape=jax.ShapeDtypeStruct((R, W), F32),
        compiler_params=_params("parallel"),
        name="adaln",
    )(cv, w_mod, b_mod.reshape(L, 1, W))


def _stream_specs(src, tm, D):
    lat, tail, nlt, off = src
    return [lat, tail], [pl.BlockSpec((tm, D), lambda i: (jnp.minimum(i, nlt - 1), 0)),
                         pl.BlockSpec((tm, D), lambda i: (jnp.maximum(i - nlt, 0) + off, 0))]


def _lnmod_kernel(xl_ref, xt_ref, sh_ref, sc_ref, o_ref, *, nlt):
    x = jnp.where(pl.program_id(0) < nlt, xl_ref[...], xt_ref[...])
    o_ref[...] = (_ln(x) * (1.0 + sc_ref[0]) + sh_ref[0]).astype(o_ref.dtype)


def _lnmod(src, mods3, shift_blk, scale_blk, rows, modrow, tm):
    D = src[0].shape[1]
    arrs, specs = _stream_specs(src, tm, D)
    return pl.pallas_call(
        functools.partial(_lnmod_kernel, nlt=src[2]),
        grid=(rows // tm,),
        in_specs=specs + [pl.BlockSpec((1, 1, D), lambda i: (modrow(i), 0, shift_blk)),
                          pl.BlockSpec((1, 1, D), lambda i: (modrow(i), 0, scale_blk))],
        out_specs=pl.BlockSpec((tm, D), lambda i: (i, 0)),
        out_shape=jax.ShapeDtypeStruct((rows, D), BF16),
        compiler_params=_params("parallel"),
        name="lnmod",
    )(*arrs, mods3, mods3)


def _mm_kernel(x_ref, w_ref, o_ref):
    o_ref[...] = jnp.dot(x_ref[...], w_ref[...], preferred_element_type=F32).astype(o_ref.dtype)


def _matmul(x, w, tn_pref, rows=None, out_dtype=F32, tm_pref=1024):
    M, K = x.shape
    M = rows or M
    N = w.shape[1]
    tm = _tile(M, tm_pref)
    tn = _tile(N, tn_pref)
    return pl.pallas_call(
        _mm_kernel,
        grid=(N // tn, M // tm),
        in_specs=[pl.BlockSpec((tm, K), lambda j, i: (i, 0)),
                  pl.BlockSpec((K, tn), lambda j, i: (0, j))],
        out_specs=pl.BlockSpec((tm, tn), lambda j, i: (i, j)),
        out_shape=jax.ShapeDtypeStruct((M, N), out_dtype),
        compiler_params=_params("parallel", "parallel"),
        name="matmul",
    )(x, w)


def _rms(t, g):
    return t * lax.rsqrt(jnp.mean(t * t, -1, keepdims=True) + RMS_EPS) * g


def _rope(t, cos, sin_signed):
    lane = lax.broadcasted_iota(jnp.int32, t.shape, 1)
    first = (lane & 63) < 32
    partner = jnp.where(first, pltpu.roll(t, LANE - 32, 1), pltpu.roll(t, 32, 1))
    return t * cos + partner * sin_signed


def _attn_kernel(*refs, rope, has_lat, tq, scale, hg):
    if has_lat:
        (q_ref, kc_ref, vc_ref, qn_ref, kn_ref, kl_ref, vl_ref, cq_ref, sq_ref, ck_ref, sk_ref,
         o_ref, kcb, vcb, klb, vlb) = refs
    else:
        q_ref, kc_ref, vc_ref, qn_ref, kn_ref, o_ref, kcb, vcb = refs

    @pl.when(pl.program_id(2) == 0)
    def _():
        kcb[...] = _rms(kc_ref[...], kn_ref[...]).astype(BF16)
        vcb[:, :LANE] = vc_ref[...].astype(BF16)
        vcb[:, LANE:] = jnp.ones(vc_ref.shape, BF16)
        if has_lat:
            klb[...] = _rope(_rms(kl_ref[...], kn_ref[...]), ck_ref[...], sk_ref[...]).astype(BF16)
            vlb[:, :LANE] = vl_ref[...].astype(BF16)
            vlb[:, LANE:] = jnp.ones(vl_ref.shape, BF16)

    groups = [range(g0, g0 + hg) for g0 in range(0, ATT_GROUP, hg)]
    scores = {}

    def qk(gi):
        qs = []
        for g in groups[gi]:
            qg = _rms(q_ref[:, g * LANE:(g + 1) * LANE], qn_ref[...])
            if rope:
                qg = _rope(qg, cq_ref[...], sq_ref[...])
            qs.append((qg * (scale * math.log2(math.e))).astype(BF16))
        qa = jnp.concatenate(qs, axis=0)
        sc = lax.dot_general(qa, kcb[...], NT_DIMS, preferred_element_type=F32)
        sl = lax.dot_general(qa, klb[...], NT_DIMS, preferred_element_type=F32) if has_lat else None
        scores[gi] = (sc, sl)

    def softmax_pv(gi):
        sc, sl = scores.pop(gi)
        m = jnp.max(sc, -1, keepdims=True)
        if has_lat:
            m = jnp.maximum(m, jnp.max(sl, -1, keepdims=True))
        od = jnp.dot(jnp.exp2(sc - m).astype(BF16), vcb[...], preferred_element_type=F32)
        if has_lat:
            od = od + jnp.dot(jnp.exp2(sl - m).astype(BF16), vlb[...], preferred_element_type=F32)
        o = od[:, :LANE] / od[:, LANE:]
        for n, g in enumerate(groups[gi]):
            o_ref[:, g * LANE:(g + 1) * LANE] = o[n * tq:(n + 1) * tq].astype(o_ref.dtype)

    qk(0)
    for gi in range(1, len(groups)):
        qk(gi)
        softmax_pv(gi - 1)
    softmax_pv(len(groups) - 1)


def _attention(qkv, qn, kn, cos_t, sin_t, *, B, S, Tc, heads, latent_queries):
    NL = B * S
    kvh = heads // ATT_GROUP
    GW = ATT_GROUP * LANE
    scale = ATT_HEAD_DIM ** -0.5
    Tq = S if latent_queries else Tc
    tq = _tile(Tq, 256)
    nq = Tq // tq
    qbase = 0 if latent_queries else NL // tq
    cbase = NL // Tc
    common = [pl.BlockSpec((tq, GW), lambda b, j, i: (qbase + b * nq + i, j)),
              pl.BlockSpec((Tc, LANE), lambda b, j, i: (cbase + b, heads + j)),
              pl.BlockSpec((Tc, LANE), lambda b, j, i: (cbase + b, heads + kvh + j)),
              pl.BlockSpec((1, LANE), lambda b, j, i: (0, 0)),
              pl.BlockSpec((1, LANE), lambda b, j, i: (0, 0))]
    args = [qkv, qkv, qkv, qn, kn]
    scratch = [pltpu.VMEM((Tc, LANE), BF16), pltpu.VMEM((Tc, 2 * LANE), BF16)]
    if latent_queries:
        common += [pl.BlockSpec((S, LANE), lambda b, j, i: (b, heads + j)),
                   pl.BlockSpec((S, LANE), lambda b, j, i: (b, heads + kvh + j)),
                   pl.BlockSpec((tq, LANE), lambda b, j, i: (i, 0)),
                   pl.BlockSpec((tq, LANE), lambda b, j, i: (i, 0)),
                   pl.BlockSpec((S, LANE), lambda b, j, i: (0, 0)),
                   pl.BlockSpec((S, LANE), lambda b, j, i: (0, 0))]
        args += [qkv, qkv, cos_t, sin_t, cos_t, sin_t]
        scratch += [pltpu.VMEM((S, LANE), BF16), pltpu.VMEM((S, 2 * LANE), BF16)]
    return pl.pallas_call(
        functools.partial(_attn_kernel, rope=latent_queries, has_lat=latent_queries, tq=tq, scale=scale, hg=ATT_HEADS_PER_STEP),
        grid=(B, kvh, nq),
        in_specs=common,
        out_specs=pl.BlockSpec((tq, GW), lambda b, j, i: (b * nq + i, j)),
        out_shape=jax.ShapeDtypeStruct((B * Tq, heads * LANE), BF16),
        scratch_shapes=scratch,
        compiler_params=_params("parallel", "parallel", "arbitrary"),
        name="attention_lat" if latent_queries else "attention_ctx",
    )(*args)


def _segsum(x, seg_ref, segt_ref):
    return _split_dot(_split_dot(x, seg_ref[...]), segt_ref[...])


def _split_dot(x, w):
    hi = x.astype(BF16)
    lo = (x - hi.astype(F32)).astype(BF16)
    return jnp.dot(hi, w, preferred_element_type=F32) + jnp.dot(lo, w, preferred_element_type=F32)


def _rwprep_kernel(first_ref, last_ref, z_ref, zp_ref, zn_ref, mu_ref, w0_ref, w2_ref, a0_ref, a2_ref, g2_ref,
                   kkp_ref, ka_ref, rk_ref, seg_ref, segt_ref,
                   r_o, v_o, kk_o, kd_o, lw_o, bb_o, g_o, bon_o, *, RW, R2W, R2A, RG):
    i = pl.program_id(0)
    has_prev = (1 - first_ref[i]).astype(F32)
    has_next = (1 - last_ref[i]).astype(F32)
    tt = z_ref.shape[0]
    row8 = lax.broadcasted_iota(jnp.int32, (SUBLANE, 1), 0)

    def shifted(lo, hi):
        z = z_ref[:, lo:hi]
        hp = zp_ref[SHIFT_HALO - 1:SHIFT_HALO, lo:hi] * has_prev
        hn = zn_ref[0:1, lo:hi] * has_next
        prev = pltpu.roll(z, 1, 0)
        nxt = pltpu.roll(z, tt - 1, 0)
        prev = jnp.concatenate([jnp.where(row8 == 0, hp, prev[:SUBLANE]), prev[SUBLANE:]], axis=0)
        nxt = jnp.concatenate([nxt[:tt - SUBLANE], jnp.where(row8 == SUBLANE - 1, hn, nxt[tt - SUBLANE:])], axis=0)
        m0, m1 = mu_ref[0:1, lo:hi], mu_ref[1:2, lo:hi]
        return (1.0 - m0 - m1) * z + m0 * prev + m1 * nxt

    o = 3 * RW
    r = shifted(0, RW)
    k = shifted(RW, 2 * RW)
    v = shifted(2 * RW, o)
    wl = shifted(o, o + R2W)
    al = shifted(o + R2W, o + R2W + R2A)
    gl = shifted(o + R2W + R2A, o + R2W + R2A + RG)

    u = -(w0_ref[...] + _bdot(jnp.tanh(wl), w2_ref[...]))
    softplus = jnp.maximum(u, 0.0) + jnp.log(1.0 + jnp.exp(-jnp.abs(u)))
    lw_o[...] = -jnp.exp(-softplus - 0.5)
    a = _sigmoid(a0_ref[...] + _bdot(al, a2_ref[...]))
    g_o[...] = _bdot(_sigmoid(gl), g2_ref[...])

    kk = k * kkp_ref[...]
    kk = kk / jnp.maximum(jnp.sqrt(_segsum(kk * kk, seg_ref, segt_ref)), 1e-12)
    ka = ka_ref[...]
    kd0 = k * (1.0 + (a[:, :RW] - 1.0) * ka)
    kd1 = k * (1.0 + (a[:, RW:] - 1.0) * ka)
    r_o[...] = r
    v_o[...] = v
    kk_o[...] = kk
    kd_o[:, :RW] = kd0
    kd_o[:, RW:] = kd1
    bb_o[:, :RW] = kk * a[:, :RW]
    bb_o[:, RW:] = kk * a[:, RW:]
    bon_o[...] = _segsum(r * rk_ref[...] * (kd0 + kd1), seg_ref, segt_ref) * v


def _blockdiag2(w):
    z = jnp.zeros_like(w[0])
    return jnp.concatenate([jnp.concatenate([w[0], z], 1), jnp.concatenate([z, w[1]], 1)], 0)


def _seq_edges(B, S, Tc, tt):
    first, last = [], []
    for n, L in ((B, S), (B, Tc)):
        per = L // tt
        for _ in range(n):
            first += [1] + [0] * (per - 1)
            last += [0] * (per - 1) + [1]
    return jnp.asarray(np.array(first, np.int32)), jnp.asarray(np.array(last, np.int32))


def _rwprep(rw, p, *, B, S, Tc):
    N, RWIN = rw.shape
    RW = p["rwkv_k_k"].shape[0]
    R2W = 2 * p["rwkv_w2"].shape[1]
    R2A = 2 * p["rwkv_a2"].shape[1]
    RG = p["rwkv_g2"].shape[0]
    tt = _tile(math.gcd(S, Tc), 128)
    nt = N // tt
    hb = tt // SHIFT_HALO
    nhb = N // SHIFT_HALO
    first, last = _seq_edges(B, S, Tc, tt)
    nh = RW // RWKV_HEAD_DIM
    seg = (jnp.arange(RW)[:, None] // RWKV_HEAD_DIM == jnp.arange(LANE)[None, :]).astype(BF16)
    full = lambda shape: pl.BlockSpec(shape, lambda i, f, l: (0,) * len(shape))
    row = lambda w: pl.BlockSpec((tt, w), lambda i, f, l: (i, 0))
    out_w = [RW, RW, RW, 2 * RW, 2 * RW, 2 * RW, RW, RW]
    assert nh <= LANE
    grid_spec = pltpu.PrefetchScalarGridSpec(
        num_scalar_prefetch=2,
        grid=(nt,),
        in_specs=[row(RWIN),
                  pl.BlockSpec((SHIFT_HALO, RWIN), lambda i, f, l: (jnp.maximum(i * hb - 1, 0), 0)),
                  pl.BlockSpec((SHIFT_HALO, RWIN), lambda i, f, l: (jnp.minimum((i + 1) * hb, nhb - 1), 0)),
                  full((2, RWIN)), full((1, 2 * RW)), full((R2W, 2 * RW)), full((1, 2 * RW)), full((R2A, 2 * RW)),
                  full((RG, RW)), full((1, RW)), full((1, RW)), full((1, RW)), full((RW, LANE)), full((LANE, RW))],
        out_specs=[row(w) for w in out_w],
    )
    return pl.pallas_call(
        functools.partial(_rwprep_kernel, RW=RW, R2W=R2W, R2A=R2A, RG=RG),
        grid_spec=grid_spec,
        out_shape=[jax.ShapeDtypeStruct((N, w), F32) for w in out_w],
        compiler_params=_params("parallel"),
        name="rwkv_prep",
    )(first, last, rw, rw, rw, p["rwkv_mu"], p["rwkv_w0"].reshape(1, 2 * RW), _blockdiag2(p["rwkv_w2"]),
      p["rwkv_a0"].reshape(1, 2 * RW), _blockdiag2(p["rwkv_a2"]), p["rwkv_g2"],
      p["rwkv_k_k"].reshape(1, RW), p["rwkv_k_a"].reshape(1, RW), p["rwkv_r_k"].reshape(1, RW), seg, seg.T)


def _bdot(a, b):
    return jnp.dot(a.astype(BF16), b.astype(BF16), preferred_element_type=F32)


def _scanprep_kernel(r_ref, v_ref, kk_ref, kd_ref, lw_ref, bb_ref, A_o, B_o, Q_o, Y_o, *, G, ngrp):
    C = CHUNK
    HD = RWKV_HEAD_DIM
    shift = int(math.log2(C))
    hshift = int(math.log2(HD))
    sign = 1 - 2 * pl.program_id(0)
    ti = lax.broadcasted_iota(jnp.int32, (G, G), 0)
    si = lax.broadcasted_iota(jnp.int32, (G, G), 1)
    same = jnp.right_shift(ti, shift) == jnp.right_shift(si, shift)
    strict = jnp.logical_and(same, sign * (ti - si) > 0)
    ti2 = lax.broadcasted_iota(jnp.int32, (G, 2 * G), 0)
    si2 = jnp.bitwise_and(lax.broadcasted_iota(jnp.int32, (G, 2 * G), 1), G - 1)
    incl2 = jnp.logical_and(jnp.right_shift(ti2, shift) == jnp.right_shift(si2, shift), sign * (ti2 - si2) >= 0)
    eye = (si == ti).astype(F32)
    head0 = lax.broadcasted_iota(jnp.int32, (G, LANE), 1) < HD
    hi2 = lax.broadcasted_iota(jnp.int32, (LANE, LANE), 0)
    hj2 = lax.broadcasted_iota(jnp.int32, (LANE, LANE), 1)
    same_head = jnp.right_shift(hi2, hshift) == jnp.right_shift(hj2, hshift)
    eye_h = (hi2 == hj2).astype(F32)
    sel = lambda a0, a1: jnp.where(head0, a0, a1)
    both = (0, 1)
    fwd = (pl.program_id(0) == 0).astype(F32)
    trow = jnp.bitwise_and(lax.broadcasted_iota(jnp.int32, (G, LANE), 0), C - 1)

    st = []
    for g in range(ngrp):
        rows = slice(g * G, (g + 1) * G)
        lw = lw_ref[rows, :]
        F = lw
        for k in (1 << i for i in range(shift)):
            F = F + jnp.where(trow >= k, pltpu.roll(F, k, 0), 0.0)
        tot = jnp.concatenate([jnp.broadcast_to(F[(c + 1) * C - 1:(c + 1) * C], (C, LANE)) for c in range(G // C)], axis=0)
        P = fwd * F + (1.0 - fwd) * (tot - F + lw)
        eP = jnp.exp(P)
        enP = jnp.exp(-P)
        at = -kk_ref[rows, :] * jnp.exp(P - lw)
        bt = (bb_ref[rows, :] * enP).astype(BF16)
        kt = (kd_ref[rows, :] * enP).astype(BF16)
        rt = r_ref[rows, :] * eP
        vv = v_ref[rows, :].astype(BF16)
        rest = jnp.exp(tot - P)
        bk = jnp.concatenate([bt, kt], axis=0)
        L, lak, grbk = [], [], []
        for h in both:
            hm = head0 if h == 0 else jnp.logical_not(head0)
            ar = jnp.concatenate([jnp.where(hm, at, 0.0), jnp.where(hm, rt, 0.0)], axis=0).astype(BF16)
            gm = lax.dot_general(ar, bk, NT_DIMS, preferred_element_type=F32)
            L.append(jnp.where(strict, gm[:G, :G], 0.0))
            lak.append(jnp.where(strict, gm[:G, G:], 0.0).astype(BF16))
            grbk.append(jnp.where(incl2, gm[G:], 0.0).astype(BF16))
        st.append(dict(at=at.astype(BF16), rt=rt, vv=vv, L=L, lak=lak, grbk=grbk,
                       bkd=[jnp.concatenate([(bb_ref[rows, :] * rest)[c * C:(c + 1) * C],
                                             (kd_ref[rows, :] * rest)[c * C:(c + 1) * C]], axis=0).astype(BF16)
                            for c in range(G // C)],
                       dec=jnp.exp(tot)))

    for s in st:
        s["T"] = [eye + s["L"][h] for h in both]
        lb = [s["L"][h].astype(BF16) for h in both]
        s["Lp"] = [jnp.dot(lb[h], lb[h], preferred_element_type=F32).astype(BF16) for h in both]
    for lvl in range(1, shift):
        for s in st:
            for h in both:
                if lvl < shift - 1:
                    x = jnp.concatenate([s["Lp"][h], s["T"][h].astype(BF16)], axis=1)
                    res = jnp.dot(s["Lp"][h], x, preferred_element_type=F32)
                    s["Lp"][h] = res[:, :G].astype(BF16)
                    s["T"][h] = s["T"][h] + res[:, G:]
                else:
                    s["T"][h] = s["T"][h] + jnp.dot(s["Lp"][h], s["T"][h].astype(BF16), preferred_element_type=F32)
    for s in st:
        s["lv"] = sel(*[jnp.dot(s["lak"][h], s["vv"], preferred_element_type=F32) for h in both]).astype(BF16)
    for s in st:
        x = jnp.concatenate([s["at"], s["lv"]], axis=1)
        tw = [jnp.dot(s["T"][h].astype(BF16), x, preferred_element_type=F32) for h in both]
        s["tw"] = jnp.concatenate([sel(tw[0][:, :LANE], tw[1][:, :LANE]), sel(tw[0][:, LANE:], tw[1][:, LANE:])],
                                  axis=1).astype(BF16)
    for g, s in enumerate(st):
        rows = slice(g * G, (g + 1) * G)
        zv = jnp.concatenate([jnp.zeros((G, LANE), BF16), s["vv"]], axis=1)
        rhs = jnp.concatenate([s["tw"], zv], axis=0)
        qy = [jnp.dot(s["grbk"][h], rhs, preferred_element_type=F32) for h in both]
        Q_o[0, rows, :] = (s["rt"] + sel(qy[0][:, :LANE], qy[1][:, :LANE])).astype(Q_o.dtype)
        Y_o[0, rows, :] = sel(qy[0][:, LANE:], qy[1][:, LANE:])
        for c in range(G // C):
            cr = slice(c * C, (c + 1) * C)
            rc = jnp.concatenate([s["tw"][cr], zv[cr]], axis=0)
            ab = lax.dot_general(s["bkd"][c], rc, TN_DIMS, preferred_element_type=F32)
            k = g * (G // C) + c
            A_o[0, k, 0] = jnp.where(same_head, eye_h * s["dec"][c * C:c * C + 1] + ab[:, :LANE], 0.0).astype(A_o.dtype)
            B_o[0, k, 0] = jnp.where(same_head, ab[:, LANE:], 0.0)


def _scanprep(r, v, kk, kd, lw, bb):
    N, RW = r.shape
    hp = RW // LANE
    G = _tile(N, 2 * CHUNK)
    ngrp = max(n for n in (8, 4, 2, 1) if N % (n * G) == 0)
    rb = G * ngrp
    nchunks = N // CHUNK
    shared = lambda: pl.BlockSpec((rb, LANE), lambda d, i, h: (i, h))
    perdir = lambda: pl.BlockSpec((rb, LANE), lambda d, i, h: (i, d * hp + h))
    rowo = lambda: pl.BlockSpec((1, rb, LANE), lambda d, i, h: (d, i, h))
    mato = lambda: pl.BlockSpec((1, rb // CHUNK, 1, LANE, LANE), lambda d, i, h: (d, i, h, 0, 0))
    return pl.pallas_call(
        functools.partial(_scanprep_kernel, G=G, ngrp=ngrp),
        grid=(2, N // rb, hp),
        in_specs=[shared(), shared(), shared(), perdir(), perdir(), perdir()],
        out_specs=[mato(), mato(), rowo(), rowo()],
        out_shape=[jax.ShapeDtypeStruct((2, nchunks, hp, LANE, LANE), BF16),
                   jax.ShapeDtypeStruct((2, nchunks, hp, LANE, LANE), F32),
                   jax.ShapeDtypeStruct((2, N, RW), BF16),
                   jax.ShapeDtypeStruct((2, N, RW), F32)],
        compiler_params=_params("parallel", "parallel", "parallel"),
        name="rwkv_chunk_maps",
    )(r, v, kk, kd, lw, bb)


def _scanserial_kernel(A_ref, B_ref, Q_ref, Y0_ref, y_ref, s_ref, *, npair, cb):
    @pl.when(pl.program_id(2) == 0)
    def _():
        s_ref[...] = jnp.zeros_like(s_ref)

    C = CHUNK
    d = pl.program_id(0)
    states = [s_ref[h] for h in range(npair)]
    for k in range(cb):
        c = jnp.where(d == 0, k, cb - 1 - k)
        rows = pl.ds(pl.multiple_of(c * C, C), C)
        for h in range(npair):
            ls = slice(h * LANE, (h + 1) * LANE)
            aq = jnp.concatenate([A_ref[0, c, h], Q_ref[0, rows, ls]], axis=0)
            res = jnp.dot(aq, states[h].astype(BF16), preferred_element_type=F32)
            y_ref[0, rows, ls] = res[LANE:] + Y0_ref[0, rows, ls]
            states[h] = res[:LANE] + B_ref[0, c, h]
    for h in range(npair):
        s_ref[h] = states[h]


def _scanserial(A, Bm, Q, Y0, *, B, S, Tc):
    _, nchunks, npair, _, _ = A.shape
    C = CHUNK
    N = nchunks * C
    RW = npair * LANE
    cb = math.gcd(math.gcd(S // C, Tc // C), 4)
    nl, nc = S // (C * cb), Tc // (C * cb)
    cbase = B * nl

    def chunk_of(d, b, p):
        kc = jnp.where(d == 0, p, nc - 1 - p)
        kl = jnp.where(d == 0, p - nc, nl - 1 - (p - nc))
        return jnp.where(p < nc, cbase + b * nc + kc, b * nl + kl)

    mat = lambda: pl.BlockSpec((1, cb, npair, LANE, LANE), lambda d, b, p: (d, chunk_of(d, b, p), 0, 0, 0))
    row = lambda: pl.BlockSpec((1, cb * C, RW), lambda d, b, p: (d, chunk_of(d, b, p), 0))
    return pl.pallas_call(
        functools.partial(_scanserial_kernel, npair=npair, cb=cb),
        grid=(2, B, nl + nc),
        in_specs=[mat(), mat(), row(), row()],
        out_specs=row(),
        out_shape=jax.ShapeDtypeStruct((2, N, RW), F32),
        scratch_shapes=[pltpu.VMEM((npair, LANE, LANE), F32)],
        compiler_params=_params("parallel", "parallel", "arbitrary"),
        name="rwkv_chunk_chain",
    )(A, Bm, Q, Y0)


def _rwout_kernel(y_ref, bon_ref, g_ref, lg_ref, lb_ref, seg_ref, segt_ref, o_ref):
    y = y_ref[0] + y_ref[1]
    inv = 1.0 / RWKV_HEAD_DIM
    mu = _segsum(y, seg_ref, segt_ref) * inv
    yc = y - mu
    var = _segsum(yc * yc, seg_ref, segt_ref) * inv
    yn = yc * lax.rsqrt(var + GN_EPS) * lg_ref[...] + lb_ref[...]
    o_ref[...] = ((yn + bon_ref[...]) * g_ref[...]).astype(o_ref.dtype)


def _rwout(y2, bonus, g, p, rows):
    RW = g.shape[1]
    tm = _tile(rows, 256)
    seg = (jnp.arange(RW)[:, None] // RWKV_HEAD_DIM == jnp.arange(LANE)[None, :]).astype(BF16)
    full = lambda shape: pl.BlockSpec(shape, lambda i: (0,) * len(shape))
    return pl.pallas_call(
        _rwout_kernel,
        grid=(rows // tm,),
        in_specs=[pl.BlockSpec((2, tm, RW), lambda i: (0, i, 0)),
                  pl.BlockSpec((tm, RW), lambda i: (i, 0)),
                  pl.BlockSpec((tm, RW), lambda i: (i, 0)),
                  full((1, RW)), full((1, RW)), full((RW, LANE)), full((LANE, RW))],
        out_specs=pl.BlockSpec((tm, RW), lambda i: (i, 0)),
        out_shape=jax.ShapeDtypeStruct((rows, RW), BF16),
        compiler_params=_params("parallel"),
        name="rwkv_out",
    )(y2, bonus, g, p["rwkv_ln_g"].reshape(1, RW), p["rwkv_ln_b"].reshape(1, RW), seg, seg.T)


CONV_ROWS = 64


def _conv_kernel(first_ref, last_ref, z_ref, zp_ref, zn_ref, w_ref, b_ref, lg_ref, lb_ref, o_ref, u_ref, c_ref, *, CW, K):
    i = pl.program_id(0)
    has_prev = (1 - first_ref[i]).astype(F32)
    has_next = (1 - last_ref[i]).astype(F32)
    tt = z_ref.shape[0]
    H = CONV_HALO
    nlt = CW // LANE
    glu = lambda ref, lt: ref[:, lt * LANE:(lt + 1) * LANE] * _sigmoid(ref[:, CW + lt * LANE:CW + (lt + 1) * LANE])
    span = tt + SUBLANE * ((H + K // 2) // SUBLANE)
    for lt in range(nlt):
        u_ref[0, lt, 0:H, :] = glu(zp_ref, lt) * has_prev
        u_ref[0, lt, H:H + tt, :] = glu(z_ref, lt)
        u_ref[0, lt, H + tt:, :] = glu(zn_ref, lt) * has_next
        for s in range(1, SUBLANE):
            u_ref[s, lt, 0:span, :] = u_ref[0, lt, s:s + span, :]
    off = H - K // 2
    rb = min(CONV_ROWS, tt)
    nrb = tt // rb

    def block(idx, carry):
        lt = idx // nrb
        r0 = pl.multiple_of((idx % nrb) * rb, rb)
        acc = jnp.zeros((rb, LANE), F32) + b_ref[lt]
        for k in range(K):
            s, j = (off + k) % SUBLANE, (off + k) // SUBLANE
            acc = acc + w_ref[k, lt] * u_ref[s, lt, pl.ds(r0 + SUBLANE * j, rb), :]
        c_ref[lt, pl.ds(r0, rb), :] = acc
        return carry

    lax.fori_loop(0, nlt * nrb, block, 0)
    y = _ln(jnp.concatenate([c_ref[lt] for lt in range(nlt)], axis=1)) * lg_ref[...] + lb_ref[...]
    o_ref[...] = (y * _sigmoid(y)).astype(o_ref.dtype)


def _conv(cv, p, *, B, S, Tc, rows):
    CW = cv.shape[1] // 2
    K = p["conv_w"].shape[0]
    assert K // 2 <= CONV_HALO
    nlt = CW // LANE
    tt = _tile(math.gcd(S, Tc), 256)
    hb = tt // CONV_HALO
    nhb = cv.shape[0] // CONV_HALO
    first, last = _seq_edges(B, S, Tc, tt)
    full = lambda shape: pl.BlockSpec(shape, lambda i, f, l: (0,) * len(shape))
    grid_spec = pltpu.PrefetchScalarGridSpec(
        num_scalar_prefetch=2,
        grid=(rows // tt,),
        in_specs=[pl.BlockSpec((tt, 2 * CW), lambda i, f, l: (i, 0)),
                  pl.BlockSpec((CONV_HALO, 2 * CW), lambda i, f, l: (jnp.maximum(i * hb - 1, 0), 0)),
                  pl.BlockSpec((CONV_HALO, 2 * CW), lambda i, f, l: (jnp.minimum((i + 1) * hb, nhb - 1), 0)),
                  full((K, nlt, 1, LANE)), full((nlt, 1, LANE)), full((1, CW)), full((1, CW))],
        out_specs=pl.BlockSpec((tt, CW), lambda i, f, l: (i, 0)),
        scratch_shapes=[pltpu.VMEM((SUBLANE, nlt, tt + 2 * CONV_HALO, LANE), F32),
                        pltpu.VMEM((nlt, tt, LANE), F32)],
    )
    return pl.pallas_call(
        functools.partial(_conv_kernel, CW=CW, K=K),
        grid_spec=grid_spec,
        out_shape=jax.ShapeDtypeStruct((rows, CW), BF16),
        compiler_params=_params("parallel"),
        name="conv_module",
    )(first, last, cv, cv, cv, p["conv_w"].reshape(K, nlt, 1, LANE), p["conv_b"].reshape(nlt, 1, LANE),
      p["conv_ln_g"].reshape(1, CW), p["conv_ln_b"].reshape(1, CW))


def _merge_kernel(a_ref, r_ref, c_ref, wa_ref, wr_ref, wc_ref, g0_ref, g1_ref, g2_ref, b0_ref, b1_ref, b2_ref, o_ref):
    m = _sigmoid(g0_ref[...] + b0_ref[...]) * jnp.dot(a_ref[...], wa_ref[...], preferred_element_type=F32)
    m = m + _sigmoid(g1_ref[...] + b1_ref[...]) * jnp.dot(r_ref[...], wr_ref[...], preferred_element_type=F32)
    m = m + _sigmoid(g2_ref[...] + b2_ref[...]) * jnp.dot(c_ref[...], wc_ref[...], preferred_element_type=F32)
    o_ref[...] = m.astype(o_ref.dtype)


def _merge(att, rwo, cvo, gt, wa, wr, wc, b_gate, rows):
    D = wa.shape[1]
    tm = _tile(rows, 512)
    tn = _tile(D, 512)
    nj = D // tn
    xin = lambda w: pl.BlockSpec((tm, w), lambda j, i: (i, 0))
    win = lambda w: pl.BlockSpec((w, tn), lambda j, i: (0, j))
    gate = lambda k: pl.BlockSpec((tm, tn), lambda j, i: (i, k * nj + j))
    bias = lambda k: pl.BlockSpec((1, tn), lambda j, i: (0, k * nj + j))
    bg = b_gate.reshape(1, 3 * D)
    return pl.pallas_call(
        _merge_kernel,
        grid=(nj, rows // tm),
        in_specs=[xin(att.shape[1]), xin(rwo.shape[1]), xin(cvo.shape[1]),
                  win(wa.shape[0]), win(wr.shape[0]), win(wc.shape[0]),
                  gate(0), gate(1), gate(2), bias(0), bias(1), bias(2)],
        out_specs=pl.BlockSpec((tm, tn), lambda j, i: (i, j)),
        out_shape=jax.ShapeDtypeStruct((rows, D), BF16),
        compiler_params=_params("parallel", "parallel"),
        name="branch_merge",
    )(att, rwo, cvo, wa, wr, wc, gt, gt, gt, bg, bg, bg)


def _outproj_kernel(m_ref, w_ref, xl_ref, xt_ref, gm_ref, sh_ref, sc_ref, lg_ref, lb_ref, wrt_ref, brt_ref,
                    x_o, h_o, lgt_o, *, alpha, nlt):
    y = jnp.dot(m_ref[...], w_ref[...], preferred_element_type=F32)
    x = jnp.where(pl.program_id(0) < nlt, xl_ref[...], xt_ref[...])
    x1 = _ln(alpha * x + gm_ref[0] * y) * lg_ref[...] + lb_ref[...]
    x_o[...] = x1
    h2 = _ln(x1) * (1.0 + sc_ref[0]) + sh_ref[0]
    half = h2.shape[1] // 2
    bits = lambda t: pltpu.bitcast(t.astype(BF16).astype(F32), jnp.uint32)
    h_o[...] = bits(h2[:, :half]) | lax.shift_right_logical(bits(h2[:, half:]), jnp.uint32(16))
    hi = h2.astype(BF16)
    lo = (h2 - hi.astype(F32)).astype(BF16)
    r1 = jnp.dot(hi, wrt_ref[...], preferred_element_type=F32)
    r2 = jnp.dot(lo, wrt_ref[:, :LANE], preferred_element_type=F32)
    lgt_o[...] = r1[:, :LANE] + r1[:, LANE:] + r2 + brt_ref[...]


def _outproj(m, w_out, src, mods3, ln_g, ln_b, w_rt, b_rt, rows, modrow, tm, alpha):
    D = w_out.shape[0]
    full = lambda shape: pl.BlockSpec(shape, lambda i: (0,) * len(shape))
    rowb = lambda w: pl.BlockSpec((tm, w), lambda i: (i, 0))
    mod = lambda blk: pl.BlockSpec((1, 1, D), lambda i: (modrow(i), 0, blk))
    xarrs, xspecs = _stream_specs(src, tm, D)
    return pl.pallas_call(
        functools.partial(_outproj_kernel, alpha=alpha, nlt=src[2]),
        grid=(rows // tm,),
        in_specs=[rowb(D), full((D, D))] + xspecs + [mod(2), mod(3), mod(4), full((1, D)), full((1, D)),
                                                     full((D, 2 * LANE)), full((1, LANE))],
        out_specs=[rowb(D), rowb(D // 2), rowb(LANE)],
        out_shape=[jax.ShapeDtypeStruct((rows, D), F32), jax.ShapeDtypeStruct((rows, D // 2), jnp.uint32),
                   jax.ShapeDtypeStruct((rows, LANE), F32)],
        compiler_params=_params("parallel"),
        name="out_proj",
    )(m, w_out, *xarrs, mods3, mods3, mods3, ln_g.reshape(1, D), ln_b.reshape(1, D), w_rt, b_rt)


def _route_kernel(l_ref, tri_ref, id_o, w_o, cnt_o, cnt_ref):
    lg = l_ref[...]
    lane = lax.broadcasted_iota(jnp.int32, lg.shape, 1)
    neg = -jnp.inf
    big = jnp.int32(2 ** 30)
    isg = jnp.logical_and(lane >= N_EXPERTS, lane < N_EXPERTS + N_GROUPS)
    gl = jnp.where(isg, lg, neg)
    gmax = jnp.max(gl, -1, keepdims=True)
    grp = jnp.min(jnp.where(gl == gmax, lane - N_EXPERTS, big), -1, keepdims=True)
    grp_w = 1.0 / jnp.sum(jnp.exp(gl - gmax), -1, keepdims=True)
    ing = jnp.logical_and(lane < N_EXPERTS, jnp.right_shift(lane, int(math.log2(EXPERTS_PER_GROUP))) == grp)
    el = jnp.where(ing, lg, neg)
    e1 = jnp.max(el, -1, keepdims=True)
    i1 = jnp.min(jnp.where(el == e1, lane, big), -1, keepdims=True)
    psum = jnp.sum(jnp.exp(el - e1), -1, keepdims=True)
    el2 = jnp.where(lane == i1, neg, el)
    e2 = jnp.max(el2, -1, keepdims=True)
    i2 = jnp.min(jnp.where(el2 == e2, lane, big), -1, keepdims=True)
    p1 = 1.0 / psum
    p2 = jnp.exp(e2 - e1) / psum
    tot = p1 + p2
    w_o[...] = jnp.where(lane == 0, grp_w * p1 / tot, jnp.where(lane == 1, grp_w * p2 / tot, 0.0))

    @pl.when(pl.program_id(0) == 0)
    def _():
        cnt_ref[...] = jnp.zeros_like(cnt_ref)

    hit = jnp.logical_or(lane == i1, lane == i2)
    before = cnt_ref[...] + jnp.dot(tri_ref[...], jnp.where(hit, 1.0, 0.0).astype(BF16), preferred_element_type=F32)
    r1 = jnp.sum(jnp.where(lane == i1, before, 0.0), -1, keepdims=True).astype(jnp.int32)
    r2 = jnp.sum(jnp.where(lane == i2, before, 0.0), -1, keepdims=True).astype(jnp.int32)
    id_o[...] = jnp.where(lane == 0, i1, jnp.where(lane == 1, i2, jnp.where(lane == 2, r1, jnp.where(lane == 3, r2, 0))))
    cnt_ref[...] = cnt_ref[...] + jnp.sum(jnp.where(hit, 1.0, 0.0), axis=0, keepdims=True)
    cnt_o[...] = cnt_ref[...]


def _route(logits):
    n = logits.shape[0]
    tm = _tile(n, 1024)
    tri = (jnp.arange(tm)[None, :] < jnp.arange(tm)[:, None]).astype(BF16)
    blk = lambda: pl.BlockSpec((tm, LANE), lambda i: (i, 0))
    return pl.pallas_call(
        _route_kernel,
        grid=(n // tm,),
        in_specs=[blk(), pl.BlockSpec((tm, tm), lambda i: (0, 0))],
        out_specs=[blk(), blk(), pl.BlockSpec((1, LANE), lambda i: (0, 0))],
        out_shape=[jax.ShapeDtypeStruct((n, LANE), jnp.int32), jax.ShapeDtypeStruct((n, LANE), F32),
                   jax.ShapeDtypeStruct((1, LANE), F32)],
        scratch_shapes=[pltpu.VMEM((1, LANE), F32)],
        compiler_params=_params("arbitrary"),
        name="route",
    )(logits, tri)


def _row_gather(tok_ref, h_hbm, xbuf, sem, slot, start, unrolled=False):
    if not start:
        pltpu.make_async_copy(h_hbm.at[pl.ds(0, MOE_BLOCK)], xbuf.at[slot], sem.at[slot]).wait()
        return

    def body(r, carry):
        pltpu.make_async_copy(h_hbm.at[pl.ds(tok_ref[0, 0, r], 1)], xbuf.at[slot, pl.ds(r, 1)], sem.at[slot]).start()
        return carry

    if unrolled:
        for r in range(MOE_BLOCK):
            body(r, 0)
    else:
        lax.fori_loop(0, MOE_BLOCK, body, 0, unroll=8)


def _moe_kernel(be_ref, act_ref, cur_ref, nxt_ref, h_hbm, wg_ref, wu_ref, wd_ref, o_ref, wgb, wub, wdb, xbuf, sem):
    i = pl.program_id(0)
    nb = pl.num_programs(0)
    slot = i % 2
    active = act_ref[i] > 0

    @pl.when(jnp.logical_and(i == 0, active))
    def _():
        _row_gather(cur_ref, h_hbm, xbuf, sem, 0, start=True)

    @pl.when(jnp.logical_or(i == 0, be_ref[i] != be_ref[jnp.maximum(i - 1, 0)]))
    def _():
        wgb[...] = wg_ref[0, 0].astype(BF16)
        wub[...] = wu_ref[0, 0].astype(BF16)
        wdb[...] = wd_ref[0, 0].astype(BF16)

    @pl.when(active)
    def _():
        _row_gather(None, h_hbm, xbuf, sem, slot, start=False)
        _row_gather(nxt_ref, h_hbm, xbuf, sem, 1 - slot, start=True, unrolled=True)
        w = xbuf[slot]
        xa = pltpu.bitcast(w & jnp.uint32(0xFFFF0000), F32)
        xb = pltpu.bitcast(lax.shift_left(w, jnp.uint32(16)), F32)
        x = jnp.concatenate([xa, xb], axis=1).astype(BF16)
        gte = jnp.dot(x, wgb[...], preferred_element_type=F32)
        up = jnp.dot(x, wub[...], preferred_element_type=F32)
        h = (gte * _sigmoid(gte) * up).astype(BF16)
        o_ref[...] = jnp.dot(h, wdb[...], preferred_element_type=F32)

    @pl.when(jnp.logical_not(active))
    def _():
        @pl.when(jnp.logical_and(i > 0, act_ref[jnp.maximum(i - 1, 0)] > 0))
        def _():
            _row_gather(None, h_hbm, xbuf, sem, slot, start=False)

        o_ref[...] = jnp.zeros_like(o_ref)


def _moe_ffn(hpk, slot_tok, block_e, active, wg, wu, wd, layer):
    D = 2 * hpk.shape[1]
    FF = wg.shape[3]
    nb = slot_tok.shape[0] // MOE_BLOCK
    once = pl.Buffered(1)
    toks = slot_tok.reshape(nb, 1, MOE_BLOCK)
    grid_spec = pltpu.PrefetchScalarGridSpec(
        num_scalar_prefetch=2,
        grid=(nb,),
        in_specs=[pl.BlockSpec((1, 1, MOE_BLOCK), lambda i, be, ac: (i, 0, 0), memory_space=pltpu.SMEM),
                  pl.BlockSpec((1, 1, MOE_BLOCK), lambda i, be, ac: (jnp.minimum(i + 1, nb - 1), 0, 0),
                               memory_space=pltpu.SMEM),
                  pl.BlockSpec(memory_space=pl.ANY),
                  pl.BlockSpec((1, 1, D, FF), lambda i, be, ac: (layer, be[i], 0, 0), pipeline_mode=once),
                  pl.BlockSpec((1, 1, D, FF), lambda i, be, ac: (layer, be[i], 0, 0), pipeline_mode=once),
                  pl.BlockSpec((1, 1, FF, D), lambda i, be, ac: (layer, be[i], 0, 0), pipeline_mode=once)],
        out_specs=pl.BlockSpec((MOE_BLOCK, D), lambda i, be, ac: (i, 0)),
        scratch_shapes=[pltpu.VMEM((D, FF), BF16), pltpu.VMEM((D, FF), BF16), pltpu.VMEM((FF, D), BF16),
                        pltpu.VMEM((2, MOE_BLOCK, D // 2), jnp.uint32), pltpu.SemaphoreType.DMA((2,))],
    )
    return pl.pallas_call(
        _moe_kernel,
        grid_spec=grid_spec,
        out_shape=jax.ShapeDtypeStruct((nb * MOE_BLOCK, D), F32),
        compiler_params=_params("arbitrary"),
        name="moe_ffn",
    )(block_e, active, toks, toks, hpk, wg, wu, wd)


def _dispatch(ids, rank, counts):
    n = ids.shape[0]
    A = 2 * n
    padded = (counts + MOE_BLOCK - 1) // MOE_BLOCK * MOE_BLOCK
    pad_end = jnp.cumsum(padded)
    pad_start = pad_end - padded
    start = jnp.cumsum(counts) - counts
    dest = jnp.take(pad_start, ids, mode="clip") + rank
    nb = -(-A // MOE_BLOCK) + N_EXPERTS + 1
    blk_start = jnp.arange(nb, dtype=jnp.int32) * MOE_BLOCK
    block_e = jnp.minimum(jnp.sum(pad_end[None, :] <= blk_start[:, None], axis=1), N_EXPERTS - 1).astype(jnp.int32)
    active = (blk_start < pad_end[-1]).astype(jnp.int32)
    order = jnp.argsort(ids.reshape(-1))
    slot_e = jnp.repeat(block_e, MOE_BLOCK)
    slot_r = jnp.arange(nb * MOE_BLOCK, dtype=jnp.int32) - jnp.take(pad_start, slot_e)
    src = jnp.clip(jnp.take(start, slot_e) + slot_r, 0, A - 1)
    slot_tok = jnp.where(slot_r < jnp.take(counts, slot_e), jnp.take(order, src) // 2, 0).astype(jnp.int32)
    return slot_tok, dest.astype(jnp.int32), block_e, active


def _final_kernel(cur_ref, nxt_ref, x_ref, yb_hbm, w_ref, gm_ref, lg_ref, lb_ref, o_ref, ybuf, sem, *, alpha, tm):
    i = pl.program_id(0)
    slot = i % 2

    def gather(tok_ref, s, unrolled):
        def body(r, carry):
            for k in range(2):
                pltpu.make_async_copy(yb_hbm.at[pl.ds(tok_ref[0, 0, k * tm + r], 1)], ybuf.at[s, k, pl.ds(r, 1)],
                                      sem.at[s]).start()
            return carry

        if unrolled:
            for r in range(tm):
                body(r, 0)
        else:
            lax.fori_loop(0, tm, body, 0, unroll=8)

    @pl.when(i == 0)
    def _():
        gather(cur_ref, 0, False)

    @pl.when(i + 1 < pl.num_programs(0))
    def _():
        gather(nxt_ref, 1 - slot, True)

    for k in range(2):
        pltpu.make_async_copy(yb_hbm.at[pl.ds(0, tm)], ybuf.at[slot, k], sem.at[slot]).wait()
    w = w_ref[...]
    f = w[:, 0:1] * ybuf[slot, 0] + w[:, 1:2] * ybuf[slot, 1]
    o_ref[...] = _ln(alpha * x_ref[...] + gm_ref[0] * f) * lg_ref[...] + lb_ref[...]


def _final(x1, yb, dest, wts, mods3, ln_g, ln_b, rows, modrow, tm, alpha):
    D = x1.shape[1]
    nt = rows // tm
    toks = jnp.swapaxes(dest.reshape(nt, tm, 2), 1, 2).reshape(nt, 1, 2 * tm)
    rowb = lambda w: pl.BlockSpec((tm, w), lambda i: (i, 0))
    full = lambda shape: pl.BlockSpec(shape, lambda i: (0,) * len(shape))
    return pl.pallas_call(
        functools.partial(_final_kernel, alpha=alpha, tm=tm),
        grid=(nt,),
        in_specs=[pl.BlockSpec((1, 1, 2 * tm), lambda i: (i, 0, 0), memory_space=pltpu.SMEM),
                  pl.BlockSpec((1, 1, 2 * tm), lambda i: (jnp.minimum(i + 1, nt - 1), 0, 0), memory_space=pltpu.SMEM),
                  rowb(D), pl.BlockSpec(memory_space=pl.ANY), rowb(LANE),
                  pl.BlockSpec((1, 1, D), lambda i: (modrow(i), 0, 5)), full((1, D)), full((1, D))],
        out_specs=rowb(D),
        out_shape=jax.ShapeDtypeStruct((rows, D), F32),
        scratch_shapes=[pltpu.VMEM((2, 2, tm, D), F32), pltpu.SemaphoreType.DMA((2,))],
        compiler_params=_params("arbitrary"),
        name="ffn_residual",
    )(toks, toks, x1, yb, wts, mods3, ln_g.reshape(1, D), ln_b.reshape(1, D))


def _rope_tables(S):
    rows = S // GRID_W
    row = jnp.repeat(jnp.arange(rows, dtype=F32), GRID_W)
    col = jnp.tile(jnp.arange(GRID_W, dtype=F32), rows)
    nf = ATT_HEAD_DIM // 4
    inv = ROPE_THETA ** (-jnp.arange(nf, dtype=F32) / nf)
    ar, ac = row[:, None] * inv, col[:, None] * inv
    cos = jnp.concatenate([jnp.cos(ar), jnp.cos(ar), jnp.cos(ac), jnp.cos(ac)], -1)
    sin = jnp.concatenate([-jnp.sin(ar), jnp.sin(ar), -jnp.sin(ac), jnp.sin(ac)], -1)
    return cos, sin


def kernel(x, c, ctx, c_ctx, w_mod, b_mod, w_in, b_gate, q_norm, k_norm, w_att_o, rwkv_mu, rwkv_w0, rwkv_w2, rwkv_a0, rwkv_a2, rwkv_g2, rwkv_k_k, rwkv_k_a, rwkv_r_k, rwkv_ln_g, rwkv_ln_b, w_rwkv_o, conv_w, conv_b, conv_ln_g, conv_ln_b, w_conv_o, w_out, ln1_g, ln1_b, w_group, b_group, w_router, b_router, w_e_gate, w_e_up, w_e_down, ln2_g, ln2_b):
    B, S, D = x.shape
    Tc = ctx.shape[1]
    depth = w_mod.shape[0]
    NL, NC = B * S, B * Tc
    N = NL + NC
    alpha = (2 * depth) ** 0.25
    att_w = w_att_o.shape[1]
    heads = att_w // ATT_HEAD_DIM
    rw_in = rwkv_mu.shape[-1]
    conv_cw = w_conv_o.shape[1]
    kv_w = (w_in.shape[-1] - att_w - rw_in - 2 * conv_cw - 3 * D) // 2
    assert kv_w == heads // ATT_GROUP * ATT_HEAD_DIM
    o_rw = att_w + 2 * kv_w
    o_cv = o_rw + rw_in
    o_gt = o_cv + 2 * conv_cw

    tmod = _tile(math.gcd(S, Tc), 256)
    tps = S // tmod
    nlt = NL // tmod
    modrow = lambda i: jnp.where(i < nlt, i // tps, B)
    cos_t, sin_t = _rope_tables(S)

    R = -(-(B + 1) // SUBLANE) * SUBLANE
    cv = jnp.zeros((R, D), F32).at[:B].set(c).at[B].set(c_ctx)
    src = (x.reshape(NL, D), ctx.reshape(NC, D), nlt, 0)

    for l in range(depth):
        last = l == depth - 1
        rows = NL if last else N
        p = dict(rwkv_mu=rwkv_mu[l], rwkv_w0=rwkv_w0[l], rwkv_w2=rwkv_w2[l], rwkv_a0=rwkv_a0[l], rwkv_a2=rwkv_a2[l],
                 rwkv_g2=rwkv_g2[l], rwkv_k_k=rwkv_k_k[l], rwkv_k_a=rwkv_k_a[l], rwkv_r_k=rwkv_r_k[l],
                 rwkv_ln_g=rwkv_ln_g[l], rwkv_ln_b=rwkv_ln_b[l], conv_w=conv_w[l], conv_b=conv_b[l],
                 conv_ln_g=conv_ln_g[l], conv_ln_b=conv_ln_b[l])
        mods3 = _adaln(cv, w_mod, b_mod, l).reshape(R, 1, 6 * D)
        wl = w_in[l]
        h = _lnmod(src, mods3, 0, 1, N, modrow, tmod)
        qkv = _matmul(h, wl[:, :o_rw].astype(BF16), 512)
        rw = _matmul(h, wl[:, o_rw:o_cv].astype(BF16), 1152)
        cvx = _matmul(h, wl[:, o_cv:o_gt].astype(BF16), 1024, rows)
        gt = _matmul(h, wl[:, o_gt:].astype(BF16), 1024, rows)

        qn, kn = q_norm[l].reshape(1, LANE), k_norm[l].reshape(1, LANE)
        att = _attention(qkv, qn, kn, cos_t, sin_t, B=B, S=S, Tc=Tc, heads=heads, latent_queries=True)
        if not last:
            att_c = _attention(qkv, qn, kn, cos_t, sin_t, B=B, S=S, Tc=Tc, heads=heads, latent_queries=False)
            att = jnp.concatenate([att, att_c], 0)

        r, v, kk, kd, lw, bb, g, bonus = _rwprep(rw, p, B=B, S=S, Tc=Tc)
        Am, Bm, Qm, Y0 = _scanprep(r, v, kk, kd, lw, bb)
        y2 = _scanserial(Am, Bm, Qm, Y0, B=B, S=S, Tc=Tc)
        rwo = _rwout(y2, bonus, g, p, rows)
        cvo = _conv(cvx, p, B=B, S=S, Tc=Tc, rows=rows)

        m = _merge(att, rwo, cvo, gt, w_att_o[l].astype(BF16), w_rwkv_o[l].astype(BF16), w_conv_o[l].astype(BF16),
                   b_gate[l], rows)
        w_rt = jnp.zeros((D, LANE), F32).at[:, :N_EXPERTS].set(w_router[l]).at[:, N_EXPERTS:N_EXPERTS + N_GROUPS].set(w_group[l])
        b_rt = jnp.zeros((1, LANE), F32).at[0, :N_EXPERTS].set(b_router[l]).at[0, N_EXPERTS:N_EXPERTS + N_GROUPS].set(b_group[l])
        w_rt_hi = w_rt.astype(BF16)
        w_rt = jnp.concatenate([w_rt_hi, (w_rt - w_rt_hi.astype(F32)).astype(BF16)], 1)
        x1, h2, logits = _outproj(m, w_out[l].astype(BF16), src, mods3, ln1_g[l], ln1_b[l], w_rt, b_rt, rows, modrow,
                                  tmod, alpha)

        ids128, wts128, cnt128 = _route(logits)
        slot_tok, dest, block_e, active = _dispatch(ids128[:, :2], ids128[:, 2:4], cnt128[0, :N_EXPERTS].astype(jnp.int32))
        yb = _moe_ffn(h2, slot_tok, block_e, active, w_e_gate, w_e_up, w_e_down, l)
        xa = _final(x1, yb, dest, wts128, mods3, ln2_g[l], ln2_b[l], rows, modrow, tmod, alpha)
        src = (xa, xa, nlt, nlt)

    return xa[:NL].reshape(B, S, D)
```

```python
import functools
import math

import numpy as np
import jax
import jax.numpy as jnp
from jax import lax
from jax.experimental import pallas as pl
from jax.experimental.pallas import tpu as pltpu

F32 = jnp.float32
BF16 = jnp.bfloat16
HI = lax.Precision.HIGHEST

LANE = 128
SUBLANE = 8
ATT_HEAD_DIM = 128
ATT_GROUP = 4
ATT_HEADS_PER_STEP = 2
ROPE_THETA = 10000.0
GRID_W = 64
RWKV_HEAD_DIM = 64
CHUNK = 64
CONV_HALO = 16
SHIFT_HALO = 8
N_GROUPS = 4
EXPERTS_PER_GROUP = 8
N_EXPERTS = N_GROUPS * EXPERTS_PER_GROUP
MOE_BLOCK = 256
GN_EPS = 64e-5
LN_EPS = 1e-6
RMS_EPS = 1e-6
VMEM_LIMIT = 56 * 1024 * 1024

NT_DIMS = (((1,), (1,)), ((), ()))
TN_DIMS = (((0,), (0,)), ((), ()))


def _params(*sem):
    return pltpu.CompilerParams(dimension_semantics=sem, vmem_limit_bytes=VMEM_LIMIT)


def _sigmoid(x):
    return 1.0 / (1.0 + jnp.exp(-x))


def _ln(x):
    mu = jnp.mean(x, -1, keepdims=True)
    xc = x - mu
    var = jnp.mean(xc * xc, -1, keepdims=True)
    return xc * lax.rsqrt(var + LN_EPS)


def _tile(n, pref):
    t = min(n, pref)
    while n % t:
        t -= SUBLANE
    return t


def _adaln_kernel(c_ref, w_ref, b_ref, o_ref):
    c = c_ref[...]
    s = c * _sigmoid(c)
    o_ref[...] = jnp.dot(s, w_ref[0], precision=HI, preferred_element_type=F32) + b_ref[0]


def _adaln(cv, w_mod, b_mod, layer):
    R, D = cv.shape
    L, _, W = w_mod.shape
    tn = _tile(W, 1024)
    return pl.pallas_call(
        _adaln_kernel,
        grid=(W // tn,),
        in_specs=[pl.BlockSpec((R, D), lambda j: (0, 0)),
                  pl.BlockSpec((1, D, tn), lambda j: (layer, 0, j)),
                  pl.BlockSpec((1, 1, tn), lambda j: (layer, 0, j))],
        out_specs=pl.BlockSpec((R, tn), lambda j: (0, j)),
        out_shape=jax.ShapeDtypeStruct((R, W), F32),
        compiler_params=_params("parallel"),
        name="adaln",
    )(cv, w_mod, b_mod.reshape(L, 1, W))


def _stream_specs(src, tm, D):
    lat, tail, nlt, off = src
    return [lat, tail], [pl.BlockSpec((tm, D), lambda i: (jnp.minimum(i, nlt - 1), 0)),
                         pl.BlockSpec((tm, D), lambda i: (jnp.maximum(i - nlt, 0) + off, 0))]


def _lnmod_kernel(xl_ref, xt_ref, sh_ref, sc_ref, o_ref, *, nlt):
    x = jnp.where(pl.program_id(0) < nlt, xl_ref[...], xt_ref[...])
    o_ref[...] = (_ln(x) * (1.0 + sc_ref[0]) + sh_ref[0]).astype(o_ref.dtype)


def _lnmod(src, mods3, shift_blk, scale_blk, rows, modrow, tm):
    D = src[0].shape[1]
    arrs, specs = _stream_specs(src, tm, D)
    return pl.pallas_call(
        functools.partial(_lnmod_kernel, nlt=src[2]),
        grid=(rows // tm,),
        in_specs=specs + [pl.BlockSpec((1, 1, D), lambda i: (modrow(i), 0, shift_blk)),
                          pl.BlockSpec((1, 1, D), lambda i: (modrow(i), 0, scale_blk))],
        out_specs=pl.BlockSpec((tm, D), lambda i: (i, 0)),
        out_shape=jax.ShapeDtypeStruct((rows, D), BF16),
        compiler_params=_params("parallel"),
        name="lnmod",
    )(*arrs, mods3, mods3)


def _mm_kernel(x_ref, w_ref, o_ref):
    o_ref[...] = jnp.dot(x_ref[...], w_ref[...], preferred_element_type=F32).astype(o_ref.dtype)


def _matmul(x, w, tn_pref, rows=None, out_dtype=F32, tm_pref=2048):
    M, K = x.shape
    M = rows or M
    N = w.shape[1]
    tm = _tile(M, tm_pref)
    tn = _tile(N, tn_pref)
    return pl.pallas_call(
        _mm_kernel,
        grid=(N // tn, M // tm),
        in_specs=[pl.BlockSpec((tm, K), lambda j, i: (i, 0)),
                  pl.BlockSpec((K, tn), lambda j, i: (0, j))],
        out_specs=pl.BlockSpec((tm, tn), lambda j, i: (i, j)),
        out_shape=jax.ShapeDtypeStruct((M, N), out_dtype),
        compiler_params=_params("parallel", "parallel"),
        name="matmul",
    )(x, w)


def _rms(t, g):
    return t * lax.rsqrt(jnp.mean(t * t, -1, keepdims=True) + RMS_EPS) * g


def _rope(t, cos, sin_signed):
    lane = lax.broadcasted_iota(jnp.int32, t.shape, 1)
    first = (lane & 63) < 32
    partner = jnp.where(first, pltpu.roll(t, LANE - 32, 1), pltpu.roll(t, 32, 1))
    return t * cos + partner * sin_signed


def _attn_kernel(*refs, rope, has_lat, tq, scale, hg):
    if has_lat:
        (q_ref, kc_ref, vc_ref, qn_ref, kn_ref, kl_ref, vl_ref, cq_ref, sq_ref, ck_ref, sk_ref,
         o_ref, kcb, vcb, klb, vlb) = refs
    else:
        q_ref, kc_ref, vc_ref, qn_ref, kn_ref, o_ref, kcb, vcb = refs

    @pl.when(pl.program_id(2) == 0)
    def _():
        kcb[...] = _rms(kc_ref[...], kn_ref[...]).astype(BF16)
        vcb[:, :LANE] = vc_ref[...].astype(BF16)
        vcb[:, LANE:] = jnp.ones(vc_ref.shape, BF16)
        if has_lat:
            klb[...] = _rope(_rms(kl_ref[...], kn_ref[...]), ck_ref[...], sk_ref[...]).astype(BF16)
            vlb[:, :LANE] = vl_ref[...].astype(BF16)
            vlb[:, LANE:] = jnp.ones(vl_ref.shape, BF16)

    groups = [range(g0, g0 + hg) for g0 in range(0, ATT_GROUP, hg)]
    scores = {}

    def qk(gi):
        qs = []
        for g in groups[gi]:
            qg = _rms(q_ref[:, g * LANE:(g + 1) * LANE], qn_ref[...])
            if rope:
                qg = _rope(qg, cq_ref[...], sq_ref[...])
            qs.append((qg * (scale * math.log2(math.e))).astype(BF16))
        qa = jnp.concatenate(qs, axis=0)
        sc = lax.dot_general(qa, kcb[...], NT_DIMS, preferred_element_type=F32)
        sl = lax.dot_general(qa, klb[...], NT_DIMS, preferred_element_type=F32) if has_lat else None
        scores[gi] = (sc, sl)

    def softmax_pv(gi):
        sc, sl = scores.pop(gi)
        m = jnp.max(sc, -1, keepdims=True)
        if has_lat:
            m = jnp.maximum(m, jnp.max(sl, -1, keepdims=True))
        od = jnp.dot(jnp.exp2(sc - m).astype(BF16), vcb[...], preferred_element_type=F32)
        if has_lat:
            od = od + jnp.dot(jnp.exp2(sl - m).astype(BF16), vlb[...], preferred_element_type=F32)
        o = od[:, :LANE] / od[:, LANE:]
        for n, g in enumerate(groups[gi]):
            o_ref[:, g * LANE:(g + 1) * LANE] = o[n * tq:(n + 1) * tq].astype(o_ref.dtype)

    qk(0)
    for gi in range(1, len(groups)):
        qk(gi)
        softmax_pv(gi - 1)
    softmax_pv(len(groups) - 1)


def _attention(qkv, qn, kn, cos_t, sin_t, *, B, S, Tc, heads, latent_queries):
    NL = B * S
    kvh = heads // ATT_GROUP
    GW = ATT_GROUP * LANE
    scale = ATT_HEAD_DIM ** -0.5
    Tq = S if latent_queries else Tc
    tq = _tile(Tq, 256)
    nq = Tq // tq
    qbase = 0 if latent_queries else NL // tq
    cbase = NL // Tc
    common = [pl.BlockSpec((tq, GW), lambda b, j, i: (qbase + b * nq + i, j)),
              pl.BlockSpec((Tc, LANE), lambda b, j, i: (cbase + b, heads + j)),
              pl.BlockSpec((Tc, LANE), lambda b, j, i: (cbase + b, heads + kvh + j)),
              pl.BlockSpec((1, LANE), lambda b, j, i: (0, 0)),
              pl.BlockSpec((1, LANE), lambda b, j, i: (0, 0))]
    args = [qkv, qkv, qkv, qn, kn]
    scratch = [pltpu.VMEM((Tc, LANE), BF16), pltpu.VMEM((Tc, 2 * LANE), BF16)]
    if latent_queries:
        common += [pl.BlockSpec((S, LANE), lambda b, j, i: (b, heads + j)),
                   pl.BlockSpec((S, LANE), lambda b, j, i: (b, heads + kvh + j)),
                   pl.BlockSpec((tq, LANE), lambda b, j, i: (i, 0)),
                   pl.BlockSpec((tq, LANE), lambda b, j, i: (i, 0)),
                   pl.BlockSpec((S, LANE), lambda b, j, i: (0, 0)),
                   pl.BlockSpec((S, LANE), lambda b, j, i: (0, 0))]
        args += [qkv, qkv, cos_t, sin_t, cos_t, sin_t]
        scratch += [pltpu.VMEM((S, LANE), BF16), pltpu.VMEM((S, 2 * LANE), BF16)]
    return pl.pallas_call(
        functools.partial(_attn_kernel, rope=latent_queries, has_lat=latent_queries, tq=tq, scale=scale, hg=ATT_HEADS_PER_STEP),
        grid=(B, kvh, nq),
        in_specs=common,
        out_specs=pl.BlockSpec((tq, GW), lambda b, j, i: (b * nq + i, j)),
        out_shape=jax.ShapeDtypeStruct((B * Tq, heads * LANE), BF16),
        scratch_shapes=scratch,
        compiler_params=_params("parallel", "parallel", "arbitrary"),
        name="attention_lat" if latent_queries else "attention_ctx",
    )(*args)


def _segsum(x, seg_ref, segt_ref):
    return _split_dot(_split_dot(x, seg_ref[...]), segt_ref[...])


def _split_dot(x, w):
    hi = x.astype(BF16)
    lo = (x - hi.astype(F32)).astype(BF16)
    return jnp.dot(hi, w, preferred_element_type=F32) + jnp.dot(lo, w, preferred_element_type=F32)


def _rwprep_kernel(first_ref, last_ref, z_ref, zp_ref, zn_ref, mu_ref, w0_ref, w2_ref, a0_ref, a2_ref, g2_ref,
                   kkp_ref, ka_ref, rk_ref, seg_ref, segt_ref,
                   r_o, v_o, kk_o, kd_o, lw_o, bb_o, g_o, bon_o, *, RW, R2W, R2A, RG):
    i = pl.program_id(0)
    has_prev = (1 - first_ref[i]).astype(F32)
    has_next = (1 - last_ref[i]).astype(F32)
    tt = z_ref.shape[0]
    row8 = lax.broadcasted_iota(jnp.int32, (SUBLANE, 1), 0)

    def shifted(lo, hi):
        z = z_ref[:, lo:hi]
        hp = zp_ref[SHIFT_HALO - 1:SHIFT_HALO, lo:hi] * has_prev
        hn = zn_ref[0:1, lo:hi] * has_next
        prev = pltpu.roll(z, 1, 0)
        nxt = pltpu.roll(z, tt - 1, 0)
        prev = jnp.concatenate([jnp.where(row8 == 0, hp, prev[:SUBLANE]), prev[SUBLANE:]], axis=0)
        nxt = jnp.concatenate([nxt[:tt - SUBLANE], jnp.where(row8 == SUBLANE - 1, hn, nxt[tt - SUBLANE:])], axis=0)
        m0, m1 = mu_ref[0:1, lo:hi], mu_ref[1:2, lo:hi]
        return (1.0 - m0 - m1) * z + m0 * prev + m1 * nxt

    o = 3 * RW
    r = shifted(0, RW)
    k = shifted(RW, 2 * RW)
    v = shifted(2 * RW, o)
    wl = shifted(o, o + R2W)
    al = shifted(o + R2W, o + R2W + R2A)
    gl = shifted(o + R2W + R2A, o + R2W + R2A + RG)

    u = -(w0_ref[...] + _bdot(jnp.tanh(wl), w2_ref[...]))
    softplus = jnp.maximum(u, 0.0) + jnp.log(1.0 + jnp.exp(-jnp.abs(u)))
    lw_o[...] = -jnp.exp(-softplus - 0.5)
    a = _sigmoid(a0_ref[...] + _bdot(al, a2_ref[...]))
    g_o[...] = _bdot(_sigmoid(gl), g2_ref[...])

    kk = k * kkp_ref[...]
    kk = kk / jnp.maximum(jnp.sqrt(_segsum(kk * kk, seg_ref, segt_ref)), 1e-12)
    ka = ka_ref[...]
    kd0 = k * (1.0 + (a[:, :RW] - 1.0) * ka)
    kd1 = k * (1.0 + (a[:, RW:] - 1.0) * ka)
    r_o[...] = r
    v_o[...] = v
    kk_o[...] = kk
    kd_o[:, :RW] = kd0
    kd_o[:, RW:] = kd1
    bb_o[:, :RW] = kk * a[:, :RW]
    bb_o[:, RW:] = kk * a[:, RW:]
    bon_o[...] = _segsum(r * rk_ref[...] * (kd0 + kd1), seg_ref, segt_ref) * v


def _blockdiag2(w):
    z = jnp.zeros_like(w[0])
    return jnp.concatenate([jnp.concatenate([w[0], z], 1), jnp.concatenate([z, w[1]], 1)], 0)


def _seq_edges(B, S, Tc, tt):
    first, last = [], []
    for n, L in ((B, S), (B, Tc)):
        per = L // tt
        for _ in range(n):
            first += [1] + [0] * (per - 1)
            last += [0] * (per - 1) + [1]
    return jnp.asarray(np.array(first, np.int32)), jnp.asarray(np.array(last, np.int32))


def _rwprep(rw, p, *, B, S, Tc):
    N, RWIN = rw.shape
    RW = p["rwkv_k_k"].shape[0]
    R2W = 2 * p["rwkv_w2"].shape[1]
    R2A = 2 * p["rwkv_a2"].shape[1]
    RG = p["rwkv_g2"].shape[0]
    tt = _tile(math.gcd(S, Tc), 128)
    nt = N // tt
    hb = tt // SHIFT_HALO
    nhb = N // SHIFT_HALO
    first, last = _seq_edges(B, S, Tc, tt)
    nh = RW // RWKV_HEAD_DIM
    seg = (jnp.arange(RW)[:, None] // RWKV_HEAD_DIM == jnp.arange(LANE)[None, :]).astype(BF16)
    full = lambda shape: pl.BlockSpec(shape, lambda i, f, l: (0,) * len(shape))
    row = lambda w: pl.BlockSpec((tt, w), lambda i, f, l: (i, 0))
    out_w = [RW, RW, RW, 2 * RW, 2 * RW, 2 * RW, RW, RW]
    assert nh <= LANE
    grid_spec = pltpu.PrefetchScalarGridSpec(
        num_scalar_prefetch=2,
        grid=(nt,),
        in_specs=[row(RWIN),
                  pl.BlockSpec((SHIFT_HALO, RWIN), lambda i, f, l: (jnp.maximum(i * hb - 1, 0), 0)),
                  pl.BlockSpec((SHIFT_HALO, RWIN), lambda i, f, l: (jnp.minimum((i + 1) * hb, nhb - 1), 0)),
                  full((2, RWIN)), full((1, 2 * RW)), full((R2W, 2 * RW)), full((1, 2 * RW)), full((R2A, 2 * RW)),
                  full((RG, RW)), full((1, RW)), full((1, RW)), full((1, RW)), full((RW, LANE)), full((LANE, RW))],
        out_specs=[row(w) for w in out_w],
    )
    return pl.pallas_call(
        functools.partial(_rwprep_kernel, RW=RW, R2W=R2W, R2A=R2A, RG=RG),
        grid_spec=grid_spec,
        out_shape=[jax.ShapeDtypeStruct((N, w), F32) for w in out_w],
        compiler_params=_params("parallel"),
        name="rwkv_prep",
    )(first, last, rw, rw, rw, p["rwkv_mu"], p["rwkv_w0"].reshape(1, 2 * RW), _blockdiag2(p["rwkv_w2"]),
      p["rwkv_a0"].reshape(1, 2 * RW), _blockdiag2(p["rwkv_a2"]), p["rwkv_g2"],
      p["rwkv_k_k"].reshape(1, RW), p["rwkv_k_a"].reshape(1, RW), p["rwkv_r_k"].reshape(1, RW), seg, seg.T)


def _bdot(a, b):
    return jnp.dot(a.astype(BF16), b.astype(BF16), preferred_element_type=F32)


def _scanprep_kernel(r_ref, v_ref, kk_ref, kd_ref, lw_ref, bb_ref, A_o, B_o, Q_o, Y_o, *, G, ngrp):
    C = CHUNK
    HD = RWKV_HEAD_DIM
    shift = int(math.log2(C))
    hshift = int(math.log2(HD))
    sign = 1 - 2 * pl.program_id(0)
    ti = lax.broadcasted_iota(jnp.int32, (G, G), 0)
    si = lax.broadcasted_iota(jnp.int32, (G, G), 1)
    same = jnp.right_shift(ti, shift) == jnp.right_shift(si, shift)
    strict = jnp.logical_and(same, sign * (ti - si) > 0)
    ti2 = lax.broadcasted_iota(jnp.int32, (G, 2 * G), 0)
    si2 = jnp.bitwise_and(lax.broadcasted_iota(jnp.int32, (G, 2 * G), 1), G - 1)
    incl2 = jnp.logical_and(jnp.right_shift(ti2, shift) == jnp.right_shift(si2, shift), sign * (ti2 - si2) >= 0)
    eye = (si == ti).astype(F32)
    head0 = lax.broadcasted_iota(jnp.int32, (G, LANE), 1) < HD
    hi2 = lax.broadcasted_iota(jnp.int32, (LANE, LANE), 0)
    hj2 = lax.broadcasted_iota(jnp.int32, (LANE, LANE), 1)
    same_head = jnp.right_shift(hi2, hshift) == jnp.right_shift(hj2, hshift)
    eye_h = (hi2 == hj2).astype(F32)
    sel = lambda a0, a1: jnp.where(head0, a0, a1)
    both = (0, 1)
    fwd = (pl.program_id(0) == 0).astype(F32)
    trow = jnp.bitwise_and(lax.broadcasted_iota(jnp.int32, (G, LANE), 0), C - 1)

    st = []
    for g in range(ngrp):
        rows = slice(g * G, (g + 1) * G)
        lw = lw_ref[rows, :]
        F = lw
        for k in (1 << i for i in range(shift)):
            F = F + jnp.where(trow >= k, pltpu.roll(F, k, 0), 0.0)
        tot = jnp.concatenate([jnp.broadcast_to(F[(c + 1) * C - 1:(c + 1) * C], (C, LANE)) for c in range(G // C)], axis=0)
        P = fwd * F + (1.0 - fwd) * (tot - F + lw)
        eP = jnp.exp(P)
        enP = jnp.exp(-P)
        at = -kk_ref[rows, :] * jnp.exp(P - lw)
        bt = (bb_ref[rows, :] * enP).astype(BF16)
        kt = (kd_ref[rows, :] * enP).astype(BF16)
        rt = r_ref[rows, :] * eP
        vv = v_ref[rows, :].astype(BF16)
        rest = jnp.exp(tot - P)
        bk = jnp.concatenate([bt, kt], axis=0)
        L, lak, grbk = [], [], []
        for h in both:
            hm = head0 if h == 0 else jnp.logical_not(head0)
            ar = jnp.concatenate([jnp.where(hm, at, 0.0), jnp.where(hm, rt, 0.0)], axis=0).astype(BF16)
            gm = lax.dot_general(ar, bk, NT_DIMS, preferred_element_type=F32)
            L.append(jnp.where(strict, gm[:G, :G], 0.0))
            lak.append(jnp.where(strict, gm[:G, G:], 0.0).astype(BF16))
            grbk.append(jnp.where(incl2, gm[G:], 0.0).astype(BF16))
        st.append(dict(at=at.astype(BF16), rt=rt, vv=vv, L=L, lak=lak, grbk=grbk,
                       bkd=[jnp.concatenate([(bb_ref[rows, :] * rest)[c * C:(c + 1) * C],
                                             (kd_ref[rows, :] * rest)[c * C:(c + 1) * C]], axis=0).astype(BF16)
                            for c in range(G // C)],
                       dec=jnp.exp(tot)))

    for s in st:
        s["T"] = [eye + s["L"][h] for h in both]
        lb = [s["L"][h].astype(BF16) for h in both]
        s["Lp"] = [jnp.dot(lb[h], lb[h], preferred_element_type=F32).astype(BF16) for h in both]
    for lvl in range(1, shift):
        for s in st:
            for h in both:
                if lvl < shift - 1:
                    x = jnp.concatenate([s["Lp"][h], s["T"][h].astype(BF16)], axis=1)
                    res = jnp.dot(s["Lp"][h], x, preferred_element_type=F32)
                    s["Lp"][h] = res[:, :G].astype(BF16)
                    s["T"][h] = s["T"][h] + res[:, G:]
                else:
                    s["T"][h] = s["T"][h] + jnp.dot(s["Lp"][h], s["T"][h].astype(BF16), preferred_element_type=F32)
    for s in st:
        s["lv"] = sel(*[jnp.dot(s["lak"][h], s["vv"], preferred_element_type=F32) for h in both]).astype(BF16)
    for s in st:
        x = jnp.concatenate([s["at"], s["lv"]], axis=1)
        tw = [jnp.dot(s["T"][h].astype(BF16), x, preferred_element_type=F32) for h in both]
        s["tw"] = jnp.concatenate([sel(tw[0][:, :LANE], tw[1][:, :LANE]), sel(tw[0][:, LANE:], tw[1][:, LANE:])],
                                  axis=1).astype(BF16)
    for g, s in enumerate(st):
        rows = slice(g * G, (g + 1) * G)
        zv = jnp.concatenate([jnp.zeros((G, LANE), BF16), s["vv"]], axis=1)
        rhs = jnp.concatenate([s["tw"], zv], axis=0)
        qy = [jnp.dot(s["grbk"][h], rhs, preferred_element_type=F32) for h in both]
        Q_o[0, rows, :] = (s["rt"] + sel(qy[0][:, :LANE], qy[1][:, :LANE])).astype(Q_o.dtype)
        Y_o[0, rows, :] = sel(qy[0][:, LANE:], qy[1][:, LANE:])
        for c in range(G // C):
            cr = slice(c * C, (c + 1) * C)
            rc = jnp.concatenate([s["tw"][cr], zv[cr]], axis=0)
            ab = lax.dot_general(s["bkd"][c], rc, TN_DIMS, preferred_element_type=F32)
            k = g * (G // C) + c
            A_o[0, k, 0] = jnp.where(same_head, eye_h * s["dec"][c * C:c * C + 1] + ab[:, :LANE], 0.0).astype(A_o.dtype)
            B_o[0, k, 0] = jnp.where(same_head, ab[:, LANE:], 0.0)


def _scanprep(r, v, kk, kd, lw, bb):
    N, RW = r.shape
    hp = RW // LANE
    G = _tile(N, 2 * CHUNK)
    ngrp = max(n for n in (8, 4, 2, 1) if N % (n * G) == 0)
    rb = G * ngrp
    nchunks = N // CHUNK
    shared = lambda: pl.BlockSpec((rb, LANE), lambda d, i, h: (i, h))
    perdir = lambda: pl.BlockSpec((rb, LANE), lambda d, i, h: (i, d * hp + h))
    rowo = lambda: pl.BlockSpec((1, rb, LANE), lambda d, i, h: (d, i, h))
    mato = lambda: pl.BlockSpec((1, rb // CHUNK, 1, LANE, LANE), lambda d, i, h: (d, i, h, 0, 0))
    return pl.pallas_call(
        functools.partial(_scanprep_kernel, G=G, ngrp=ngrp),
        grid=(2, N // rb, hp),
        in_specs=[shared(), shared(), shared(), perdir(), perdir(), perdir()],
        out_specs=[mato(), mato(), rowo(), rowo()],
        out_shape=[jax.ShapeDtypeStruct((2, nchunks, hp, LANE, LANE), BF16),
                   jax.ShapeDtypeStruct((2, nchunks, hp, LANE, LANE), F32),
                   jax.ShapeDtypeStruct((2, N, RW), BF16),
                   jax.ShapeDtypeStruct((2, N, RW), F32)],
        compiler_params=_params("parallel", "parallel", "parallel"),
        name="rwkv_chunk_maps",
    )(r, v, kk, kd, lw, bb)


def _scanserial_kernel(A_ref, B_ref, Q_ref, Y0_ref, y_ref, s_ref, *, npair, cb):
    @pl.when(pl.program_id(2) == 0)
    def _():
        s_ref[...] = jnp.zeros_like(s_ref)

    C = CHUNK
    d = pl.program_id(0)
    states = [s_ref[h] for h in range(npair)]
    for k in range(cb):
        c = jnp.where(d == 0, k, cb - 1 - k)
        rows = pl.ds(pl.multiple_of(c * C, C), C)
        for h in range(npair):
            ls = slice(h * LANE, (h + 1) * LANE)
            aq = jnp.concatenate([A_ref[0, c, h], Q_ref[0, rows, ls]], axis=0)
            res = jnp.dot(aq, states[h].astype(BF16), preferred_element_type=F32)
            y_ref[0, rows, ls] = res[LANE:] + Y0_ref[0, rows, ls]
            states[h] = res[:LANE] + B_ref[0, c, h]
    for h in range(npair):
        s_ref[h] = states[h]


def _scanserial(A, Bm, Q, Y0, *, B, S, Tc):
    _, nchunks, npair, _, _ = A.shape
    C = CHUNK
    N = nchunks * C
    RW = npair * LANE
    cb = math.gcd(math.gcd(S // C, Tc // C), 4)
    nl, nc = S // (C * cb), Tc // (C * cb)
    cbase = B * nl

    def chunk_of(d, b, p):
        kc = jnp.where(d == 0, p, nc - 1 - p)
        kl = jnp.where(d == 0, p - nc, nl - 1 - (p - nc))
        return jnp.where(p < nc, cbase + b * nc + kc, b * nl + kl)

    mat = lambda: pl.BlockSpec((1, cb, npair, LANE, LANE), lambda d, b, p: (d, chunk_of(d, b, p), 0, 0, 0))
    row = lambda: pl.BlockSpec((1, cb * C, RW), lambda d, b, p: (d, chunk_of(d, b, p), 0))
    return pl.pallas_call(
        functools.partial(_scanserial_kernel, npair=npair, cb=cb),
        grid=(2, B, nl + nc),
        in_specs=[mat(), mat(), row(), row()],
        out_specs=row(),
        out_shape=jax.ShapeDtypeStruct((2, N, RW), F32),
        scratch_shapes=[pltpu.VMEM((npair, LANE, LANE), F32)],
        compiler_params=_params("parallel", "parallel", "arbitrary"),
        name="rwkv_chunk_chain",
    )(A, Bm, Q, Y0)


def _rwout_kernel(y_ref, bon_ref, g_ref, lg_ref, lb_ref, seg_ref, segt_ref, o_ref):
    y = y_ref[0] + y_ref[1]
    inv = 1.0 / RWKV_HEAD_DIM
    mu = _segsum(y, seg_ref, segt_ref) * inv
    yc = y - mu
    var = _segsum(yc * yc, seg_ref, segt_ref) * inv
    yn = yc * lax.rsqrt(var + GN_EPS) * lg_ref[...] + lb_ref[...]
    o_ref[...] = ((yn + bon_ref[...]) * g_ref[...]).astype(o_ref.dtype)


def _rwout(y2, bonus, g, p, rows):
    RW = g.shape[1]
    tm = _tile(rows, 256)
    seg = (jnp.arange(RW)[:, None] // RWKV_HEAD_DIM == jnp.arange(LANE)[None, :]).astype(BF16)
    full = lambda shape: pl.BlockSpec(shape, lambda i: (0,) * len(shape))
    return pl.pallas_call(
        _rwout_kernel,
        grid=(rows // tm,),
        in_specs=[pl.BlockSpec((2, tm, RW), lambda i: (0, i, 0)),
                  pl.BlockSpec((tm, RW), lambda i: (i, 0)),
                  pl.BlockSpec((tm, RW), lambda i: (i, 0)),
                  full((1, RW)), full((1, RW)), full((RW, LANE)), full((LANE, RW))],
        out_specs=pl.BlockSpec((tm, RW), lambda i: (i, 0)),
        out_shape=jax.ShapeDtypeStruct((rows, RW), BF16),
        compiler_params=_params("parallel"),
        name="rwkv_out",
    )(y2, bonus, g, p["rwkv_ln_g"].reshape(1, RW), p["rwkv_ln_b"].reshape(1, RW), seg, seg.T)


CONV_ROWS = 64


def _conv_kernel(first_ref, last_ref, z_ref, zp_ref, zn_ref, w_ref, b_ref, lg_ref, lb_ref, o_ref, u_ref, c_ref, *, CW, K):
    i = pl.program_id(0)
    has_prev = (1 - first_ref[i]).astype(F32)
    has_next = (1 - last_ref[i]).astype(F32)
    tt = z_ref.shape[0]
    H = CONV_HALO
    nlt = CW // LANE
    glu = lambda ref, lt: ref[:, lt * LANE:(lt + 1) * LANE] * _sigmoid(ref[:, CW + lt * LANE:CW + (lt + 1) * LANE])
    span = tt + SUBLANE * ((H + K // 2) // SUBLANE)
    for lt in range(nlt):
        u_ref[0, lt, 0:H, :] = glu(zp_ref, lt) * has_prev
        u_ref[0, lt, H:H + tt, :] = glu(z_ref, lt)
        u_ref[0, lt, H + tt:, :] = glu(zn_ref, lt) * has_next
        for s in range(1, SUBLANE):
            u_ref[s, lt, 0:span, :] = u_ref[0, lt, s:s + span, :]
    off = H - K // 2
    rb = min(CONV_ROWS, tt)
    nrb = tt // rb

    def block(idx, carry):
        lt = idx // nrb
        r0 = pl.multiple_of((idx % nrb) * rb, rb)
        acc = jnp.zeros((rb, LANE), F32) + b_ref[lt]
        for k in range(K):
            s, j = (off + k) % SUBLANE, (off + k) // SUBLANE
            acc = acc + w_ref[k, lt] * u_ref[s, lt, pl.ds(r0 + SUBLANE * j, rb), :]
        c_ref[lt, pl.ds(r0, rb), :] = acc
        return carry

    lax.fori_loop(0, nlt * nrb, block, 0)
    y = _ln(jnp.concatenate([c_ref[lt] for lt in range(nlt)], axis=1)) * lg_ref[...] + lb_ref[...]
    o_ref[...] = (y * _sigmoid(y)).astype(o_ref.dtype)


def _conv(cv, p, *, B, S, Tc, rows):
    CW = cv.shape[1] // 2
    K = p["conv_w"].shape[0]
    assert K // 2 <= CONV_HALO
    nlt = CW // LANE
    tt = _tile(math.gcd(S, Tc), 256)
    hb = tt // CONV_HALO
    nhb = cv.shape[0] // CONV_HALO
    first, last = _seq_edges(B, S, Tc, tt)
    full = lambda shape: pl.BlockSpec(shape, lambda i, f, l: (0,) * len(shape))
    grid_spec = pltpu.PrefetchScalarGridSpec(
        num_scalar_prefetch=2,
        grid=(rows // tt,),
        in_specs=[pl.BlockSpec((tt, 2 * CW), lambda i, f, l: (i, 0)),
                  pl.BlockSpec((CONV_HALO, 2 * CW), lambda i, f, l: (jnp.maximum(i * hb - 1, 0), 0)),
                  pl.BlockSpec((CONV_HALO, 2 * CW), lambda i, f, l: (jnp.minimum((i + 1) * hb, nhb - 1), 0)),
                  full((K, nlt, 1, LANE)), full((nlt, 1, LANE)), full((1, CW)), full((1, CW))],
        out_specs=pl.BlockSpec((tt, CW), lambda i, f, l: (i, 0)),
        scratch_shapes=[pltpu.VMEM((SUBLANE, nlt, tt + 2 * CONV_HALO, LANE), F32),
                        pltpu.VMEM((nlt, tt, LANE), F32)],
    )
    return pl.pallas_call(
        functools.partial(_conv_kernel, CW=CW, K=K),
        grid_spec=grid_spec,
        out_shape=jax.ShapeDtypeStruct((rows, CW), BF16),
        compiler_params=_params("parallel"),
        name="conv_module",
    )(first, last, cv, cv, cv, p["conv_w"].reshape(K, nlt, 1, LANE), p["conv_b"].reshape(nlt, 1, LANE),
      p["conv_ln_g"].reshape(1, CW), p["conv_ln_b"].reshape(1, CW))


def _merge_kernel(a_ref, r_ref, c_ref, wa_ref, wr_ref, wc_ref, g0_ref, g1_ref, g2_ref, b0_ref, b1_ref, b2_ref, o_ref):
    m = _sigmoid(g0_ref[...] + b0_ref[...]) * jnp.dot(a_ref[...], wa_ref[...], preferred_element_type=F32)
    m = m + _sigmoid(g1_ref[...] + b1_ref[...]) * jnp.dot(r_ref[...], wr_ref[...], preferred_element_type=F32)
    m = m + _sigmoid(g2_ref[...] + b2_ref[...]) * jnp.dot(c_ref[...], wc_ref[...], preferred_element_type=F32)
    o_ref[...] = m.astype(o_ref.dtype)


def _merge(att, rwo, cvo, gt, wa, wr, wc, b_gate, rows):
    D = wa.shape[1]
    tm = _tile(rows, 1024)
    tn = _tile(D, 512)
    nj = D // tn
    xin = lambda w: pl.BlockSpec((tm, w), lambda j, i: (i, 0))
    win = lambda w: pl.BlockSpec((w, tn), lambda j, i: (0, j))
    gate = lambda k: pl.BlockSpec((tm, tn), lambda j, i: (i, k * nj + j))
    bias = lambda k: pl.BlockSpec((1, tn), lambda j, i: (0, k * nj + j))
    bg = b_gate.reshape(1, 3 * D)
    return pl.pallas_call(
        _merge_kernel,
        grid=(nj, rows // tm),
        in_specs=[xin(att.shape[1]), xin(rwo.shape[1]), xin(cvo.shape[1]),
                  win(wa.shape[0]), win(wr.shape[0]), win(wc.shape[0]),
                  gate(0), gate(1), gate(2), bias(0), bias(1), bias(2)],
        out_specs=pl.BlockSpec((tm, tn), lambda j, i: (i, j)),
        out_shape=jax.ShapeDtypeStruct((rows, D), BF16),
        compiler_params=_params("parallel", "parallel"),
        name="branch_merge",
    )(att, rwo, cvo, wa, wr, wc, gt, gt, gt, bg, bg, bg)


def _outproj_kernel(m_ref, w_ref, xl_ref, xt_ref, gm_ref, sh_ref, sc_ref, lg_ref, lb_ref, wrt_ref, brt_ref,
                    x_o, h_o, lgt_o, *, alpha, nlt):
    y = jnp.dot(m_ref[...], w_ref[...], preferred_element_type=F32)
    x = jnp.where(pl.program_id(0) < nlt, xl_ref[...], xt_ref[...])
    x1 = _ln(alpha * x + gm_ref[0] * y) * lg_ref[...] + lb_ref[...]
    x_o[...] = x1
    h2 = _ln(x1) * (1.0 + sc_ref[0]) + sh_ref[0]
    half = h2.shape[1] // 2
    bits = lambda t: pltpu.bitcast(t.astype(BF16).astype(F32), jnp.uint32)
    h_o[...] = bits(h2[:, :half]) | lax.shift_right_logical(bits(h2[:, half:]), jnp.uint32(16))
    hi = h2.astype(BF16)
    lo = (h2 - hi.astype(F32)).astype(BF16)
    r1 = jnp.dot(hi, wrt_ref[...], preferred_element_type=F32)
    r2 = jnp.dot(lo, wrt_ref[:, :LANE], preferred_element_type=F32)
    lgt_o[...] = r1[:, :LANE] + r1[:, LANE:] + r2 + brt_ref[...]


def _outproj(m, w_out, src, mods3, ln_g, ln_b, w_rt, b_rt, rows, modrow, tm, alpha):
    D = w_out.shape[0]
    full = lambda shape: pl.BlockSpec(shape, lambda i: (0,) * len(shape))
    rowb = lambda w: pl.BlockSpec((tm, w), lambda i: (i, 0))
    mod = lambda blk: pl.BlockSpec((1, 1, D), lambda i: (modrow(i), 0, blk))
    xarrs, xspecs = _stream_specs(src, tm, D)
    return pl.pallas_call(
        functools.partial(_outproj_kernel, alpha=alpha, nlt=src[2]),
        grid=(rows // tm,),
        in_specs=[rowb(D), full((D, D))] + xspecs + [mod(2), mod(3), mod(4), full((1, D)), full((1, D)),
                                                     full((D, 2 * LANE)), full((1, LANE))],
        out_specs=[rowb(D), rowb(D // 2), rowb(LANE)],
        out_shape=[jax.ShapeDtypeStruct((rows, D), F32), jax.ShapeDtypeStruct((rows, D // 2), jnp.uint32),
                   jax.ShapeDtypeStruct((rows, LANE), F32)],
        compiler_params=_params("parallel"),
        name="out_proj",
    )(m, w_out, *xarrs, mods3, mods3, mods3, ln_g.reshape(1, D), ln_b.reshape(1, D), w_rt, b_rt)


def _route_kernel(l_ref, tri_ref, id_o, w_o, cnt_o, cnt_ref):
    lg = l_ref[...]
    lane = lax.broadcasted_iota(jnp.int32, lg.shape, 1)
    neg = -jnp.inf
    big = jnp.int32(2 ** 30)
    isg = jnp.logical_and(lane >= N_EXPERTS, lane < N_EXPERTS + N_GROUPS)
    gl = jnp.where(isg, lg, neg)
    gmax = jnp.max(gl, -1, keepdims=True)
    grp = jnp.min(jnp.where(gl == gmax, lane - N_EXPERTS, big), -1, keepdims=True)
    grp_w = 1.0 / jnp.sum(jnp.exp(gl - gmax), -1, keepdims=True)
    ing = jnp.logical_and(lane < N_EXPERTS, jnp.right_shift(lane, int(math.log2(EXPERTS_PER_GROUP))) == grp)
    el = jnp.where(ing, lg, neg)
    e1 = jnp.max(el, -1, keepdims=True)
    i1 = jnp.min(jnp.where(el == e1, lane, big), -1, keepdims=True)
    psum = jnp.sum(jnp.exp(el - e1), -1, keepdims=True)
    el2 = jnp.where(lane == i1, neg, el)
    e2 = jnp.max(el2, -1, keepdims=True)
    i2 = jnp.min(jnp.where(el2 == e2, lane, big), -1, keepdims=True)
    p1 = 1.0 / psum
    p2 = jnp.exp(e2 - e1) / psum
    tot = p1 + p2
    w_o[...] = jnp.where(lane == 0, grp_w * p1 / tot, jnp.where(lane == 1, grp_w * p2 / tot, 0.0))

    @pl.when(pl.program_id(0) == 0)
    def _():
        cnt_ref[...] = jnp.zeros_like(cnt_ref)

    hit = jnp.logical_or(lane == i1, lane == i2)
    before = cnt_ref[...] + jnp.dot(tri_ref[...], jnp.where(hit, 1.0, 0.0).astype(BF16), preferred_element_type=F32)
    r1 = jnp.sum(jnp.where(lane == i1, before, 0.0), -1, keepdims=True).astype(jnp.int32)
    r2 = jnp.sum(jnp.where(lane == i2, before, 0.0), -1, keepdims=True).astype(jnp.int32)
    id_o[...] = jnp.where(lane == 0, i1, jnp.where(lane == 1, i2, jnp.where(lane == 2, r1, jnp.where(lane == 3, r2, 0))))
    cnt_ref[...] = cnt_ref[...] + jnp.sum(jnp.where(hit, 1.0, 0.0), axis=0, keepdims=True)
    cnt_o[...] = cnt_ref[...]


def _route(logits):
    n = logits.shape[0]
    tm = _tile(n, 1024)
    tri = (jnp.arange(tm)[None, :] < jnp.arange(tm)[:, None]).astype(BF16)
    blk = lambda: pl.BlockSpec((tm, LANE), lambda i: (i, 0))
    return pl.pallas_call(
        _route_kernel,
        grid=(n // tm,),
        in_specs=[blk(), pl.BlockSpec((tm, tm), lambda i: (0, 0))],
        out_specs=[blk(), blk(), pl.BlockSpec((1, LANE), lambda i: (0, 0))],
        out_shape=[jax.ShapeDtypeStruct((n, LANE), jnp.int32), jax.ShapeDtypeStruct((n, LANE), F32),
                   jax.ShapeDtypeStruct((1, LANE), F32)],
        scratch_shapes=[pltpu.VMEM((1, LANE), F32)],
        compiler_params=_params("arbitrary"),
        name="route",
    )(logits, tri)


def _row_gather(tok_ref, h_hbm, xbuf, sem, slot, start, unrolled=False):
    if not start:
        pltpu.make_async_copy(h_hbm.at[pl.ds(0, MOE_BLOCK)], xbuf.at[slot], sem.at[slot]).wait()
        return

    def body(r, carry):
        pltpu.make_async_copy(h_hbm.at[pl.ds(tok_ref[0, 0, r], 1)], xbuf.at[slot, pl.ds(r, 1)], sem.at[slot]).start()
        return carry

    if unrolled:
        for r in range(MOE_BLOCK):
            body(r, 0)
    else:
        lax.fori_loop(0, MOE_BLOCK, body, 0, unroll=8)


def _moe_kernel(be_ref, act_ref, cur_ref, nxt_ref, h_hbm, wg_ref, wu_ref, wd_ref, o_ref, wgb, wub, wdb, xbuf, sem):
    i = pl.program_id(0)
    nb = pl.num_programs(0)
    slot = i % 2
    active = act_ref[i] > 0

    @pl.when(jnp.logical_and(i == 0, active))
    def _():
        _row_gather(cur_ref, h_hbm, xbuf, sem, 0, start=True)

    @pl.when(jnp.logical_or(i == 0, be_ref[i] != be_ref[jnp.maximum(i - 1, 0)]))
    def _():
        wgb[...] = wg_ref[0, 0].astype(BF16)
        wub[...] = wu_ref[0, 0].astype(BF16)
        wdb[...] = wd_ref[0, 0].astype(BF16)

    @pl.when(active)
    def _():
        _row_gather(None, h_hbm, xbuf, sem, slot, start=False)
        _row_gather(nxt_ref, h_hbm, xbuf, sem, 1 - slot, start=True, unrolled=True)
        w = xbuf[slot]
        xa = pltpu.bitcast(w & jnp.uint32(0xFFFF0000), F32)
        xb = pltpu.bitcast(lax.shift_left(w, jnp.uint32(16)), F32)
        x = jnp.concatenate([xa, xb], axis=1).astype(BF16)
        gte = jnp.dot(x, wgb[...], preferred_element_type=F32)
        up = jnp.dot(x, wub[...], preferred_element_type=F32)
        h = (gte * _sigmoid(gte) * up).astype(BF16)
        o_ref[...] = jnp.dot(h, wdb[...], preferred_element_type=F32)

    @pl.when(jnp.logical_not(active))
    def _():
        @pl.when(jnp.logical_and(i > 0, act_ref[jnp.maximum(i - 1, 0)] > 0))
        def _():
            _row_gather(None, h_hbm, xbuf, sem, slot, start=False)

        o_ref[...] = jnp.zeros_like(o_ref)


def _moe_ffn(hpk, slot_tok, block_e, active, wg, wu, wd, layer):
    D = 2 * hpk.shape[1]
    FF = wg.shape[3]
    nb = slot_tok.shape[0] // MOE_BLOCK
    once = pl.Buffered(1)
    toks = slot_tok.reshape(nb, 1, MOE_BLOCK)
    grid_spec = pltpu.PrefetchScalarGridSpec(
        num_scalar_prefetch=2,
        grid=(nb,),
        in_specs=[pl.BlockSpec((1, 1, MOE_BLOCK), lambda i, be, ac: (i, 0, 0), memory_space=pltpu.SMEM),
                  pl.BlockSpec((1, 1, MOE_BLOCK), lambda i, be, ac: (jnp.minimum(i + 1, nb - 1), 0, 0),
                               memory_space=pltpu.SMEM),
                  pl.BlockSpec(memory_space=pl.ANY),
                  pl.BlockSpec((1, 1, D, FF), lambda i, be, ac: (layer, be[i], 0, 0), pipeline_mode=once),
                  pl.BlockSpec((1, 1, D, FF), lambda i, be, ac: (layer, be[i], 0, 0), pipeline_mode=once),
                  pl.BlockSpec((1, 1, FF, D), lambda i, be, ac: (layer, be[i], 0, 0), pipeline_mode=once)],
        out_specs=pl.BlockSpec((MOE_BLOCK, D), lambda i, be, ac: (i, 0)),
        scratch_shapes=[pltpu.VMEM((D, FF), BF16), pltpu.VMEM((D, FF), BF16), pltpu.VMEM((FF, D), BF16),
                        pltpu.VMEM((2, MOE_BLOCK, D // 2), jnp.uint32), pltpu.SemaphoreType.DMA((2,))],
    )
    return pl.pallas_call(
        _moe_kernel,
        grid_spec=grid_spec,
        out_shape=jax.ShapeDtypeStruct((nb * MOE_BLOCK, D), F32),
        compiler_params=_params("arbitrary"),
        name="moe_ffn",
    )(block_e, active, toks, toks, hpk, wg, wu, wd)


def _dispatch(ids, rank, counts):
    n = ids.shape[0]
    A = 2 * n
    padded = (counts + MOE_BLOCK - 1) // MOE_BLOCK * MOE_BLOCK
    pad_end = jnp.cumsum(padded)
    pad_start = pad_end - padded
    start = jnp.cumsum(counts) - counts
    dest = jnp.take(pad_start, ids, mode="clip") + rank
    nb = -(-A // MOE_BLOCK) + N_EXPERTS + 1
    blk_start = jnp.arange(nb, dtype=jnp.int32) * MOE_BLOCK
    block_e = jnp.minimum(jnp.sum(pad_end[None, :] <= blk_start[:, None], axis=1), N_EXPERTS - 1).astype(jnp.int32)
    active = (blk_start < pad_end[-1]).astype(jnp.int32)
    order = jnp.argsort(ids.reshape(-1))
    slot_e = jnp.repeat(block_e, MOE_BLOCK)
    slot_r = jnp.arange(nb * MOE_BLOCK, dtype=jnp.int32) - jnp.take(pad_start, slot_e)
    src = jnp.clip(jnp.take(start, slot_e) + slot_r, 0, A - 1)
    slot_tok = jnp.where(slot_r < jnp.take(counts, slot_e), jnp.take(order, src) // 2, 0).astype(jnp.int32)
    return slot_tok, dest.astype(jnp.int32), block_e, active


def _final_kernel(cur_ref, nxt_ref, x_ref, yb_hbm, w_ref, gm_ref, lg_ref, lb_ref, o_ref, ybuf, sem, *, alpha, tm):
    i = pl.program_id(0)
    slot = i % 2

    def gather(tok_ref, s, unrolled):
        def body(r, carry):
            for k in range(2):
                pltpu.make_async_copy(yb_hbm.at[pl.ds(tok_ref[0, 0, k * tm + r], 1)], ybuf.at[s, k, pl.ds(r, 1)],
                                      sem.at[s]).start()
            return carry

        if unrolled:
            for r in range(tm):
                body(r, 0)
        else:
            lax.fori_loop(0, tm, body, 0, unroll=8)

    @pl.when(i == 0)
    def _():
        gather(cur_ref, 0, False)

    @pl.when(i + 1 < pl.num_programs(0))
    def _():
        gather(nxt_ref, 1 - slot, True)

    for k in range(2):
        pltpu.make_async_copy(yb_hbm.at[pl.ds(0, tm)], ybuf.at[slot, k], sem.at[slot]).wait()
    w = w_ref[...]
    f = w[:, 0:1] * ybuf[slot, 0] + w[:, 1:2] * ybuf[slot, 1]
    o_ref[...] = _ln(alpha * x_ref[...] + gm_ref[0] * f) * lg_ref[...] + lb_ref[...]


def _final(x1, yb, dest, wts, mods3, ln_g, ln_b, rows, modrow, tm, alpha):
    D = x1.shape[1]
    nt = rows // tm
    toks = jnp.swapaxes(dest.reshape(nt, tm, 2), 1, 2).reshape(nt, 1, 2 * tm)
    rowb = lambda w: pl.BlockSpec((tm, w), lambda i: (i, 0))
    full = lambda shape: pl.BlockSpec(shape, lambda i: (0,) * len(shape))
    return pl.pallas_call(
        functools.partial(_final_kernel, alpha=alpha, tm=tm),
        grid=(nt,),
        in_specs=[pl.BlockSpec((1, 1, 2 * tm), lambda i: (i, 0, 0), memory_space=pltpu.SMEM),
                  pl.BlockSpec((1, 1, 2 * tm), lambda i: (jnp.minimum(i + 1, nt - 1), 0, 0), memory_space=pltpu.SMEM),
                  rowb(D), pl.BlockSpec(memory_space=pl.ANY), rowb(LANE),
                  pl.BlockSpec((1, 1, D), lambda i: (modrow(i), 0, 5)), full((1, D)), full((1, D))],
        out_specs=rowb(D),
        out_shape=jax.ShapeDtypeStruct((rows, D), F32),
        scratch_shapes=[pltpu.VMEM((2, 2, tm, D), F32), pltpu.SemaphoreType.DMA((2,))],
        compiler_params=_params("arbitrary"),
        name="ffn_residual",
    )(toks, toks, x1, yb, wts, mods3, ln_g.reshape(1, D), ln_b.reshape(1, D))


def _rope_tables(S):
    rows = S // GRID_W
    row = jnp.repeat(jnp.arange(rows, dtype=F32), GRID_W)
    col = jnp.tile(jnp.arange(GRID_W, dtype=F32), rows)
    nf = ATT_HEAD_DIM // 4
    inv = ROPE_THETA ** (-jnp.arange(nf, dtype=F32) / nf)
    ar, ac = row[:, None] * inv, col[:, None] * inv
    cos = jnp.concatenate([jnp.cos(ar), jnp.cos(ar), jnp.cos(ac), jnp.cos(ac)], -1)
    sin = jnp.concatenate([-jnp.sin(ar), jnp.sin(ar), -jnp.sin(ac), jnp.sin(ac)], -1)
    return cos, sin


def kernel(x, c, ctx, c_ctx, w_mod, b_mod, w_in, b_gate, q_norm, k_norm, w_att_o, rwkv_mu, rwkv_w0, rwkv_w2, rwkv_a0, rwkv_a2, rwkv_g2, rwkv_k_k, rwkv_k_a, rwkv_r_k, rwkv_ln_g, rwkv_ln_b, w_rwkv_o, conv_w, conv_b, conv_ln_g, conv_ln_b, w_conv_o, w_out, ln1_g, ln1_b, w_group, b_group, w_router, b_router, w_e_gate, w_e_up, w_e_down, ln2_g, ln2_b):
    B, S, D = x.shape
    Tc = ctx.shape[1]
    depth = w_mod.shape[0]
    NL, NC = B * S, B * Tc
    N = NL + NC
    alpha = (2 * depth) ** 0.25
    att_w = w_att_o.shape[1]
    heads = att_w // ATT_HEAD_DIM
    rw_in = rwkv_mu.shape[-1]
    conv_cw = w_conv_o.shape[1]
    kv_w = (w_in.shape[-1] - att_w - rw_in - 2 * conv_cw - 3 * D) // 2
    assert kv_w == heads // ATT_GROUP * ATT_HEAD_DIM
    o_rw = att_w + 2 * kv_w
    o_cv = o_rw + rw_in
    o_gt = o_cv + 2 * conv_cw

    tmod = _tile(math.gcd(S, Tc), 256)
    tps = S // tmod
    nlt = NL // tmod
    modrow = lambda i: jnp.where(i < nlt, i // tps, B)
    cos_t, sin_t = _rope_tables(S)

    R = -(-(B + 1) // SUBLANE) * SUBLANE
    cv = jnp.zeros((R, D), F32).at[:B].set(c).at[B].set(c_ctx)
    src = (x.reshape(NL, D), ctx.reshape(NC, D), nlt, 0)

    for l in range(depth):
        last = l == depth - 1
        rows = NL if last else N
        p = dict(rwkv_mu=rwkv_mu[l], rwkv_w0=rwkv_w0[l], rwkv_w2=rwkv_w2[l], rwkv_a0=rwkv_a0[l], rwkv_a2=rwkv_a2[l],
                 rwkv_g2=rwkv_g2[l], rwkv_k_k=rwkv_k_k[l], rwkv_k_a=rwkv_k_a[l], rwkv_r_k=rwkv_r_k[l],
                 rwkv_ln_g=rwkv_ln_g[l], rwkv_ln_b=rwkv_ln_b[l], conv_w=conv_w[l], conv_b=conv_b[l],
                 conv_ln_g=conv_ln_g[l], conv_ln_b=conv_ln_b[l])
        mods3 = _adaln(cv, w_mod, b_mod, l).reshape(R, 1, 6 * D)
        wl = w_in[l]
        h = _lnmod(src, mods3, 0, 1, N, modrow, tmod)
        qkv = _matmul(h, wl[:, :o_rw].astype(BF16), 512)
        rw = _matmul(h, wl[:, o_rw:o_cv].astype(BF16), 1152)
        cvx = _matmul(h, wl[:, o_cv:o_gt].astype(BF16), 1024, rows)
        gt = _matmul(h, wl[:, o_gt:].astype(BF16), 1024, rows)

        qn, kn = q_norm[l].reshape(1, LANE), k_norm[l].reshape(1, LANE)
        att = _attention(qkv, qn, kn, cos_t, sin_t, B=B, S=S, Tc=Tc, heads=heads, latent_queries=True)
        if not last:
            att_c = _attention(qkv, qn, kn, cos_t, sin_t, B=B, S=S, Tc=Tc, heads=heads, latent_queries=False)
            att = jnp.concatenate([att, att_c], 0)

        r, v, kk, kd, lw, bb, g, bonus = _rwprep(rw, p, B=B, S=S, Tc=Tc)
        Am, Bm, Qm, Y0 = _scanprep(r, v, kk, kd, lw, bb)
        y2 = _scanserial(Am, Bm, Qm, Y0, B=B, S=S, Tc=Tc)
        rwo = _rwout(y2, bonus, g, p, rows)
        cvo = _conv(cvx, p, B=B, S=S, Tc=Tc, rows=rows)

        m = _merge(att, rwo, cvo, gt, w_att_o[l].astype(BF16), w_rwkv_o[l].astype(BF16), w_conv_o[l].astype(BF16),
                   b_gate[l], rows)
        w_rt = jnp.zeros((D, LANE), F32).at[:, :N_EXPERTS].set(w_router[l]).at[:, N_EXPERTS:N_EXPERTS + N_GROUPS].set(w_group[l])
        b_rt = jnp.zeros((1, LANE), F32).at[0, :N_EXPERTS].set(b_router[l]).at[0, N_EXPERTS:N_EXPERTS + N_GROUPS].set(b_group[l])
        w_rt_hi = w_rt.astype(BF16)
        w_rt = jnp.concatenate([w_rt_hi, (w_rt - w_rt_hi.astype(F32)).astype(BF16)], 1)
        x1, h2, logits = _outproj(m, w_out[l].astype(BF16), src, mods3, ln1_g[l], ln1_b[l], w_rt, b_rt, rows, modrow,
                                  tmod, alpha)

        ids128, wts128, cnt128 = _route(logits)
        slot_tok, dest, block_e, active = _dispatch(ids128[:, :2], ids128[:, 2:4], cnt128[0, :N_EXPERTS].astype(jnp.int32))
        yb = _moe_ffn(h2, slot_tok, block_e, active, w_e_gate, w_e_up, w_e_down, l)
        xa = _final(x1, yb, dest, wts128, mods3, ln2_g[l], ln2_b[l], rows, modrow, tmod, alpha)
        src = (xa, xa, nlt, nlt)

    return xa[:NL].reshape(B, S, D)
```

```python
import functools
import math

import numpy as np
import jax
import jax.numpy as jnp
from jax import lax
from jax.experimental import pallas as pl
from jax.experimental.pallas import tpu as pltpu

F32 = jnp.float32
BF16 = jnp.bfloat16
HI = lax.Precision.HIGHEST

LANE = 128
SUBLANE = 8
ATT_HEAD_DIM = 128
ATT_GROUP = 4
ATT_HEADS_PER_STEP = 2
ROPE_THETA = 10000.0
GRID_W = 64
RWKV_HEAD_DIM = 64
CHUNK = 64
CONV_HALO = 16
SHIFT_HALO = 8
N_GROUPS = 4
EXPERTS_PER_GROUP = 8
N_EXPERTS = N_GROUPS * EXPERTS_PER_GROUP
MOE_BLOCK = 256
GN_EPS = 64e-5
LN_EPS = 1e-6
RMS_EPS = 1e-6
VMEM_LIMIT = 56 * 1024 * 1024

NT_DIMS = (((1,), (1,)), ((), ()))
TN_DIMS = (((0,), (0,)), ((), ()))


def _params(*sem):
    return pltpu.CompilerParams(dimension_semantics=sem, vmem_limit_bytes=VMEM_LIMIT)


def _sigmoid(x):
    return 1.0 / (1.0 + jnp.exp(-x))


def _ln(x):
    mu = jnp.mean(x, -1, keepdims=True)
    xc = x - mu
    var = jnp.mean(xc * xc, -1, keepdims=True)
    return xc * lax.rsqrt(var + LN_EPS)


def _tile(n, pref):
    t = min(n, pref)
    while n % t:
        t -= SUBLANE
    return t


def _adaln_kernel(c_ref, w_ref, b_ref, o_ref):
    c = c_ref[...]
    s = c * _sigmoid(c)
    o_ref[...] = jnp.dot(s, w_ref[0], precision=HI, preferred_element_type=F32) + b_ref[0]


def _adaln(cv, w_mod, b_mod, layer):
    R, D = cv.shape
    L, _, W = w_mod.shape
    tn = _tile(W, 1024)
    return pl.pallas_call(
        _adaln_kernel,
        grid=(W // tn,),
        in_specs=[pl.BlockSpec((R, D), lambda j: (0, 0)),
                  pl.BlockSpec((1, D, tn), lambda j: (layer, 0, j)),
                  pl.BlockSpec((1, 1, tn), lambda j: (layer, 0, j))],
        out_specs=pl.BlockSpec((R, tn), lambda j: (0, j)),
        out_shape=jax.ShapeDtypeStruct((R, W), F32),
        compiler_params=_params("parallel"),
        name="adaln",
    )(cv, w_mod, b_mod.reshape(L, 1, W))


def _stream_specs(src, tm, D):
    lat, tail, nlt, off = src
    return [lat, tail], [pl.BlockSpec((tm, D), lambda i: (jnp.minimum(i, nlt - 1), 0)),
                         pl.BlockSpec((tm, D), lambda i: (jnp.maximum(i - nlt, 0) + off, 0))]


def _lnmod_kernel(xl_ref, xt_ref, sh_ref, sc_ref, o_ref, *, nlt):
    x = jnp.where(pl.program_id(0) < nlt, xl_ref[...], xt_ref[...])
    o_ref[...] = (_ln(x) * (1.0 + sc_ref[0]) + sh_ref[0]).astype(o_ref.dtype)


def _lnmod(src, mods3, shift_blk, scale_blk, rows, modrow, tm):
    D = src[0].shape[1]
    arrs, specs = _stream_specs(src, tm, D)
    return pl.pallas_call(
        functools.partial(_lnmod_kernel, nlt=src[2]),
        grid=(rows // tm,),
        in_specs=specs + [pl.BlockSpec((1, 1, D), lambda i: (modrow(i), 0, shift_blk)),
                          pl.BlockSpec((1, 1, D), lambda i: (modrow(i), 0, scale_blk))],
        out_specs=pl.BlockSpec((tm, D), lambda i: (i, 0)),
        out_shape=jax.ShapeDtypeStruct((rows, D), BF16),
        compiler_params=_params("parallel"),
        name="lnmod",
    )(*arrs, mods3, mods3)


def _mm_kernel(x_ref, w_ref, o_ref):
    o_ref[...] = jnp.dot(x_ref[...], w_ref[...], preferred_element_type=F32).astype(o_ref.dtype)


def _matmul(x, w, tn_pref, rows=None, out_dtype=F32, tm_pref=2048):
    M, K = x.shape
    M = rows or M
    N = w.shape[1]
    tm = _tile(M, tm_pref)
    tn = _tile(N, tn_pref)
    return pl.pallas_call(
        _mm_kernel,
        grid=(N // tn, M // tm),
        in_specs=[pl.BlockSpec((tm, K), lambda j, i: (i, 0)),
                  pl.BlockSpec((K, tn), lambda j, i: (0, j))],
        out_specs=pl.BlockSpec((tm, tn), lambda j, i: (i, j)),
        out_shape=jax.ShapeDtypeStruct((M, N), out_dtype),
        compiler_params=_params("parallel", "parallel"),
        name="matmul",
    )(x, w)


def _rms(t, g):
    return t * lax.rsqrt(jnp.mean(t * t, -1, keepdims=True) + RMS_EPS) * g


def _rope(t, cos, sin_signed):
    lane = lax.broadcasted_iota(jnp.int32, t.shape, 1)
    first = (lane & 63) < 32
    partner = jnp.where(first, pltpu.roll(t, LANE - 32, 1), pltpu.roll(t, 32, 1))
    return t * cos + partner * sin_signed


def _attn_kernel(*refs, rope, has_lat, tq, scale, hg):
    if has_lat:
        (q_ref, kc_ref, vc_ref, qn_ref, kn_ref, kl_ref, vl_ref, cq_ref, sq_ref, ck_ref, sk_ref,
         o_ref, kcb, vcb, klb, vlb) = refs
    else:
        q_ref, kc_ref, vc_ref, qn_ref, kn_ref, o_ref, kcb, vcb = refs

    @pl.when(pl.program_id(2) == 0)
    def _():
        kcb[...] = _rms(kc_ref[...], kn_ref[...]).astype(BF16)
        vcb[:, :LANE] = vc_ref[...].astype(BF16)
        vcb[:, LANE:] = jnp.ones(vc_ref.shape, BF16)
        if has_lat:
            klb[...] = _rope(_rms(kl_ref[...], kn_ref[...]), ck_ref[...], sk_ref[...]).astype(BF16)
            vlb[:, :LANE] = vl_ref[...].astype(BF16)
            vlb[:, LANE:] = jnp.ones(vl_ref.shape, BF16)

    groups = [range(g0, g0 + hg) for g0 in range(0, ATT_GROUP, hg)]
    scores = {}

    def qk(gi):
        qs = []
        for g in groups[gi]:
            qg = _rms(q_ref[:, g * LANE:(g + 1) * LANE], qn_ref[...])
            if rope:
                qg = _rope(qg, cq_ref[...], sq_ref[...])
            qs.append((qg * (scale * math.log2(math.e))).astype(BF16))
        qa = jnp.concatenate(qs, axis=0)
        sc = lax.dot_general(qa, kcb[...], NT_DIMS, preferred_element_type=F32)
        sl = lax.dot_general(qa, klb[...], NT_DIMS, preferred_element_type=F32) if has_lat else None
        scores[gi] = (sc, sl)

    def softmax_pv(gi):
        sc, sl = scores.pop(gi)
        m = jnp.max(sc, -1, keepdims=True)
        if has_lat:
            m = jnp.maximum(m, jnp.max(sl, -1, keepdims=True))
        od = jnp.dot(jnp.exp2(sc - m).astype(BF16), vcb[...], preferred_element_type=F32)
        if has_lat:
            od = od + jnp.dot(jnp.exp2(sl - m).astype(BF16), vlb[...], preferred_element_type=F32)
        o = od[:, :LANE] / od[:, LANE:]
        for n, g in enumerate(groups[gi]):
            o_ref[:, g * LANE:(g + 1) * LANE] = o[n * tq:(n + 1) * tq].astype(o_ref.dtype)

    qk(0)
    for gi in range(1, len(groups)):
        qk(gi)
        softmax_pv(gi - 1)
    softmax_pv(len(groups) - 1)


def _attention(qkv, qn, kn, cos_t, sin_t, *, B, S, Tc, heads, latent_queries):
    NL = B * S
    kvh = heads // ATT_GROUP
    GW = ATT_GROUP * LANE
    scale = ATT_HEAD_DIM ** -0.5
    Tq = S if latent_queries else Tc
    tq = _tile(Tq, 256)
    nq = Tq // tq
    qbase = 0 if latent_queries else NL // tq
    cbase = NL // Tc
    common = [pl.BlockSpec((tq, GW), lambda b, j, i: (qbase + b * nq + i, j)),
              pl.BlockSpec((Tc, LANE), lambda b, j, i: (cbase + b, heads + j)),
              pl.BlockSpec((Tc, LANE), lambda b, j, i: (cbase + b, heads + kvh + j)),
              pl.BlockSpec((1, LANE), lambda b, j, i: (0, 0)),
              pl.BlockSpec((1, LANE), lambda b, j, i: (0, 0))]
    args = [qkv, qkv, qkv, qn, kn]
    scratch = [pltpu.VMEM((Tc, LANE), BF16), pltpu.VMEM((Tc, 2 * LANE), BF16)]
    if latent_queries:
        common += [pl.BlockSpec((S, LANE), lambda b, j, i: (b, heads + j)),
                   pl.BlockSpec((S, LANE), lambda b, j, i: (b, heads + kvh + j)),
                   pl.BlockSpec((tq, LANE), lambda b, j, i: (i, 0)),
                   pl.BlockSpec((tq, LANE), lambda b, j, i: (i, 0)),
                   pl.BlockSpec((S, LANE), lambda b, j, i: (0, 0)),
                   pl.BlockSpec((S, LANE), lambda b, j, i: (0, 0))]
        args += [qkv, qkv, cos_t, sin_t, cos_t, sin_t]
        scratch += [pltpu.VMEM((S, LANE), BF16), pltpu.VMEM((S, 2 * LANE), BF16)]
    return pl.pallas_call(
        functools.partial(_attn_kernel, rope=latent_queries, has_lat=latent_queries, tq=tq, scale=scale, hg=ATT_HEADS_PER_STEP),
        grid=(B, kvh, nq),
        in_specs=common,
        out_specs=pl.BlockSpec((tq, GW), lambda b, j, i: (b * nq + i, j)),
        out_shape=jax.ShapeDtypeStruct((B * Tq, heads * LANE), BF16),
        scratch_shapes=scratch,
        compiler_params=_params("parallel", "parallel", "arbitrary"),
        name="attention_lat" if latent_queries else "attention_ctx",
    )(*args)


def _segsum(x, seg_ref, segt_ref):
    return _split_dot(_split_dot(x, seg_ref[...]), segt_ref[...])


def _split_dot(x, w):
    hi = x.astype(BF16)
    lo = (x - hi.astype(F32)).astype(BF16)
    return jnp.dot(hi, w, preferred_element_type=F32) + jnp.dot(lo, w, preferred_element_type=F32)


def _rwprep_kernel(first_ref, last_ref, z_ref, zp_ref, zn_ref, mu_ref, w0_ref, w2_ref, a0_ref, a2_ref, g2_ref,
                   kkp_ref, ka_ref, rk_ref, seg_ref, segt_ref,
                   r_o, v_o, kk_o, kd_o, lw_o, bb_o, g_o, bon_o, *, RW, R2W, R2A, RG):
    i = pl.program_id(0)
    has_prev = (1 - first_ref[i]).astype(F32)
    has_next = (1 - last_ref[i]).astype(F32)
    tt = z_ref.shape[0]
    row8 = lax.broadcasted_iota(jnp.int32, (SUBLANE, 1), 0)

    def shifted(lo, hi):
        z = z_ref[:, lo:hi]
        hp = zp_ref[SHIFT_HALO - 1:SHIFT_HALO, lo:hi] * has_prev
        hn = zn_ref[0:1, lo:hi] * has_next
        prev = pltpu.roll(z, 1, 0)
        nxt = pltpu.roll(z, tt - 1, 0)
        prev = jnp.concatenate([jnp.where(row8 == 0, hp, prev[:SUBLANE]), prev[SUBLANE:]], axis=0)
        nxt = jnp.concatenate([nxt[:tt - SUBLANE], jnp.where(row8 == SUBLANE - 1, hn, nxt[tt - SUBLANE:])], axis=0)
        m0, m1 = mu_ref[0:1, lo:hi], mu_ref[1:2, lo:hi]
        return (1.0 - m0 - m1) * z + m0 * prev + m1 * nxt

    o = 3 * RW
    r = shifted(0, RW)
    k = shifted(RW, 2 * RW)
    v = shifted(2 * RW, o)
    wl = shifted(o, o + R2W)
    al = shifted(o + R2W, o + R2W + R2A)
    gl = shifted(o + R2W + R2A, o + R2W + R2A + RG)

    u = -(w0_ref[...] + _bdot(jnp.tanh(wl), w2_ref[...]))
    softplus = jnp.maximum(u, 0.0) + jnp.log(1.0 + jnp.exp(-jnp.abs(u)))
    lw_o[...] = -jnp.exp(-softplus - 0.5)
    a = _sigmoid(a0_ref[...] + _bdot(al, a2_ref[...]))
    g_o[...] = _bdot(_sigmoid(gl), g2_ref[...])

    kk = k * kkp_ref[...]
    kk = kk / jnp.maximum(jnp.sqrt(_segsum(kk * kk, seg_ref, segt_ref)), 1e-12)
    ka = ka_ref[...]
    kd0 = k * (1.0 + (a[:, :RW] - 1.0) * ka)
    kd1 = k * (1.0 + (a[:, RW:] - 1.0) * ka)
    r_o[...] = r
    v_o[...] = v
    kk_o[...] = kk
    kd_o[:, :RW] = kd0
    kd_o[:, RW:] = kd1
    bb_o[:, :RW] = kk * a[:, :RW]
    bb_o[:, RW:] = kk * a[:, RW:]
    bon_o[...] = _segsum(r * rk_ref[...] * (kd0 + kd1), seg_ref, segt_ref) * v


def _blockdiag2(w):
    z = jnp.zeros_like(w[0])
    return jnp.concatenate([jnp.concatenate([w[0], z], 1), jnp.concatenate([z, w[1]], 1)], 0)


def _seq_edges(B, S, Tc, tt):
    first, last = [], []
    for n, L in ((B, S), (B, Tc)):
        per = L // tt
        for _ in range(n):
            first += [1] + [0] * (per - 1)
            last += [0] * (per - 1) + [1]
    return jnp.asarray(np.array(first, np.int32)), jnp.asarray(np.array(last, np.int32))


def _rwprep(rw, p, *, B, S, Tc):
    N, RWIN = rw.shape
    RW = p["rwkv_k_k"].shape[0]
    R2W = 2 * p["rwkv_w2"].shape[1]
    R2A = 2 * p["rwkv_a2"].shape[1]
    RG = p["rwkv_g2"].shape[0]
    tt = _tile(math.gcd(S, Tc), 128)
    nt = N // tt
    hb = tt // SHIFT_HALO
    nhb = N // SHIFT_HALO
    first, last = _seq_edges(B, S, Tc, tt)
    nh = RW // RWKV_HEAD_DIM
    seg = (jnp.arange(RW)[:, None] // RWKV_HEAD_DIM == jnp.arange(LANE)[None, :]).astype(BF16)
    full = lambda shape: pl.BlockSpec(shape, lambda i, f, l: (0,) * len(shape))
    row = lambda w: pl.BlockSpec((tt, w), lambda i, f, l: (i, 0))
    out_w = [RW, RW, RW, 2 * RW, 2 * RW, 2 * RW, RW, RW]
    assert nh <= LANE
    grid_spec = pltpu.PrefetchScalarGridSpec(
        num_scalar_prefetch=2,
        grid=(nt,),
        in_specs=[row(RWIN),
                  pl.BlockSpec((SHIFT_HALO, RWIN), lambda i, f, l: (jnp.maximum(i * hb - 1, 0), 0)),
                  pl.BlockSpec((SHIFT_HALO, RWIN), lambda i, f, l: (jnp.minimum((i + 1) * hb, nhb - 1), 0)),
                  full((2, RWIN)), full((1, 2 * RW)), full((R2W, 2 * RW)), full((1, 2 * RW)), full((R2A, 2 * RW)),
                  full((RG, RW)), full((1, RW)), full((1, RW)), full((1, RW)), full((RW, LANE)), full((LANE, RW))],
        out_specs=[row(w) for w in out_w],
    )
    return pl.pallas_call(
        functools.partial(_rwprep_kernel, RW=RW, R2W=R2W, R2A=R2A, RG=RG),
        grid_spec=grid_spec,
        out_shape=[jax.ShapeDtypeStruct((N, w), F32) for w in out_w],
        compiler_params=_params("parallel"),
        name="rwkv_prep",
    )(first, last, rw, rw, rw, p["rwkv_mu"], p["rwkv_w0"].reshape(1, 2 * RW), _blockdiag2(p["rwkv_w2"]),
      p["rwkv_a0"].reshape(1, 2 * RW), _blockdiag2(p["rwkv_a2"]), p["rwkv_g2"],
      p["rwkv_k_k"].reshape(1, RW), p["rwkv_k_a"].reshape(1, RW), p["rwkv_r_k"].reshape(1, RW), seg, seg.T)


def _bdot(a, b):
    return jnp.dot(a.astype(BF16), b.astype(BF16), preferred_element_type=F32)


def _scanprep_kernel(r_ref, v_ref, kk_ref, kd_ref, lw_ref, bb_ref, A_o, B_o, Q_o, Y_o, *, G, ngrp):
    C = CHUNK
    HD = RWKV_HEAD_DIM
    shift = int(math.log2(C))
    hshift = int(math.log2(HD))
    sign = 1 - 2 * pl.program_id(0)
    ti = lax.broadcasted_iota(jnp.int32, (G, G), 0)
    si = lax.broadcasted_iota(jnp.int32, (G, G), 1)
    same = jnp.right_shift(ti, shift) == jnp.right_shift(si, shift)
    strict = jnp.logical_and(same, sign * (ti - si) > 0)
    ti2 = lax.broadcasted_iota(jnp.int32, (G, 2 * G), 0)
    si2 = jnp.bitwise_and(lax.broadcasted_iota(jnp.int32, (G, 2 * G), 1), G - 1)
    incl2 = jnp.logical_and(jnp.right_shift(ti2, shift) == jnp.right_shift(si2, shift), sign * (ti2 - si2) >= 0)
    eye = (si == ti).astype(F32)
    head0 = lax.broadcasted_iota(jnp.int32, (G, LANE), 1) < HD
    hi2 = lax.broadcasted_iota(jnp.int32, (LANE, LANE), 0)
    hj2 = lax.broadcasted_iota(jnp.int32, (LANE, LANE), 1)
    same_head = jnp.right_shift(hi2, hshift) == jnp.right_shift(hj2, hshift)
    eye_h = (hi2 == hj2).astype(F32)
    sel = lambda a0, a1: jnp.where(head0, a0, a1)
    both = (0, 1)
    fwd = (pl.program_id(0) == 0).astype(F32)
    trow = jnp.bitwise_and(lax.broadcasted_iota(jnp.int32, (G, LANE), 0), C - 1)

    st = []
    for g in range(ngrp):
        rows = slice(g * G, (g + 1) * G)
        lw = lw_ref[rows, :]
        F = lw
        for k in (1 << i for i in range(shift)):
            F = F + jnp.where(trow >= k, pltpu.roll(F, k, 0), 0.0)
        tot = jnp.concatenate([jnp.broadcast_to(F[(c + 1) * C - 1:(c + 1) * C], (C, LANE)) for c in range(G // C)], axis=0)
        P = fwd * F + (1.0 - fwd) * (tot - F + lw)
        eP = jnp.exp(P)
        enP = jnp.exp(-P)
        at = -kk_ref[rows, :] * jnp.exp(P - lw)
        bt = (bb_ref[rows, :] * enP).astype(BF16)
        kt = (kd_ref[rows, :] * enP).astype(BF16)
        rt = r_ref[rows, :] * eP
        vv = v_ref[rows, :].astype(BF16)
        rest = jnp.exp(tot - P)
        bk = jnp.concatenate([bt, kt], axis=0)
        L, lak, grbk = [], [], []
        for h in both:
            hm = head0 if h == 0 else jnp.logical_not(head0)
            ar = jnp.concatenate([jnp.where(hm, at, 0.0), jnp.where(hm, rt, 0.0)], axis=0).astype(BF16)
            gm = lax.dot_general(ar, bk, NT_DIMS, preferred_element_type=F32)
            L.append(jnp.where(strict, gm[:G, :G], 0.0))
            lak.append(jnp.where(strict, gm[:G, G:], 0.0).astype(BF16))
            grbk.append(jnp.where(incl2, gm[G:], 0.0).astype(BF16))
        st.append(dict(at=at.astype(BF16), rt=rt, vv=vv, L=L, lak=lak, grbk=grbk,
                       bkd=[jnp.concatenate([(bb_ref[rows, :] * rest)[c * C:(c + 1) * C],
                                             (kd_ref[rows, :] * rest)[c * C:(c + 1) * C]], axis=0).astype(BF16)
                            for c in range(G // C)],
                       dec=jnp.exp(tot)))

    for s in st:
        s["T"] = [eye + s["L"][h] for h in both]
        lb = [s["L"][h].astype(BF16) for h in both]
        s["Lp"] = [jnp.dot(lb[h], lb[h], preferred_element_type=F32).astype(BF16) for h in both]
    for lvl in range(1, shift):
        for s in st:
            for h in both:
                if lvl < shift - 1:
                    x = jnp.concatenate([s["Lp"][h], s["T"][h].astype(BF16)], axis=1)
                    res = jnp.dot(s["Lp"][h], x, preferred_element_type=F32)
                    s["Lp"][h] = res[:, :G].astype(BF16)
                    s["T"][h] = s["T"][h] + res[:, G:]
                else:
                    s["T"][h] = s["T"][h] + jnp.dot(s["Lp"][h], s["T"][h].astype(BF16), preferred_element_type=F32)
    for s in st:
        s["lv"] = sel(*[jnp.dot(s["lak"][h], s["vv"], preferred_element_type=F32) for h in both]).astype(BF16)
    for s in st:
        x = jnp.concatenate([s["at"], s["lv"]], axis=1)
        tw = [jnp.dot(s["T"][h].astype(BF16), x, preferred_element_type=F32) for h in both]
        s["tw"] = jnp.concatenate([sel(tw[0][:, :LANE], tw[1][:, :LANE]), sel(tw[0][:, LANE:], tw[1][:, LANE:])],
                                  axis=1).astype(BF16)
    for g, s in enumerate(st):
        rows = slice(g * G, (g + 1) * G)
        zv = jnp.concatenate([jnp.zeros((G, LANE), BF16), s["vv"]], axis=1)
        rhs = jnp.concatenate([s["tw"], zv], axis=0)
        qy = [jnp.dot(s["grbk"][h], rhs, preferred_element_type=F32) for h in both]
        Q_o[0, rows, :] = (s["rt"] + sel(qy[0][:, :LANE], qy[1][:, :LANE])).astype(Q_o.dtype)
        Y_o[0, rows, :] = sel(qy[0][:, LANE:], qy[1][:, LANE:])
        for c in range(G // C):
            cr = slice(c * C, (c + 1) * C)
            rc = jnp.concatenate([s["tw"][cr], zv[cr]], axis=0)
            ab = lax.dot_general(s["bkd"][c], rc, TN_DIMS, preferred_element_type=F32)
            k = g * (G // C) + c
            A_o[0, k, 0] = jnp.where(same_head, eye_h * s["dec"][c * C:c * C + 1] + ab[:, :LANE], 0.0).astype(A_o.dtype)
            B_o[0, k, 0] = jnp.where(same_head, ab[:, LANE:], 0.0)


def _scanprep(r, v, kk, kd, lw, bb):
    N, RW = r.shape
    hp = RW // LANE
    G = _tile(N, 2 * CHUNK)
    ngrp = max(n for n in (12, 8, 4, 2, 1) if N % (n * G) == 0)
    rb = G * ngrp
    nchunks = N // CHUNK
    shared = lambda: pl.BlockSpec((rb, LANE), lambda d, i, h: (i, h))
    perdir = lambda: pl.BlockSpec((rb, LANE), lambda d, i, h: (i, d * hp + h))
    rowo = lambda: pl.BlockSpec((1, rb, LANE), lambda d, i, h: (d, i, h))
    mato = lambda: pl.BlockSpec((1, rb // CHUNK, 1, LANE, LANE), lambda d, i, h: (d, i, h, 0, 0))
    return pl.pallas_call(
        functools.partial(_scanprep_kernel, G=G, ngrp=ngrp),
        grid=(2, N // rb, hp),
        in_specs=[shared(), shared(), shared(), perdir(), perdir(), perdir()],
        out_specs=[mato(), mato(), rowo(), rowo()],
        out_shape=[jax.ShapeDtypeStruct((2, nchunks, hp, LANE, LANE), BF16),
                   jax.ShapeDtypeStruct((2, nchunks, hp, LANE, LANE), F32),
                   jax.ShapeDtypeStruct((2, N, RW), BF16),
                   jax.ShapeDtypeStruct((2, N, RW), F32)],
        compiler_params=_params("parallel", "parallel", "parallel"),
        name="rwkv_chunk_maps",
    )(r, v, kk, kd, lw, bb)


def _scanserial_kernel(A_ref, B_ref, Q_ref, Y0_ref, y_ref, s_ref, *, npair, cb):
    @pl.when(pl.program_id(2) == 0)
    def _():
        s_ref[...] = jnp.zeros_like(s_ref)

    C = CHUNK
    d = pl.program_id(0)
    states = [s_ref[h] for h in range(npair)]
    for k in range(cb):
        c = jnp.where(d == 0, k, cb - 1 - k)
        rows = pl.ds(pl.multiple_of(c * C, C), C)
        for h in range(npair):
            ls = slice(h * LANE, (h + 1) * LANE)
            aq = jnp.concatenate([A_ref[0, c, h], Q_ref[0, rows, ls]], axis=0)
            res = jnp.dot(aq, states[h].astype(BF16), preferred_element_type=F32)
            y_ref[0, rows, ls] = res[LANE:] + Y0_ref[0, rows, ls]
            states[h] = res[:LANE] + B_ref[0, c, h]
    for h in range(npair):
        s_ref[h] = states[h]


def _scanserial(A, Bm, Q, Y0, *, B, S, Tc):
    _, nchunks, npair, _, _ = A.shape
    C = CHUNK
    N = nchunks * C
    RW = npair * LANE
    cb = math.gcd(math.gcd(S // C, Tc // C), 4)
    nl, nc = S // (C * cb), Tc // (C * cb)
    cbase = B * nl

    def chunk_of(d, b, p):
        kc = jnp.where(d == 0, p, nc - 1 - p)
        kl = jnp.where(d == 0, p - nc, nl - 1 - (p - nc))
        return jnp.where(p < nc, cbase + b * nc + kc, b * nl + kl)

    mat = lambda: pl.BlockSpec((1, cb, npair, LANE, LANE), lambda d, b, p: (d, chunk_of(d, b, p), 0, 0, 0))
    row = lambda: pl.BlockSpec((1, cb * C, RW), lambda d, b, p: (d, chunk_of(d, b, p), 0))
    return pl.pallas_call(
        functools.partial(_scanserial_kernel, npair=npair, cb=cb),
        grid=(2, B, nl + nc),
        in_specs=[mat(), mat(), row(), row()],
        out_specs=row(),
        out_shape=jax.ShapeDtypeStruct((2, N, RW), F32),
        scratch_shapes=[pltpu.VMEM((npair, LANE, LANE), F32)],
        compiler_params=_params("parallel", "parallel", "arbitrary"),
        name="rwkv_chunk_chain",
    )(A, Bm, Q, Y0)


def _rwout_kernel(y_ref, bon_ref, g_ref, lg_ref, lb_ref, seg_ref, segt_ref, o_ref):
    y = y_ref[0] + y_ref[1]
    inv = 1.0 / RWKV_HEAD_DIM
    mu = _segsum(y, seg_ref, segt_ref) * inv
    yc = y - mu
    var = _segsum(yc * yc, seg_ref, segt_ref) * inv
    yn = yc * lax.rsqrt(var + GN_EPS) * lg_ref[...] + lb_ref[...]
    o_ref[...] = ((yn + bon_ref[...]) * g_ref[...]).astype(o_ref.dtype)


def _rwout(y2, bonus, g, p, rows):
    RW = g.shape[1]
    tm = _tile(rows, 256)
    seg = (jnp.arange(RW)[:, None] // RWKV_HEAD_DIM == jnp.arange(LANE)[None, :]).astype(BF16)
    full = lambda shape: pl.BlockSpec(shape, lambda i: (0,) * len(shape))
    return pl.pallas_call(
        _rwout_kernel,
        grid=(rows // tm,),
        in_specs=[pl.BlockSpec((2, tm, RW), lambda i: (0, i, 0)),
                  pl.BlockSpec((tm, RW), lambda i: (i, 0)),
                  pl.BlockSpec((tm, RW), lambda i: (i, 0)),
                  full((1, RW)), full((1, RW)), full((RW, LANE)), full((LANE, RW))],
        out_specs=pl.BlockSpec((tm, RW), lambda i: (i, 0)),
        out_shape=jax.ShapeDtypeStruct((rows, RW), BF16),
        compiler_params=_params("parallel"),
        name="rwkv_out",
    )(y2, bonus, g, p["rwkv_ln_g"].reshape(1, RW), p["rwkv_ln_b"].reshape(1, RW), seg, seg.T)


CONV_ROWS = 64


def _conv_kernel(first_ref, last_ref, z_ref, zp_ref, zn_ref, w_ref, b_ref, lg_ref, lb_ref, o_ref, u_ref, c_ref, *, CW, K):
    i = pl.program_id(0)
    has_prev = (1 - first_ref[i]).astype(F32)
    has_next = (1 - last_ref[i]).astype(F32)
    tt = z_ref.shape[0]
    H = CONV_HALO
    nlt = CW // LANE
    glu = lambda ref, lt: ref[:, lt * LANE:(lt + 1) * LANE] * _sigmoid(ref[:, CW + lt * LANE:CW + (lt + 1) * LANE])
    span = tt + SUBLANE * ((H + K // 2) // SUBLANE)
    for lt in range(nlt):
        u_ref[0, lt, 0:H, :] = glu(zp_ref, lt) * has_prev
        u_ref[0, lt, H:H + tt, :] = glu(z_ref, lt)
        u_ref[0, lt, H + tt:, :] = glu(zn_ref, lt) * has_next
        for s in range(1, SUBLANE):
            u_ref[s, lt, 0:span, :] = u_ref[0, lt, s:s + span, :]
    off = H - K // 2
    rb = min(CONV_ROWS, tt)
    nrb = tt // rb

    def block(idx, carry):
        lt = idx // nrb
        r0 = pl.multiple_of((idx % nrb) * rb, rb)
        acc = jnp.zeros((rb, LANE), F32) + b_ref[lt]
        for k in range(K):
            s, j = (off + k) % SUBLANE, (off + k) // SUBLANE
            acc = acc + w_ref[k, lt] * u_ref[s, lt, pl.ds(r0 + SUBLANE * j, rb), :]
        c_ref[lt, pl.ds(r0, rb), :] = acc
        return carry

    lax.fori_loop(0, nlt * nrb, block, 0)
    y = _ln(jnp.concatenate([c_ref[lt] for lt in range(nlt)], axis=1)) * lg_ref[...] + lb_ref[...]
    o_ref[...] = (y * _sigmoid(y)).astype(o_ref.dtype)


def _conv(cv, p, *, B, S, Tc, rows):
    CW = cv.shape[1] // 2
    K = p["conv_w"].shape[0]
    assert K // 2 <= CONV_HALO
    nlt = CW // LANE
    tt = _tile(math.gcd(S, Tc), 256)
    hb = tt // CONV_HALO
    nhb = cv.shape[0] // CONV_HALO
    first, last = _seq_edges(B, S, Tc, tt)
    full = lambda shape: pl.BlockSpec(shape, lambda i, f, l: (0,) * len(shape))
    grid_spec = pltpu.PrefetchScalarGridSpec(
        num_scalar_prefetch=2,
        grid=(rows // tt,),
        in_specs=[pl.BlockSpec((tt, 2 * CW), lambda i, f, l: (i, 0)),
                  pl.BlockSpec((CONV_HALO, 2 * CW), lambda i, f, l: (jnp.maximum(i * hb - 1, 0), 0)),
                  pl.BlockSpec((CONV_HALO, 2 * CW), lambda i, f, l: (jnp.minimum((i + 1) * hb, nhb - 1), 0)),
                  full((K, nlt, 1, LANE)), full((nlt, 1, LANE)), full((1, CW)), full((1, CW))],
        out_specs=pl.BlockSpec((tt, CW), lambda i, f, l: (i, 0)),
        scratch_shapes=[pltpu.VMEM((SUBLANE, nlt, tt + 2 * CONV_HALO, LANE), F32),
                        pltpu.VMEM((nlt, tt, LANE), F32)],
    )
    return pl.pallas_call(
        functools.partial(_conv_kernel, CW=CW, K=K),
        grid_spec=grid_spec,
        out_shape=jax.ShapeDtypeStruct((rows, CW), BF16),
        compiler_params=_params("parallel"),
        name="conv_module",
    )(first, last, cv, cv, cv, p["conv_w"].reshape(K, nlt, 1, LANE), p["conv_b"].reshape(nlt, 1, LANE),
      p["conv_ln_g"].reshape(1, CW), p["conv_ln_b"].reshape(1, CW))


def _merge_kernel(a_ref, r_ref, c_ref, wa_ref, wr_ref, wc_ref, g0_ref, g1_ref, g2_ref, b0_ref, b1_ref, b2_ref, o_ref):
    m = _sigmoid(g0_ref[...] + b0_ref[...]) * jnp.dot(a_ref[...], wa_ref[...], preferred_element_type=F32)
    m = m + _sigmoid(g1_ref[...] + b1_ref[...]) * jnp.dot(r_ref[...], wr_ref[...], preferred_element_type=F32)
    m = m + _sigmoid(g2_ref[...] + b2_ref[...]) * jnp.dot(c_ref[...], wc_ref[...], preferred_element_type=F32)
    o_ref[...] = m.astype(o_ref.dtype)


def _merge(att, rwo, cvo, gt, wa, wr, wc, b_gate, rows):
    D = wa.shape[1]
    tm = _tile(rows, 1024)
    tn = _tile(D, 512)
    nj = D // tn
    xin = lambda w: pl.BlockSpec((tm, w), lambda j, i: (i, 0))
    win = lambda w: pl.BlockSpec((w, tn), lambda j, i: (0, j))
    gate = lambda k: pl.BlockSpec((tm, tn), lambda j, i: (i, k * nj + j))
    bias = lambda k: pl.BlockSpec((1, tn), lambda j, i: (0, k * nj + j))
    bg = b_gate.reshape(1, 3 * D)
    return pl.pallas_call(
        _merge_kernel,
        grid=(nj, rows // tm),
        in_specs=[xin(att.shape[1]), xin(rwo.shape[1]), xin(cvo.shape[1]),
                  win(wa.shape[0]), win(wr.shape[0]), win(wc.shape[0]),
                  gate(0), gate(1), gate(2), bias(0), bias(1), bias(2)],
        out_specs=pl.BlockSpec((tm, tn), lambda j, i: (i, j)),
        out_shape=jax.ShapeDtypeStruct((rows, D), BF16),
        compiler_params=_params("parallel", "parallel"),
        name="branch_merge",
    )(att, rwo, cvo, wa, wr, wc, gt, gt, gt, bg, bg, bg)


def _outproj_kernel(m_ref, w_ref, xl_ref, xt_ref, gm_ref, sh_ref, sc_ref, lg_ref, lb_ref, wrt_ref, brt_ref,
                    x_o, h_o, lgt_o, *, alpha, nlt):
    y = jnp.dot(m_ref[...], w_ref[...], preferred_element_type=F32)
    x = jnp.where(pl.program_id(0) < nlt, xl_ref[...], xt_ref[...])
    x1 = _ln(alpha * x + gm_ref[0] * y) * lg_ref[...] + lb_ref[...]
    x_o[...] = x1
    h2 = _ln(x1) * (1.0 + sc_ref[0]) + sh_ref[0]
    half = h2.shape[1] // 2
    bits = lambda t: pltpu.bitcast(t.astype(BF16).astype(F32), jnp.uint32)
    h_o[...] = bits(h2[:, :half]) | lax.shift_right_logical(bits(h2[:, half:]), jnp.uint32(16))
    hi = h2.astype(BF16)
    lo = (h2 - hi.astype(F32)).astype(BF16)
    r1 = jnp.dot(hi, wrt_ref[...], preferred_element_type=F32)
    r2 = jnp.dot(lo, wrt_ref[:, :LANE], preferred_element_type=F32)
    lgt_o[...] = r1[:, :LANE] + r1[:, LANE:] + r2 + brt_ref[...]


def _outproj(m, w_out, src, mods3, ln_g, ln_b, w_rt, b_rt, rows, modrow, tm, alpha):
    D = w_out.shape[0]
    full = lambda shape: pl.BlockSpec(shape, lambda i: (0,) * len(shape))
    rowb = lambda w: pl.BlockSpec((tm, w), lambda i: (i, 0))
    mod = lambda blk: pl.BlockSpec((1, 1, D), lambda i: (modrow(i), 0, blk))
    xarrs, xspecs = _stream_specs(src, tm, D)
    return pl.pallas_call(
        functools.partial(_outproj_kernel, alpha=alpha, nlt=src[2]),
        grid=(rows // tm,),
        in_specs=[rowb(D), full((D, D))] + xspecs + [mod(2), mod(3), mod(4), full((1, D)), full((1, D)),
                                                     full((D, 2 * LANE)), full((1, LANE))],
        out_specs=[rowb(D), rowb(D // 2), rowb(LANE)],
        out_shape=[jax.ShapeDtypeStruct((rows, D), F32), jax.ShapeDtypeStruct((rows, D // 2), jnp.uint32),
                   jax.ShapeDtypeStruct((rows, LANE), F32)],
        compiler_params=_params("parallel"),
        name="out_proj",
    )(m, w_out, *xarrs, mods3, mods3, mods3, ln_g.reshape(1, D), ln_b.reshape(1, D), w_rt, b_rt)


def _route_kernel(l_ref, tri_ref, id_o, w_o, cnt_o, cnt_ref):
    lg = l_ref[...]
    lane = lax.broadcasted_iota(jnp.int32, lg.shape, 1)
    neg = -jnp.inf
    big = jnp.int32(2 ** 30)
    isg = jnp.logical_and(lane >= N_EXPERTS, lane < N_EXPERTS + N_GROUPS)
    gl = jnp.where(isg, lg, neg)
    gmax = jnp.max(gl, -1, keepdims=True)
    grp = jnp.min(jnp.where(gl == gmax, lane - N_EXPERTS, big), -1, keepdims=True)
    grp_w = 1.0 / jnp.sum(jnp.exp(gl - gmax), -1, keepdims=True)
    ing = jnp.logical_and(lane < N_EXPERTS, jnp.right_shift(lane, int(math.log2(EXPERTS_PER_GROUP))) == grp)
    el = jnp.where(ing, lg, neg)
    e1 = jnp.max(el, -1, keepdims=True)
    i1 = jnp.min(jnp.where(el == e1, lane, big), -1, keepdims=True)
    psum = jnp.sum(jnp.exp(el - e1), -1, keepdims=True)
    el2 = jnp.where(lane == i1, neg, el)
    e2 = jnp.max(el2, -1, keepdims=True)
    i2 = jnp.min(jnp.where(el2 == e2, lane, big), -1, keepdims=True)
    p1 = 1.0 / psum
    p2 = jnp.exp(e2 - e1) / psum
    tot = p1 + p2
    w_o[...] = jnp.where(lane == 0, grp_w * p1 / tot, jnp.where(lane == 1, grp_w * p2 / tot, 0.0))

    @pl.when(pl.program_id(0) == 0)
    def _():
        cnt_ref[...] = jnp.zeros_like(cnt_ref)

    hit = jnp.logical_or(lane == i1, lane == i2)
    before = cnt_ref[...] + jnp.dot(tri_ref[...], jnp.where(hit, 1.0, 0.0).astype(BF16), preferred_element_type=F32)
    r1 = jnp.sum(jnp.where(lane == i1, before, 0.0), -1, keepdims=True).astype(jnp.int32)
    r2 = jnp.sum(jnp.where(lane == i2, before, 0.0), -1, keepdims=True).astype(jnp.int32)
    id_o[...] = jnp.where(lane == 0, i1, jnp.where(lane == 1, i2, jnp.where(lane == 2, r1, jnp.where(lane == 3, r2, 0))))
    cnt_ref[...] = cnt_ref[...] + jnp.sum(jnp.where(hit, 1.0, 0.0), axis=0, keepdims=True)
    cnt_o[...] = cnt_ref[...]


def _route(logits):
    n = logits.shape[0]
    tm = _tile(n, 1024)
    tri = (jnp.arange(tm)[None, :] < jnp.arange(tm)[:, None]).astype(BF16)
    blk = lambda: pl.BlockSpec((tm, LANE), lambda i: (i, 0))
    return pl.pallas_call(
        _route_kernel,
        grid=(n // tm,),
        in_specs=[blk(), pl.BlockSpec((tm, tm), lambda i: (0, 0))],
        out_specs=[blk(), blk(), pl.BlockSpec((1, LANE), lambda i: (0, 0))],
        out_shape=[jax.ShapeDtypeStruct((n, LANE), jnp.int32), jax.ShapeDtypeStruct((n, LANE), F32),
                   jax.ShapeDtypeStruct((1, LANE), F32)],
        scratch_shapes=[pltpu.VMEM((1, LANE), F32)],
        compiler_params=_params("arbitrary"),
        name="route",
    )(logits, tri)


def _row_gather(tok_ref, h_hbm, xbuf, sem, slot, start, unrolled=False):
    if not start:
        pltpu.make_async_copy(h_hbm.at[pl.ds(0, MOE_BLOCK)], xbuf.at[slot], sem.at[slot]).wait()
        return

    def body(r, carry):
        pltpu.make_async_copy(h_hbm.at[pl.ds(tok_ref[0, 0, r], 1)], xbuf.at[slot, pl.ds(r, 1)], sem.at[slot]).start()
        return carry

    if unrolled:
        for r in range(MOE_BLOCK):
            body(r, 0)
    else:
        lax.fori_loop(0, MOE_BLOCK, body, 0, unroll=8)


def _moe_kernel(be_ref, act_ref, cur_ref, nxt_ref, h_hbm, wg_ref, wu_ref, wd_ref, o_ref, wgb, wub, wdb, xbuf, sem):
    i = pl.program_id(0)
    nb = pl.num_programs(0)
    slot = i % 2
    active = act_ref[i] > 0

    @pl.when(jnp.logical_and(i == 0, active))
    def _():
        _row_gather(cur_ref, h_hbm, xbuf, sem, 0, start=True)

    @pl.when(jnp.logical_or(i == 0, be_ref[i] != be_ref[jnp.maximum(i - 1, 0)]))
    def _():
        wgb[...] = wg_ref[0, 0].astype(BF16)
        wub[...] = wu_ref[0, 0].astype(BF16)
        wdb[...] = wd_ref[0, 0].astype(BF16)

    @pl.when(active)
    def _():
        _row_gather(None, h_hbm, xbuf, sem, slot, start=False)
        _row_gather(nxt_ref, h_hbm, xbuf, sem, 1 - slot, start=True, unrolled=True)
        w = xbuf[slot]
        xa = pltpu.bitcast(w & jnp.uint32(0xFFFF0000), F32)
        xb = pltpu.bitcast(lax.shift_left(w, jnp.uint32(16)), F32)
        x = jnp.concatenate([xa, xb], axis=1).astype(BF16)
        gte = jnp.dot(x, wgb[...], preferred_element_type=F32)
        up = jnp.dot(x, wub[...], preferred_element_type=F32)
        h = (gte * _sigmoid(gte) * up).astype(BF16)
        o_ref[...] = jnp.dot(h, wdb[...], preferred_element_type=F32)

    @pl.when(jnp.logical_not(active))
    def _():
        @pl.when(jnp.logical_and(i > 0, act_ref[jnp.maximum(i - 1, 0)] > 0))
        def _():
            _row_gather(None, h_hbm, xbuf, sem, slot, start=False)

        o_ref[...] = jnp.zeros_like(o_ref)


def _moe_ffn(hpk, slot_tok, block_e, active, wg, wu, wd, layer):
    D = 2 * hpk.shape[1]
    FF = wg.shape[3]
    nb = slot_tok.shape[0] // MOE_BLOCK
    once = pl.Buffered(1)
    toks = slot_tok.reshape(nb, 1, MOE_BLOCK)
    grid_spec = pltpu.PrefetchScalarGridSpec(
        num_scalar_prefetch=2,
        grid=(nb,),
        in_specs=[pl.BlockSpec((1, 1, MOE_BLOCK), lambda i, be, ac: (i, 0, 0), memory_space=pltpu.SMEM),
                  pl.BlockSpec((1, 1, MOE_BLOCK), lambda i, be, ac: (jnp.minimum(i + 1, nb - 1), 0, 0),
                               memory_space=pltpu.SMEM),
                  pl.BlockSpec(memory_space=pl.ANY),
                  pl.BlockSpec((1, 1, D, FF), lambda i, be, ac: (layer, be[i], 0, 0), pipeline_mode=once),
                  pl.BlockSpec((1, 1, D, FF), lambda i, be, ac: (layer, be[i], 0, 0), pipeline_mode=once),
                  pl.BlockSpec((1, 1, FF, D), lambda i, be, ac: (layer, be[i], 0, 0), pipeline_mode=once)],
        out_specs=pl.BlockSpec((MOE_BLOCK, D), lambda i, be, ac: (i, 0)),
        scratch_shapes=[pltpu.VMEM((D, FF), BF16), pltpu.VMEM((D, FF), BF16), pltpu.VMEM((FF, D), BF16),
                        pltpu.VMEM((2, MOE_BLOCK, D // 2), jnp.uint32), pltpu.SemaphoreType.DMA((2,))],
    )
    return pl.pallas_call(
        _moe_kernel,
        grid_spec=grid_spec,
        out_shape=jax.ShapeDtypeStruct((nb * MOE_BLOCK, D), F32),
        compiler_params=_params("arbitrary"),
        name="moe_ffn",
    )(block_e, active, toks, toks, hpk, wg, wu, wd)


def _dispatch(ids, rank, counts):
    n = ids.shape[0]
    A = 2 * n
    padded = (counts + MOE_BLOCK - 1) // MOE_BLOCK * MOE_BLOCK
    pad_end = jnp.cumsum(padded)
    pad_start = pad_end - padded
    start = jnp.cumsum(counts) - counts
    dest = jnp.take(pad_start, ids, mode="clip") + rank
    nb = -(-A // MOE_BLOCK) + N_EXPERTS + 1
    blk_start = jnp.arange(nb, dtype=jnp.int32) * MOE_BLOCK
    block_e = jnp.minimum(jnp.sum(pad_end[None, :] <= blk_start[:, None], axis=1), N_EXPERTS - 1).astype(jnp.int32)
    active = (blk_start < pad_end[-1]).astype(jnp.int32)
    order = jnp.argsort(ids.reshape(-1))
    slot_e = jnp.repeat(block_e, MOE_BLOCK)
    slot_r = jnp.arange(nb * MOE_BLOCK, dtype=jnp.int32) - jnp.take(pad_start, slot_e)
    src = jnp.clip(jnp.take(start, slot_e) + slot_r, 0, A - 1)
    slot_tok = jnp.where(slot_r < jnp.take(counts, slot_e), jnp.take(order, src) // 2, 0).astype(jnp.int32)
    return slot_tok, dest.astype(jnp.int32), block_e, active


def _final_kernel(cur_ref, nxt_ref, x_ref, yb_hbm, w_ref, gm_ref, lg_ref, lb_ref, o_ref, ybuf, sem, *, alpha, tm):
    i = pl.program_id(0)
    slot = i % 2

    def gather(tok_ref, s, unrolled):
        def body(r, carry):
            for k in range(2):
                pltpu.make_async_copy(yb_hbm.at[pl.ds(tok_ref[0, 0, k * tm + r], 1)], ybuf.at[s, k, pl.ds(r, 1)],
                                      sem.at[s]).start()
            return carry

        if unrolled:
            for r in range(tm):
                body(r, 0)
        else:
            lax.fori_loop(0, tm, body, 0, unroll=8)

    @pl.when(i == 0)
    def _():
        gather(cur_ref, 0, False)

    @pl.when(i + 1 < pl.num_programs(0))
    def _():
        gather(nxt_ref, 1 - slot, True)

    for k in range(2):
        pltpu.make_async_copy(yb_hbm.at[pl.ds(0, tm)], ybuf.at[slot, k], sem.at[slot]).wait()
    w = w_ref[...]
    f = w[:, 0:1] * ybuf[slot, 0] + w[:, 1:2] * ybuf[slot, 1]
    o_ref[...] = _ln(alpha * x_ref[...] + gm_ref[0] * f) * lg_ref[...] + lb_ref[...]


def _final(x1, yb, dest, wts, mods3, ln_g, ln_b, rows, modrow, tm, alpha):
    D = x1.shape[1]
    nt = rows // tm
    toks = jnp.swapaxes(dest.reshape(nt, tm, 2), 1, 2).reshape(nt, 1, 2 * tm)
    rowb = lambda w: pl.BlockSpec((tm, w), lambda i: (i, 0))
    full = lambda shape: pl.BlockSpec(shape, lambda i: (0,) * len(shape))
    return pl.pallas_call(
        functools.partial(_final_kernel, alpha=alpha, tm=tm),
        grid=(nt,),
        in_specs=[pl.BlockSpec((1, 1, 2 * tm), lambda i: (i, 0, 0), memory_space=pltpu.SMEM),
                  pl.BlockSpec((1, 1, 2 * tm), lambda i: (jnp.minimum(i + 1, nt - 1), 0, 0), memory_space=pltpu.SMEM),
                  rowb(D), pl.BlockSpec(memory_space=pl.ANY), rowb(LANE),
                  pl.BlockSpec((1, 1, D), lambda i: (modrow(i), 0, 5)), full((1, D)), full((1, D))],
        out_specs=rowb(D),
        out_shape=jax.ShapeDtypeStruct((rows, D), F32),
        scratch_shapes=[pltpu.VMEM((2, 2, tm, D), F32), pltpu.SemaphoreType.DMA((2,))],
        compiler_params=_params("arbitrary"),
        name="ffn_residual",
    )(toks, toks, x1, yb, wts, mods3, ln_g.reshape(1, D), ln_b.reshape(1, D))


def _rope_tables(S):
    rows = S // GRID_W
    row = jnp.repeat(jnp.arange(rows, dtype=F32), GRID_W)
    col = jnp.tile(jnp.arange(GRID_W, dtype=F32), rows)
    nf = ATT_HEAD_DIM // 4
    inv = ROPE_THETA ** (-jnp.arange(nf, dtype=F32) / nf)
    ar, ac = row[:, None] * inv, col[:, None] * inv
    cos = jnp.concatenate([jnp.cos(ar), jnp.cos(ar), jnp.cos(ac), jnp.cos(ac)], -1)
    sin = jnp.concatenate([-jnp.sin(ar), jnp.sin(ar), -jnp.sin(ac), jnp.sin(ac)], -1)
    return cos, sin


def kernel(x, c, ctx, c_ctx, w_mod, b_mod, w_in, b_gate, q_norm, k_norm, w_att_o, rwkv_mu, rwkv_w0, rwkv_w2, rwkv_a0, rwkv_a2, rwkv_g2, rwkv_k_k, rwkv_k_a, rwkv_r_k, rwkv_ln_g, rwkv_ln_b, w_rwkv_o, conv_w, conv_b, conv_ln_g, conv_ln_b, w_conv_o, w_out, ln1_g, ln1_b, w_group, b_group, w_router, b_router, w_e_gate, w_e_up, w_e_down, ln2_g, ln2_b):
    B, S, D = x.shape
    Tc = ctx.shape[1]
    depth = w_mod.shape[0]
    NL, NC = B * S, B * Tc
    N = NL + NC
    alpha = (2 * depth) ** 0.25
    att_w = w_att_o.shape[1]
    heads = att_w // ATT_HEAD_DIM
    rw_in = rwkv_mu.shape[-1]
    conv_cw = w_conv_o.shape[1]
    kv_w = (w_in.shape[-1] - att_w - rw_in - 2 * conv_cw - 3 * D) // 2
    assert kv_w == heads // ATT_GROUP * ATT_HEAD_DIM
    o_rw = att_w + 2 * kv_w
    o_cv = o_rw + rw_in
    o_gt = o_cv + 2 * conv_cw

    tmod = _tile(math.gcd(S, Tc), 256)
    tps = S // tmod
    nlt = NL // tmod
    modrow = lambda i: jnp.where(i < nlt, i // tps, B)
    cos_t, sin_t = _rope_tables(S)

    R = -(-(B + 1) // SUBLANE) * SUBLANE
    cv = jnp.zeros((R, D), F32).at[:B].set(c).at[B].set(c_ctx)
    src = (x.reshape(NL, D), ctx.reshape(NC, D), nlt, 0)

    for l in range(depth):
        last = l == depth - 1
        rows = NL if last else N
        p = dict(rwkv_mu=rwkv_mu[l], rwkv_w0=rwkv_w0[l], rwkv_w2=rwkv_w2[l], rwkv_a0=rwkv_a0[l], rwkv_a2=rwkv_a2[l],
                 rwkv_g2=rwkv_g2[l], rwkv_k_k=rwkv_k_k[l], rwkv_k_a=rwkv_k_a[l], rwkv_r_k=rwkv_r_k[l],
                 rwkv_ln_g=rwkv_ln_g[l], rwkv_ln_b=rwkv_ln_b[l], conv_w=conv_w[l], conv_b=conv_b[l],
                 conv_ln_g=conv_ln_g[l], conv_ln_b=conv_ln_b[l])
        mods3 = _adaln(cv, w_mod, b_mod, l).reshape(R, 1, 6 * D)
        wl = w_in[l]
        h = _lnmod(src, mods3, 0, 1, N, modrow, tmod)
        qkv = _matmul(h, wl[:, :o_rw].astype(BF16), 512)
        rw = _matmul(h, wl[:, o_rw:o_cv].astype(BF16), 1152)
        cvx = _matmul(h, wl[:, o_cv:o_gt].astype(BF16), 1024, rows)
        gt = _matmul(h, wl[:, o_gt:].astype(BF16), 1024, rows)

        qn, kn = q_norm[l].reshape(1, LANE), k_norm[l].reshape(1, LANE)
        att = _attention(qkv, qn, kn, cos_t, sin_t, B=B, S=S, Tc=Tc, heads=heads, latent_queries=True)
        if not last:
            att_c = _attention(qkv, qn, kn, cos_t, sin_t, B=B, S=S, Tc=Tc, heads=heads, latent_queries=False)
            att = jnp.concatenate([att, att_c], 0)

        r, v, kk, kd, lw, bb, g, bonus = _rwprep(rw, p, B=B, S=S, Tc=Tc)
        Am, Bm, Qm, Y0 = _scanprep(r, v, kk, kd, lw, bb)
        y2 = _scanserial(Am, Bm, Qm, Y0, B=B, S=S, Tc=Tc)
        rwo = _rwout(y2, bonus, g, p, rows)
        cvo = _conv(cvx, p, B=B, S=S, Tc=Tc, rows=rows)

        m = _merge(att, rwo, cvo, gt, w_att_o[l].astype(BF16), w_rwkv_o[l].astype(BF16), w_conv_o[l].astype(BF16),
                   b_gate[l], rows)
        w_rt = jnp.zeros((D, LANE), F32).at[:, :N_EXPERTS].set(w_router[l]).at[:, N_EXPERTS:N_EXPERTS + N_GROUPS].set(w_group[l])
        b_rt = jnp.zeros((1, LANE), F32).at[0, :N_EXPERTS].set(b_router[l]).at[0, N_EXPERTS:N_EXPERTS + N_GROUPS].set(b_group[l])
        w_rt_hi = w_rt.astype(BF16)
        w_rt = jnp.concatenate([w_rt_hi, (w_rt - w_rt_hi.astype(F32)).astype(BF16)], 1)
        x1, h2, logits = _outproj(m, w_out[l].astype(BF16), src, mods3, ln1_g[l], ln1_b[l], w_rt, b_rt, rows, modrow,
                                  tmod, alpha)

        ids128, wts128, cnt128 = _route(logits)
        slot_tok, dest, block_e, active = _dispatch(ids128[:, :2], ids128[:, 2:4], cnt128[0, :N_EXPERTS].astype(jnp.int32))
        yb = _moe_ffn(h2, slot_tok, block_e, active, w_e_gate, w_e_up, w_e_down, l)
        xa = _final(x1, yb, dest, wts128, mods3, ln2_g[l], ln2_b[l], rows, modrow, tmod, alpha)
        src = (xa, xa, nlt, nlt)

    return xa[:NL].reshape(B, S, D)
```
